```python
import math
import jax, jax.numpy as jnp
from jax import lax
import numpy as np

D_MODEL = 2048
BATCH = 4
SEQ = 2048
DEPTH = 1
DEC_BATCH = 128
DEC_SEQ = 1
PAST_LEN = 16384
PAGE_SIZE = 128

MIX_WIDTH = D_MODEL
RET_HEADS = 4
RET_HEAD_DIM = (MIX_WIDTH // 2) // RET_HEADS
RET_WIDTH = RET_HEADS * RET_HEAD_DIM
MLSTM_HEADS = 4
MLSTM_HEAD_DIM = (MIX_WIDTH // 2) // MLSTM_HEADS
MLSTM_WIDTH = MLSTM_HEADS * MLSTM_HEAD_DIM
D_FF = 4 * D_MODEL
PLE_DIM = 256
CHUNK = 128
ROPE_BASE = 10000.0
LN_EPS = 1e-5
GN_EPS = 1e-6
DEEPNORM_ALPHA = (2 * DEPTH) ** 0.25
DEEPNORM_BETA = (8 * DEPTH) ** -0.25
IN_COLS = 4 * RET_WIDTH + 4 * MLSTM_WIDTH + 2 * MLSTM_HEADS
SPLITS = tuple(int(s) for s in np.cumsum([RET_WIDTH] * 4 + [MLSTM_WIDTH] * 4 + [MLSTM_HEADS]))

kernel_name = "hymba_retention_mlstm_deepnorm_step"

F32 = jnp.float32


def layer_norm(x, g, b):
    xf = x.astype(F32)
    mu = jnp.mean(xf, axis=-1, keepdims=True)
    var = jnp.mean(jnp.square(xf - mu), axis=-1, keepdims=True)
    y = (xf - mu) * lax.rsqrt(var + LN_EPS) * g.astype(F32) + b.astype(F32)
    return y.astype(x.dtype)


def head_norm(h, w):
    B, T, H, Dh = h.shape
    mu = jnp.mean(h, axis=-1, keepdims=True)
    var = jnp.mean(jnp.square(h - mu), axis=-1, keepdims=True)
    y = (h - mu) * lax.rsqrt(var + GN_EPS)
    return y.reshape(B, T, H * Dh) * w.astype(F32)


def rotary(x, pos):
    half = x.shape[-1] // 2
    inv = ROPE_BASE ** (-jnp.arange(half, dtype=F32) * 2.0 / x.shape[-1])
    ang = pos[:, None] * inv[None, :]
    cos = jnp.cos(ang)[None, :, None, :]
    sin = jnp.sin(ang)[None, :, None, :]
    x1, x2 = x[..., :half], x[..., half:]
    return jnp.concatenate([x1 * cos - x2 * sin, x1 * sin + x2 * cos], axis=-1)


def to_chunks(a, L):
    B, H, T = a.shape[:3]
    a = a.reshape((B, H, T // L, L) + a.shape[3:])
    return jnp.moveaxis(a, 2, 0)


def from_chunks(a):
    a = jnp.moveaxis(a, 0, 2)
    B, H, NC, L = a.shape[:4]
    return a.reshape((B, H, NC * L) + a.shape[4:])


def retention(q, k, v, S0):
    T = q.shape[2]
    L = math.gcd(T, CHUNK)
    lg = jnp.log(1.0 - 2.0 ** (-5.0 - jnp.arange(RET_HEADS, dtype=F32)))
    t = jnp.arange(L, dtype=F32)
    causal = t[:, None] >= t[None, :]
    intra = jnp.exp(jnp.where(causal, (t[:, None] - t[None, :]) * lg[:, None, None], -jnp.inf))
    q_decay = jnp.exp(lg[:, None] * (t + 1.0))[:, :, None]
    k_decay = jnp.exp(lg[:, None] * (L - 1.0 - t))[:, :, None]
    state_decay = jnp.exp(lg * L)[:, None, None]

    def step(S, qkv):
        qc, kc, vc = qkv
        sc = jnp.einsum('bhtd,bhsd->bhts', qc, kc) * intra
        o = jnp.einsum('bhts,bhsv->bhtv', sc, vc) + jnp.einsum('bhtd,bhdv->bhtv', qc, S) * q_decay
        S = S * state_decay + jnp.einsum('bhsd,bhsv->bhdv', kc * k_decay, vc)
        return S, o

    S, o = lax.scan(step, S0, (to_chunks(q, L), to_chunks(k, L), to_chunks(v, L)))
    return from_chunks(o), S


def mlstm(q, k, v, ig, lf, C0, n0, m0):
    T = q.shape[2]
    L = math.gcd(T, CHUNK)
    t = jnp.arange(L)
    causal = t[:, None] >= t[None, :]

    def step(carry, inp):
        C, n, m = carry
        qc, kc, vc, ic, fc = inp
        b = jnp.cumsum(fc, axis=-1)
        dlog = jnp.where(causal, b[..., :, None] - b[..., None, :] + ic[..., None, :], -jnp.inf)
        inter = b + m[..., None]
        mt = jnp.maximum(inter, jnp.max(dlog, axis=-1))
        dw = jnp.exp(dlog - mt[..., None])
        iw = jnp.exp(inter - mt)
        sc = jnp.einsum('bhtd,bhsd->bhts', qc, kc) * dw
        num = jnp.einsum('bhts,bhsv->bhtv', sc, vc) + iw[..., None] * jnp.einsum('bhtd,bhdv->bhtv', qc, C)
        den = jnp.sum(sc, axis=-1) + iw * jnp.einsum('bhtd,bhd->bht', qc, n)
        h = num / jnp.maximum(jnp.abs(den), jnp.exp(-mt))[..., None]
        m_new = mt[..., -1]
        sw = jnp.exp(b[..., -1:] - b + ic - m_new[..., None])
        sd = jnp.exp(b[..., -1] + m - m_new)
        C = sd[..., None, None] * C + jnp.einsum('bhs,bhsd,bhsv->bhdv', sw, kc, vc)
        n = sd[..., None] * n + jnp.einsum('bhs,bhsd->bhd', sw, kc)
        return (C, n, m_new), h

    (C, n, m), h = lax.scan(step, (C0, n0, m0),
                            (to_chunks(q, L), to_chunks(k, L), to_chunks(v, L),
                             to_chunks(ig, L), to_chunks(lf, L)))
    return from_chunks(h), C, n, m


def hybrid_layer(x, p, S_ret, C, n, m, pos0, w_in, b_gate, ret_gn_w, mlstm_gn_w, w_out,
                 ln1_g, ln1_b, w_ff1, w_ff2, w_pe, w_pe_gate, ln2_g, ln2_b):
    B, T, _ = x.shape
    proj = jnp.einsum('btd,de->bte', x, w_in).astype(F32)
    rq, rk, rv, rg, mq, mk, mv, mo, mi, mf = jnp.split(proj, SPLITS, axis=-1)
    pos = pos0 + jnp.arange(T, dtype=F32)

    rq = rotary(rq.reshape(B, T, RET_HEADS, RET_HEAD_DIM), pos)
    rk = rotary(rk.reshape(B, T, RET_HEADS, RET_HEAD_DIM), pos) * (RET_HEAD_DIM ** -0.5)
    rv = rv.reshape(B, T, RET_HEADS, RET_HEAD_DIM)
    ret_o, S_new = retention(rq.transpose(0, 2, 1, 3), rk.transpose(0, 2, 1, 3),
                             rv.transpose(0, 2, 1, 3), S_ret.astype(F32))
    ret_o = head_norm(ret_o.transpose(0, 2, 1, 3), ret_gn_w) * jax.nn.silu(rg)

    bg = b_gate.astype(F32)
    mq = mq.reshape(B, T, MLSTM_HEADS, MLSTM_HEAD_DIM).transpose(0, 2, 1, 3)
    mk = (mk * (MLSTM_HEAD_DIM ** -0.5)).reshape(B, T, MLSTM_HEADS, MLSTM_HEAD_DIM).transpose(0, 2, 1, 3)
    mv = mv.reshape(B, T, MLSTM_HEADS, MLSTM_HEAD_DIM).transpose(0, 2, 1, 3)
    ig = (mi + bg[:MLSTM_HEADS]).transpose(0, 2, 1)
    lf = jax.nn.log_sigmoid(mf + bg[MLSTM_HEADS:]).transpose(0, 2, 1)
    h, C_new, n_new, m_new = mlstm(mq, mk, mv, ig, lf, C.astype(F32), n.astype(F32), m.astype(F32))
    m_o = head_norm(h.transpose(0, 2, 1, 3), mlstm_gn_w) * jax.nn.sigmoid(mo)

    mix = jnp.einsum('bte,ed->btd', jnp.concatenate([ret_o, m_o], axis=-1).astype(x.dtype), w_out)
    x1 = layer_norm(DEEPNORM_ALPHA * x + mix, ln1_g, ln1_b)

    ff = jnp.einsum('btf,fd->btd', jnp.square(jax.nn.relu(jnp.einsum('btd,df->btf', x1, w_ff1))), w_ff2)
    pe = jnp.einsum('btp,pd->btd', p, w_pe) * jax.nn.sigmoid(jnp.einsum('btd,de->bte', x1, w_pe_gate))
    x2 = layer_norm(DEEPNORM_ALPHA * x1 + ff + pe, ln2_g, ln2_b)
    return x2, S_new, C_new, n_new, m_new


def setup_inputs(seed: int = 0) -> dict:
    key = jax.random.key(seed)
    ks = jax.random.split(key, 24)
    nrm = jax.random.normal
    H, Dh = MLSTM_HEADS, MLSTM_HEAD_DIM
    f_bias = jnp.broadcast_to(jnp.linspace(3.0, 6.0, H, dtype=F32), (DEPTH, H)) + 0.1 * nrm(ks[20], (DEPTH, H))
    i_bias = 0.1 * nrm(ks[21], (DEPTH, H))
    return {
        'x_prompt': nrm(ks[0], (BATCH, SEQ, D_MODEL), F32),
        'x_sample': nrm(ks[1], (DEC_BATCH, DEC_SEQ, D_MODEL), F32),
        'state_ret': 0.5 * nrm(ks[2], (DEPTH, DEC_BATCH, RET_HEADS, RET_HEAD_DIM, RET_HEAD_DIM), F32),
        'state_mlstm_C': 0.5 * nrm(ks[3], (DEPTH, DEC_BATCH, H, Dh, Dh), F32),
        'state_mlstm_n': 0.5 * nrm(ks[4], (DEPTH, DEC_BATCH, H, Dh), F32),
        'state_mlstm_m': jax.random.uniform(ks[5], (DEPTH, DEC_BATCH, H), F32, 0.0, 3.0),
        'p_prompt': nrm(ks[6], (DEPTH, BATCH, SEQ, PLE_DIM), F32),
        'p_sample': nrm(ks[7], (DEPTH, DEC_BATCH, DEC_SEQ, PLE_DIM), F32),
        'w_in': nrm(ks[8], (DEPTH, D_MODEL, IN_COLS), F32) * D_MODEL ** -0.5,
        'b_gate': jnp.concatenate([i_bias, f_bias], axis=-1),
        'ret_gn_w': 1.0 + 0.02 * nrm(ks[9], (DEPTH, RET_WIDTH), F32),
        'mlstm_gn_w': 1.0 + 0.02 * nrm(ks[10], (DEPTH, MLSTM_WIDTH), F32),
        'w_out': nrm(ks[11], (DEPTH, MIX_WIDTH, D_MODEL), F32) * MIX_WIDTH ** -0.5 * DEEPNORM_BETA,
        'ln1_g': 1.0 + 0.02 * nrm(ks[12], (DEPTH, D_MODEL), F32),
        'ln1_b': 0.02 * nrm(ks[13], (DEPTH, D_MODEL), F32),
        'w_ff1': nrm(ks[14], (DEPTH, D_MODEL, D_FF), F32) * D_MODEL ** -0.5,
        'w_ff2': nrm(ks[15], (DEPTH, D_FF, D_MODEL), F32) * D_FF ** -0.5 * DEEPNORM_BETA,
        'w_pe': nrm(ks[16], (DEPTH, PLE_DIM, D_MODEL), F32) * PLE_DIM ** -0.5 * DEEPNORM_BETA,
        'w_pe_gate': nrm(ks[17], (DEPTH, D_MODEL, D_MODEL), F32) * D_MODEL ** -0.5,
        'ln2_g': 1.0 + 0.02 * nrm(ks[18], (DEPTH, D_MODEL), F32),
        'ln2_b': 0.02 * nrm(ks[19], (DEPTH, D_MODEL), F32),
    }


def reference(x_prompt, x_sample, state_ret, state_mlstm_C, state_mlstm_n, state_mlstm_m,
              p_prompt, p_sample, w_in, b_gate, ret_gn_w, mlstm_gn_w, w_out, ln1_g, ln1_b,
              w_ff1, w_ff2, w_pe, w_pe_gate, ln2_g, ln2_b):
    Bp = x_prompt.shape[0]
    yp, ys = x_prompt, x_sample
    rS_p, C_p, n_p, m_p = [], [], [], []
    rS_s, C_s, n_s, m_s = [], [], [], []
    for i in range(DEPTH):
        w = (w_in[i], b_gate[i], ret_gn_w[i], mlstm_gn_w[i], w_out[i], ln1_g[i], ln1_b[i],
             w_ff1[i], w_ff2[i], w_pe[i], w_pe_gate[i], ln2_g[i], ln2_b[i])
        S0 = jnp.zeros((Bp, RET_HEADS, RET_HEAD_DIM, RET_HEAD_DIM), F32)
        C0 = jnp.zeros((Bp, MLSTM_HEADS, MLSTM_HEAD_DIM, MLSTM_HEAD_DIM), F32)
        n0 = jnp.zeros((Bp, MLSTM_HEADS, MLSTM_HEAD_DIM), F32)
        m0 = jnp.zeros((Bp, MLSTM_HEADS), F32)
        yp, a, b, c, d = hybrid_layer(yp, p_prompt[i], S0, C0, n0, m0, 0, *w)
        rS_p.append(a); C_p.append(b); n_p.append(c); m_p.append(d)
        ys, a, b, c, d = hybrid_layer(ys, p_sample[i], state_ret[i], state_mlstm_C[i],
                                      state_mlstm_n[i], state_mlstm_m[i], PAST_LEN, *w)
        rS_s.append(a); C_s.append(b); n_s.append(c); m_s.append(d)
    return (yp, ys, jnp.stack(rS_p), jnp.stack(C_p), jnp.stack(n_p), jnp.stack(m_p),
            jnp.stack(rS_s), jnp.stack(C_s), jnp.stack(n_s), jnp.stack(m_s))
```

```python
import functools

import jax
import jax.numpy as jnp
import numpy as np
from jax import lax
from jax.experimental import pallas as pl
from jax.experimental.pallas import tpu as pltpu

F32 = jnp.float32
BF16 = jnp.bfloat16

D_MODEL = 2048
HEADS = 4
HEAD_DIM = 256
HALF = HEAD_DIM // 2
WIDTH = HEADS * HEAD_DIM
D_FF = 4 * D_MODEL
PLE_DIM = 256
CHUNK = 128
PAST_LEN = 16384
ROPE_BASE = 10000.0
LN_EPS = 1e-5
GN_EPS = 1e-6
DEPTH = 1
ALPHA = (2 * DEPTH) ** 0.25
QK_SCALE = HEAD_DIM ** -0.5
LANES = 128
VMEM_LIMIT = 56 * 1024 * 1024

RET_G = tuple(1.0 - 2.0 ** (-5.0 - h) for h in range(HEADS))
RET_LOG_G = tuple(float(np.log(np.float32(g))) for g in RET_G)
RET_STATE_DECAY = tuple(float(np.exp(np.float32(lg) * np.float32(CHUNK))) for lg in RET_LOG_G)

_NT = (((1,), (1,)), ((), ()))
_TN = (((0,), (0,)), ((), ()))


def _params(*sem):
    return pltpu.CompilerParams(dimension_semantics=sem, vmem_limit_bytes=VMEM_LIMIT)


def _sigmoid(x):
    return 1.0 / (1.0 + jnp.exp(-x))


def _log_sigmoid(x):
    return jnp.minimum(x, 0.0) - jnp.log(1.0 + jnp.exp(-jnp.abs(x)))


def _layer_norm(y, g, b):
    mu = jnp.mean(y, axis=-1, keepdims=True)
    yc = y - mu
    var = jnp.mean(jnp.square(yc), axis=-1, keepdims=True)
    return yc * lax.rsqrt(var + LN_EPS) * g + b


def _head_norm(o):
    mu = jnp.mean(o, axis=-1, keepdims=True)
    oc = o - mu
    var = jnp.mean(jnp.square(oc), axis=-1, keepdims=True)
    return oc * lax.rsqrt(var + GN_EPS)


def _inproj_kernel(x_ref, w_ref, cos_ref, sin_ref, o_ref, *, qkv):
    j = pl.program_id(0)
    acc = jnp.dot(x_ref[...], w_ref[...], preferred_element_type=F32)
    if not qkv:
        o_ref[...] = acc.astype(o_ref.dtype)
        return
    scale = jnp.where((j == 1) | (j == 4), QK_SCALE, 1.0).astype(F32)

    @pl.when(j < 2)
    def _():
        cos = cos_ref[...]
        sin = sin_ref[...]
        for h in range(HEADS):
            lo = h * HEAD_DIM
            x1 = acc[:, lo:lo + HALF]
            x2 = acc[:, lo + HALF:lo + HEAD_DIM]
            o_ref[:, lo:lo + HALF] = ((x1 * cos - x2 * sin) * scale).astype(o_ref.dtype)
            o_ref[:, lo + HALF:lo + HEAD_DIM] = ((x1 * sin + x2 * cos) * scale).astype(o_ref.dtype)

    @pl.when(j >= 2)
    def _():
        o_ref[...] = (acc * scale).astype(o_ref.dtype)


def _inproj(xb, wb, cos, sin, *, tm, qkv, out_dtype):
    m = xb.shape[0]
    n_groups = 6 if qkv else 2
    w_map = (lambda j, i: (0, j + j // 3)) if qkv else (lambda j, i: (0, 3 + 4 * j))
    n_pos = cos.shape[0] // tm
    return pl.pallas_call(
        functools.partial(_inproj_kernel, qkv=qkv),
        grid=(n_groups, m // tm),
        in_specs=[
            pl.BlockSpec((tm, D_MODEL), lambda j, i: (i, 0)),
            pl.BlockSpec((D_MODEL, WIDTH), w_map),
            pl.BlockSpec((tm, HALF), lambda j, i: (i % n_pos, 0)),
            pl.BlockSpec((tm, HALF), lambda j, i: (i % n_pos, 0)),
        ],
        out_specs=pl.BlockSpec((tm, WIDTH), lambda j, i: (i, j)),
        out_shape=jax.ShapeDtypeStruct((m, n_groups * WIDTH), out_dtype),
        compiler_params=_params("arbitrary", "arbitrary"),
        name="inproj_qkv" if qkv else "inproj_gates",
    )(xb, wb, cos, sin)


def _mgate_kernel(x_ref, w_ref, b_ref, g_ref, gt_ref):
    g = jnp.dot(x_ref[...], w_ref[...], preferred_element_type=F32) + b_ref[...]
    lane = lax.broadcasted_iota(jnp.int32, g.shape, 1)
    g = jnp.where((lane >= HEADS) & (lane < 2 * HEADS), _log_sigmoid(g), g)
    g_ref[...] = g
    gt_ref[...] = g.T[0:2 * HEADS, :]


def _mgate(xb, w8, b8, *, tm):
    m = xb.shape[0]
    return pl.pallas_call(
        _mgate_kernel,
        grid=(m // tm,),
        in_specs=[
            pl.BlockSpec((tm, D_MODEL), lambda i: (i, 0)),
            pl.BlockSpec((D_MODEL, LANES), lambda i: (0, 0)),
            pl.BlockSpec((1, LANES), lambda i: (0, 0)),
        ],
        out_specs=[
            pl.BlockSpec((tm, LANES), lambda i: (i, 0)),
            pl.BlockSpec((2 * HEADS, tm), lambda i: (0, i)),
        ],
        out_shape=[
            jax.ShapeDtypeStruct((m, LANES), F32),
            jax.ShapeDtypeStruct((2 * HEADS, m), F32),
        ],
        compiler_params=_params("arbitrary"),
        name="mlstm_gates",
    )(xb, w8, b8)


def _prompt_mixer_kernel(qkv_ref, gate_ref, g_ref, gt_ref, intra_ref, dec_ref, rgn_ref, mgn_ref,
                         mix_ref, s_ref, c_ref, n_ref, m_ref, m_scr):
    c = pl.program_id(1)

    @pl.when(c == 0)
    def _():
        s_ref[...] = jnp.zeros_like(s_ref)
        c_ref[...] = jnp.zeros_like(c_ref)
        n_ref[...] = jnp.zeros_like(n_ref)
        m_scr[...] = jnp.zeros_like(m_scr)

    ti = lax.broadcasted_iota(jnp.int32, (CHUNK, CHUNK), 0)
    si = lax.broadcasted_iota(jnp.int32, (CHUNK, CHUNK), 1)
    causal = ti >= si

    for h in range(HEADS):
        lo = h * HEAD_DIM
        q = qkv_ref[:, lo:lo + HEAD_DIM]
        k = qkv_ref[:, WIDTH + lo:WIDTH + lo + HEAD_DIM]
        v = qkv_ref[:, 2 * WIDTH + lo:2 * WIDTH + lo + HEAD_DIM]
        s_old = s_ref[0, 0, h]
        sc = lax.dot_general(q, k, _NT, preferred_element_type=F32) * intra_ref[h]
        o = (jnp.dot(sc.astype(BF16), v, preferred_element_type=F32)
             + jnp.dot(q, s_old.astype(BF16), preferred_element_type=F32) * dec_ref[:, h:h + 1])
        kd = (k.astype(F32) * dec_ref[:, HEADS + h:HEADS + h + 1]).astype(BF16)
        s_ref[0, 0, h] = (s_old * RET_STATE_DECAY[h]
                          + lax.dot_general(kd, v, _TN, preferred_element_type=F32))
        rg = gate_ref[:, lo:lo + HEAD_DIM]
        y = _head_norm(o) * rgn_ref[:, lo:lo + HEAD_DIM] * (rg * _sigmoid(rg))
        mix_ref[:, lo:lo + HEAD_DIM] = y.astype(mix_ref.dtype)

        q = qkv_ref[:, 3 * WIDTH + lo:3 * WIDTH + lo + HEAD_DIM]
        k = qkv_ref[:, 4 * WIDTH + lo:4 * WIDTH + lo + HEAD_DIM]
        v = qkv_ref[:, 5 * WIDTH + lo:5 * WIDTH + lo + HEAD_DIM]
        ig_row = gt_ref[h:h + 1, :]
        lf_row = gt_ref[HEADS + h:HEADS + h + 1, :]
        ig_col = g_ref[:, h:h + 1]
        lf_col = g_ref[:, HEADS + h:HEADS + h + 1]
        b_col = jnp.sum(jnp.where(causal, lf_row, 0.0), axis=1, keepdims=True)
        b_row = jnp.sum(jnp.where(ti <= si, lf_col, 0.0), axis=0, keepdims=True)
        m_prev = m_scr[h:h + 1, 0:1]
        dlog = jnp.where(causal, b_col - b_row + ig_row, -jnp.inf)
        inter = b_col + m_prev
        mt = jnp.maximum(inter, jnp.max(dlog, axis=1, keepdims=True))
        dw = jnp.exp(dlog - mt)
        iw = jnp.exp(inter - mt)
        c_old = c_ref[0, 0, h]
        n_old = n_ref[0, 0, h:h + 1, :]
        sc = lax.dot_general(q, k, _NT, preferred_element_type=F32) * dw
        num = (jnp.dot(sc.astype(BF16), v, preferred_element_type=F32)
               + iw * jnp.dot(q, c_old.astype(BF16), preferred_element_type=F32))
        den = (jnp.sum(sc, axis=1, keepdims=True)
               + iw * jnp.sum(q.astype(F32) * n_old, axis=1, keepdims=True))
        hid = num * (1.0 / jnp.maximum(jnp.abs(den), jnp.exp(-mt)))
        m_new = mt[CHUNK - 1:CHUNK, :]
        b_last = b_col[CHUNK - 1:CHUNK, :]
        sw = jnp.exp(b_last - b_col + ig_col - m_new)
        sd = jnp.exp(b_last + m_prev - m_new)
        ksw = k.astype(F32) * sw
        c_ref[0, 0, h] = sd * c_old + lax.dot_general(ksw.astype(BF16), v, _TN,
                                                      preferred_element_type=F32)
        n_ref[0, 0, h:h + 1, :] = sd * n_old + jnp.sum(ksw, axis=0, keepdims=True)
        m_scr[h:h + 1, :] = jnp.broadcast_to(m_new, (1, LANES))
        mo = gate_ref[:, WIDTH + lo:WIDTH + lo + HEAD_DIM]
        y = _head_norm(hid) * mgn_ref[:, lo:lo + HEAD_DIM] * _sigmoid(mo)
        mix_ref[:, WIDTH + lo:WIDTH + lo + HEAD_DIM] = y.astype(mix_ref.dtype)

    @pl.when(c == pl.num_programs(1) - 1)
    def _():
        m_ref[0] = m_scr[...]


def _prompt_mixer(qkv, gates, g_tok, g_t, intra, dec, rgn, mgn, *, batch, seq):
    nc = seq // CHUNK
    row = lambda b, c: (b * nc + c, 0)
    state = pl.BlockSpec((1, 1, HEADS, HEAD_DIM, HEAD_DIM), lambda b, c: (0, b, 0, 0, 0))
    return pl.pallas_call(
        _prompt_mixer_kernel,
        grid=(batch, nc),
        in_specs=[
            pl.BlockSpec((CHUNK, 6 * WIDTH), row),
            pl.BlockSpec((CHUNK, 2 * WIDTH), row),
            pl.BlockSpec((CHUNK, LANES), row),
            pl.BlockSpec((2 * HEADS, CHUNK), lambda b, c: (0, b * nc + c)),
            pl.BlockSpec((HEADS, CHUNK, CHUNK), lambda b, c: (0, 0, 0)),
            pl.BlockSpec((CHUNK, 2 * HEADS), lambda b, c: (0, 0)),
            pl.BlockSpec((1, WIDTH), lambda b, c: (0, 0)),
            pl.BlockSpec((1, WIDTH), lambda b, c: (0, 0)),
        ],
        out_specs=[
            pl.BlockSpec((CHUNK, 2 * WIDTH), row),
            state,
            state,
            pl.BlockSpec((1, 1, HEADS, HEAD_DIM), lambda b, c: (0, b, 0, 0)),
            pl.BlockSpec((1, 8, LANES), lambda b, c: (b, 0, 0)),
        ],
        out_shape=[
            jax.ShapeDtypeStruct((batch * seq, 2 * WIDTH), BF16),
            jax.ShapeDtypeStruct((1, batch, HEADS, HEAD_DIM, HEAD_DIM), F32),
            jax.ShapeDtypeStruct((1, batch, HEADS, HEAD_DIM, HEAD_DIM), F32),
            jax.ShapeDtypeStruct((1, batch, HEADS, HEAD_DIM), F32),
            jax.ShapeDtypeStruct((batch, 8, LANES), F32),
        ],
        scratch_shapes=[pltpu.VMEM((8, LANES), F32)],
        compiler_params=_params("arbitrary", "arbitrary"),
        name="prompt_mixer",
    )(qkv, gates, g_tok, g_t, intra, dec, rgn, mgn)


def _sample_gate_kernel(g_ref, m0_ref, mt_ref, dw_ref, iw_ref, emt_ref):
    ig = g_ref[...]
    lf = pltpu.roll(ig, LANES - HEADS, 1)
    inter = lf + m0_ref[...]
    mt = jnp.maximum(inter, ig)
    mt_ref[...] = mt
    dw_ref[...] = jnp.exp(ig - mt)
    iw_ref[...] = jnp.exp(inter - mt)
    emt_ref[...] = jnp.exp(-mt)


def _sample_gates(g_tok, m0_pad):
    m = g_tok.shape[0]
    spec = pl.BlockSpec((m, LANES), lambda: (0, 0))
    return pl.pallas_call(
        _sample_gate_kernel,
        in_specs=[spec, spec],
        out_specs=[spec] * 4,
        out_shape=[jax.ShapeDtypeStruct((m, LANES), F32)] * 4,
        name="sample_gates",
    )(g_tok, m0_pad)


def _sample_mixer_kernel(scal_ref, cols_ref, rows_ref, gate_ref, s0_ref, c0_ref, n0_ref,
                         rgn_ref, mgn_ref, mix_ref, s_ref, c_ref, n_ref):
    b = pl.program_id(0)
    for h in range(HEADS):
        lo = h * HEAD_DIM
        qc = cols_ref[0, :, h:h + 1]
        kc = cols_ref[0, :, HEADS + h:HEADS + h + 1]
        v = rows_ref[0, :, 2 * WIDTH + lo:2 * WIDTH + lo + HEAD_DIM]
        s_new = RET_G[h] * s0_ref[0, 0, h] + kc * v
        s_ref[0, 0, h] = s_new
        o = jnp.sum(qc * s_new, axis=0, keepdims=True)
        rg = gate_ref[0, :, lo:lo + HEAD_DIM]
        mix_ref[0, :, lo:lo + HEAD_DIM] = (_head_norm(o) * rgn_ref[:, lo:lo + HEAD_DIM]
                                           * (rg * _sigmoid(rg)))
        qc = cols_ref[0, :, 2 * HEADS + h:2 * HEADS + h + 1]
        kc = cols_ref[0, :, 3 * HEADS + h:3 * HEADS + h + 1]
        q_row = rows_ref[0, :, 3 * WIDTH + lo:3 * WIDTH + lo + HEAD_DIM]
        k_row = rows_ref[0, :, 4 * WIDTH + lo:4 * WIDTH + lo + HEAD_DIM]
        v = rows_ref[0, :, 5 * WIDTH + lo:5 * WIDTH + lo + HEAD_DIM]
        dw = scal_ref[b, HEADS + h]
        iw = scal_ref[b, 2 * HEADS + h]
        emt = scal_ref[b, 3 * HEADS + h]
        c_new = iw * c0_ref[0, 0, h] + (dw * kc) * v
        c_ref[0, 0, h] = c_new
        n_new = iw * n0_ref[0, 0, h:h + 1, :] + dw * k_row
        n_ref[0, 0, h:h + 1, :] = n_new
        num = jnp.sum(qc * c_new, axis=0, keepdims=True)
        den = jnp.sum(q_row * n_new, axis=1, keepdims=True)
        hid = num * (1.0 / jnp.maximum(jnp.abs(den), emt))
        mo = gate_ref[0, :, WIDTH + lo:WIDTH + lo + HEAD_DIM]
        mix_ref[0, :, WIDTH + lo:WIDTH + lo + HEAD_DIM] = (
            _head_norm(hid) * mgn_ref[:, lo:lo + HEAD_DIM] * _sigmoid(mo))


def _sample_mixer(scal, cols, rows, gates, s0, c0, n0, rgn, mgn):
    nb = cols.shape[0]
    state = pl.BlockSpec((1, 1, HEADS, HEAD_DIM, HEAD_DIM), lambda b: (0, b, 0, 0, 0))
    nspec = pl.BlockSpec((1, 1, HEADS, HEAD_DIM), lambda b: (0, b, 0, 0))
    return pl.pallas_call(
        _sample_mixer_kernel,
        grid=(nb,),
        in_specs=[
            pl.BlockSpec(memory_space=pltpu.SMEM),
            pl.BlockSpec((1, HEAD_DIM, 4 * HEADS), lambda b: (b, 0, 0)),
            pl.BlockSpec((1, 1, 6 * WIDTH), lambda b: (b, 0, 0)),
            pl.BlockSpec((1, 1, 2 * WIDTH), lambda b: (b, 0, 0)),
            state, state, nspec,
            pl.BlockSpec((1, WIDTH), lambda b: (0, 0)),
            pl.BlockSpec((1, WIDTH), lambda b: (0, 0)),
        ],
        out_specs=[pl.BlockSpec((1, 1, 2 * WIDTH), lambda b: (b, 0, 0)), state, state, nspec],
        out_shape=[
            jax.ShapeDtypeStruct((nb, 1, 2 * WIDTH), F32),
            jax.ShapeDtypeStruct(s0.shape, F32),
            jax.ShapeDtypeStruct(c0.shape, F32),
            jax.ShapeDtypeStruct(n0.shape, F32),
        ],
        compiler_params=_params("arbitrary"),
        name="sample_mixer",
    )(scal, cols, rows, gates, s0, c0, n0, rgn, mgn)


def _outproj_kernel(a_ref, w_ref, x_ref, g_ref, b_ref, x1_ref, x1b_ref):
    mix = jnp.dot(a_ref[...].astype(BF16), w_ref[...], preferred_element_type=F32)
    x1 = _layer_norm(ALPHA * x_ref[...] + mix, g_ref[...], b_ref[...])
    x1_ref[...] = x1
    x1b_ref[...] = x1.astype(BF16)


def _outproj(a, wb, x, g, b, *, tm):
    m = a.shape[0]
    row = pl.BlockSpec((tm, D_MODEL), lambda i: (i, 0))
    vec = pl.BlockSpec((1, D_MODEL), lambda i: (0, 0))
    return pl.pallas_call(
        _outproj_kernel,
        grid=(m // tm,),
        in_specs=[row, pl.BlockSpec((D_MODEL, D_MODEL), lambda i: (0, 0)), row, vec, vec],
        out_specs=[row, row],
        out_shape=[jax.ShapeDtypeStruct((m, D_MODEL), F32), jax.ShapeDtypeStruct((m, D_MODEL), BF16)],
        compiler_params=_params("arbitrary"),
        name="outproj_ln1",
    )(a, wb, x, g, b)


def _pe_kernel(x1_ref, x1b_ref, p_ref, wpe_ref, wg_ref, o_ref):
    gate = _sigmoid(jnp.dot(x1b_ref[...], wg_ref[...], preferred_element_type=F32))
    pe = jnp.dot(p_ref[...].astype(BF16), wpe_ref[...], preferred_element_type=F32)
    o_ref[...] = ALPHA * x1_ref[...] + pe * gate


def _pe_base(x1, x1b, p, wpe, wg, *, tm):
    m = x1.shape[0]
    row = pl.BlockSpec((tm, D_MODEL), lambda i: (i, 0))
    return pl.pallas_call(
        _pe_kernel,
        grid=(m // tm,),
        in_specs=[row, row, pl.BlockSpec((tm, PLE_DIM), lambda i: (i, 0)),
                  pl.BlockSpec((PLE_DIM, D_MODEL), lambda i: (0, 0)),
                  pl.BlockSpec((D_MODEL, D_MODEL), lambda i: (0, 0))],
        out_specs=row,
        out_shape=jax.ShapeDtypeStruct((m, D_MODEL), F32),
        compiler_params=_params("arbitrary"),
        name="pe_base",
    )(x1, x1b, p, wpe, wg)


def _ffn_kernel(x1b_ref, base_ref, w1_ref, w2_ref, g_ref, b_ref, o_ref):
    f = pl.program_id(1)

    @pl.when(f == 0)
    def _():
        o_ref[...] = base_ref[...]

    hid = jnp.dot(x1b_ref[...], w1_ref[...], preferred_element_type=F32)
    hid = jnp.square(jnp.maximum(hid, 0.0)).astype(BF16)
    o_ref[...] += jnp.dot(hid, w2_ref[...], preferred_element_type=F32)

    @pl.when(f == pl.num_programs(1) - 1)
    def _():
        o_ref[...] = _layer_norm(o_ref[...], g_ref[...], b_ref[...])


def _ffn(x1b, base, w1, w2, g, b, *, tm, tf):
    m = x1b.shape[0]
    row = pl.BlockSpec((tm, D_MODEL), lambda i, f: (i, 0))
    vec = pl.BlockSpec((1, D_MODEL), lambda i, f: (0, 0))
    return pl.pallas_call(
        _ffn_kernel,
        grid=(m // tm, D_FF // tf),
        in_specs=[row, row,
                  pl.BlockSpec((D_MODEL, tf), lambda i, f: (0, f)),
                  pl.BlockSpec((tf, D_MODEL), lambda i, f: (f, 0)),
                  vec, vec],
        out_specs=row,
        out_shape=jax.ShapeDtypeStruct((m, D_MODEL), F32),
        compiler_params=_params("arbitrary", "arbitrary"),
        name="ffn_ln2",
    )(x1b, base, w1, w2, g, b)


def _rope_tables(pos):
    inv = ROPE_BASE ** (-jnp.arange(HALF, dtype=F32) * 2.0 / HEAD_DIM)
    ang = pos[:, None] * inv[None, :]
    return jnp.cos(ang), jnp.sin(ang)


def _decay_tables():
    lg = jnp.log(1.0 - 2.0 ** (-5.0 - jnp.arange(HEADS, dtype=F32)))
    t = jnp.arange(CHUNK, dtype=F32)
    causal = t[:, None] >= t[None, :]
    intra = jnp.exp(jnp.where(causal, (t[:, None] - t[None, :]) * lg[:, None, None], -jnp.inf))
    q_decay = jnp.exp(lg[:, None] * (t + 1.0))
    k_decay = jnp.exp(lg[:, None] * (CHUNK - 1.0 - t))
    return intra, jnp.concatenate([q_decay, k_decay], axis=0).T


def _tail(x, mixed, p, w, *, tm, tm_ffn, tf):
    x1, x1b = _outproj(mixed, w["out"], x, w["ln1_g"], w["ln1_b"], tm=tm)
    base = _pe_base(x1, x1b, p, w["pe"], w["pe_gate"], tm=tm)
    return _ffn(x1b, base, w["ff1"], w["ff2"], w["ln2_g"], w["ln2_b"], tm=tm_ffn, tf=tf)


def kernel(x_prompt, x_sample, state_ret, state_mlstm_C, state_mlstm_n, state_mlstm_m, p_prompt, p_sample, w_in, b_gate, ret_gn_w, mlstm_gn_w, w_out, ln1_g, ln1_b, w_ff1, w_ff2, w_pe, w_pe_gate, ln2_g, ln2_b):
    batch, seq, _ = x_prompt.shape
    nb = x_sample.shape[0]

    w_in_b = w_in[0].astype(BF16)
    w8 = jnp.pad(w_in[0][:, 8 * WIDTH:], ((0, 0), (0, LANES - 2 * HEADS))).astype(BF16)
    b8 = jnp.pad(b_gate[0].astype(F32), (0, LANES - 2 * HEADS)).reshape(1, LANES)
    w = {
        "out": w_out[0].astype(BF16), "ff1": w_ff1[0].astype(BF16), "ff2": w_ff2[0].astype(BF16),
        "pe": w_pe[0].astype(BF16), "pe_gate": w_pe_gate[0].astype(BF16),
        "ln1_g": ln1_g[0].reshape(1, D_MODEL), "ln1_b": ln1_b[0].reshape(1, D_MODEL),
        "ln2_g": ln2_g[0].reshape(1, D_MODEL), "ln2_b": ln2_b[0].reshape(1, D_MODEL),
    }
    rgn = ret_gn_w[0].reshape(1, WIDTH)
    mgn = mlstm_gn_w[0].reshape(1, WIDTH)
    intra, dec = _decay_tables()

    xp = x_prompt.reshape(batch * seq, D_MODEL)
    xpb = xp.astype(BF16)
    cos_p, sin_p = _rope_tables(jnp.arange(seq, dtype=F32))
    qkv_p = _inproj(xpb, w_in_b, cos_p, sin_p, tm=1024, qkv=True, out_dtype=BF16)
    gates_p = _inproj(xpb, w_in_b, cos_p, sin_p, tm=1024, qkv=False, out_dtype=F32)
    g_tok, g_t = _mgate(xpb, w8, b8, tm=1024)
    mixed_p, s_p, c_p, n_p, m_rows = _prompt_mixer(qkv_p, gates_p, g_tok, g_t, intra, dec, rgn, mgn,
                                                   batch=batch, seq=seq)
    m_p = m_rows[:, :HEADS, 0].reshape(1, batch, HEADS)
    y_p = _tail(xp, mixed_p, p_prompt[0].reshape(batch * seq, PLE_DIM), w, tm=512, tm_ffn=512, tf=512)

    xs = x_sample.reshape(nb, D_MODEL)
    xsb = xs.astype(BF16)
    cos_s, sin_s = _rope_tables(jnp.full((nb,), PAST_LEN, dtype=F32))
    qkv_s = _inproj(xsb, w_in_b, cos_s, sin_s, tm=nb, qkv=True, out_dtype=F32)
    gates_s = _inproj(xsb, w_in_b, cos_s, sin_s, tm=nb, qkv=False, out_dtype=F32)
    gs_tok, _ = _mgate(xsb, w8, b8, tm=nb)
    m0 = jnp.pad(state_mlstm_m[0], ((0, 0), (0, LANES - HEADS)))
    mt, dw, iw, emt = _sample_gates(gs_tok, m0)
    scal = jnp.concatenate([mt[:, :HEADS], dw[:, :HEADS], iw[:, :HEADS], emt[:, :HEADS]], axis=1)
    qk = jnp.concatenate([qkv_s[:, 0:2 * WIDTH], qkv_s[:, 3 * WIDTH:5 * WIDTH]], axis=1)
    cols = jnp.transpose(qk.reshape(nb, 4 * HEADS, HEAD_DIM), (0, 2, 1))
    mixed_s, s_s, c_s, n_s = _sample_mixer(
        scal, cols, qkv_s.reshape(nb, 1, 6 * WIDTH), gates_s.reshape(nb, 1, 2 * WIDTH),
        state_ret, state_mlstm_C, state_mlstm_n, rgn, mgn)
    m_s = mt[:, :HEADS].reshape(1, nb, HEADS)
    y_s = _tail(xs, mixed_s.reshape(nb, 2 * WIDTH), p_sample[0].reshape(nb, PLE_DIM), w,
                tm=nb, tm_ffn=nb, tf=512)

    return (y_p.reshape(batch, seq, D_MODEL), y_s.reshape(nb, 1, D_MODEL),
            s_p, c_p, n_p, m_p, s_s, c_s, n_s, m_s)
```

```python
import functools

import jax
import jax.numpy as jnp
import numpy as np
from jax import lax
from jax.experimental import pallas as pl
from jax.experimental.pallas import tpu as pltpu

F32 = jnp.float32
BF16 = jnp.bfloat16

D_MODEL = 2048
HEADS = 4
HEAD_DIM = 256
HALF = HEAD_DIM // 2
WIDTH = HEADS * HEAD_DIM
D_FF = 4 * D_MODEL
PLE_DIM = 256
CHUNK = 128
PAST_LEN = 16384
ROPE_BASE = 10000.0
LN_EPS = 1e-5
GN_EPS = 1e-6
DEPTH = 1
ALPHA = (2 * DEPTH) ** 0.25
QK_SCALE = HEAD_DIM ** -0.5
LANES = 128
VMEM_LIMIT = 56 * 1024 * 1024

RET_G = tuple(1.0 - 2.0 ** (-5.0 - h) for h in range(HEADS))
RET_LOG_G = tuple(float(np.log(np.float32(g))) for g in RET_G)
RET_STATE_DECAY = tuple(float(np.exp(np.float32(lg) * np.float32(CHUNK))) for lg in RET_LOG_G)

_NT = (((1,), (1,)), ((), ()))
_TN = (((0,), (0,)), ((), ()))


def _params(*sem):
    return pltpu.CompilerParams(dimension_semantics=sem, vmem_limit_bytes=VMEM_LIMIT)


def _sigmoid(x):
    return 1.0 / (1.0 + jnp.exp(-x))


def _log_sigmoid(x):
    return jnp.minimum(x, 0.0) - jnp.log(1.0 + jnp.exp(-jnp.abs(x)))


def _layer_norm(y, g, b):
    mu = jnp.mean(y, axis=-1, keepdims=True)
    yc = y - mu
    var = jnp.mean(jnp.square(yc), axis=-1, keepdims=True)
    return yc * lax.rsqrt(var + LN_EPS) * g + b


def _head_norm(o):
    mu = jnp.mean(o, axis=-1, keepdims=True)
    oc = o - mu
    var = jnp.mean(jnp.square(oc), axis=-1, keepdims=True)
    return oc * lax.rsqrt(var + GN_EPS)


def _inproj_kernel(x_ref, w_ref, cos_ref, sin_ref, o_ref, *, qkv):
    j = pl.program_id(0)
    acc = jnp.dot(x_ref[...], w_ref[...], preferred_element_type=F32)
    if not qkv:
        sig = _sigmoid(acc)
        o_ref[...] = jnp.where(j == 0, acc * sig, sig).astype(o_ref.dtype)
        return
    scale = jnp.where((j == 1) | (j == 4), QK_SCALE, 1.0).astype(F32)

    @pl.when(j < 2)
    def _():
        cos = cos_ref[...]
        sin = sin_ref[...]
        for h in range(HEADS):
            lo = h * HEAD_DIM
            x1 = acc[:, lo:lo + HALF]
            x2 = acc[:, lo + HALF:lo + HEAD_DIM]
            o_ref[:, lo:lo + HALF] = ((x1 * cos - x2 * sin) * scale).astype(o_ref.dtype)
            o_ref[:, lo + HALF:lo + HEAD_DIM] = ((x1 * sin + x2 * cos) * scale).astype(o_ref.dtype)

    @pl.when(j >= 2)
    def _():
        o_ref[...] = (acc * scale).astype(o_ref.dtype)


def _inproj(xb, wb, cos, sin, *, tm, qkv, out_dtype):
    m = xb.shape[0]
    n_groups = 6 if qkv else 2
    w_map = (lambda j, i: (0, j + j // 3)) if qkv else (lambda j, i: (0, 3 + 4 * j))
    n_pos = cos.shape[0] // tm
    return pl.pallas_call(
        functools.partial(_inproj_kernel, qkv=qkv),
        grid=(n_groups, m // tm),
        in_specs=[
            pl.BlockSpec((tm, D_MODEL), lambda j, i: (i, 0)),
            pl.BlockSpec((D_MODEL, WIDTH), w_map),
            pl.BlockSpec((tm, HALF), lambda j, i: (i % n_pos, 0)),
            pl.BlockSpec((tm, HALF), lambda j, i: (i % n_pos, 0)),
        ],
        out_specs=pl.BlockSpec((tm, WIDTH), lambda j, i: (i, j)),
        out_shape=jax.ShapeDtypeStruct((m, n_groups * WIDTH), out_dtype),
        compiler_params=_params("arbitrary", "arbitrary"),
        name="inproj_qkv" if qkv else "inproj_gates",
    )(xb, wb, cos, sin)


def _mgate_kernel(x_ref, w_ref, b_ref, g_ref, gt_ref):
    g = jnp.dot(x_ref[...], w_ref[...], preferred_element_type=F32) + b_ref[...]
    lane = lax.broadcasted_iota(jnp.int32, g.shape, 1)
    g = jnp.where((lane >= HEADS) & (lane < 2 * HEADS), _log_sigmoid(g), g)
    r = lax.broadcasted_iota(jnp.int32, (CHUNK, CHUNK), 0)
    s = lax.broadcasted_iota(jnp.int32, (CHUNK, CHUNK), 1)
    tril = (r >= s).astype(BF16)
    g1 = g.astype(BF16)
    res = g - g1.astype(F32)
    g2 = res.astype(BF16)
    g3 = (res - g2.astype(F32)).astype(BF16)
    parts = []
    for c in range(g.shape[0] // CHUNK):
        sl = slice(c * CHUNK, (c + 1) * CHUNK)
        parts.append(jnp.dot(tril, g1[sl], preferred_element_type=F32)
                     + jnp.dot(tril, g2[sl], preferred_element_type=F32)
                     + jnp.dot(tril, g3[sl], preferred_element_type=F32))
    cs = parts[0] if len(parts) == 1 else jnp.concatenate(parts, axis=0)
    b = pltpu.roll(cs, HEADS, 1)
    u = pltpu.roll(g, 3 * HEADS, 1) - pltpu.roll(cs, 2 * HEADS, 1)
    g = jnp.where((lane >= 2 * HEADS) & (lane < 3 * HEADS), b, g)
    g = jnp.where((lane >= 3 * HEADS) & (lane < 4 * HEADS), u, g)
    g_ref[...] = g
    gt_ref[...] = g.T[0:4 * HEADS, :]


def _mgate(xb, w8, b8, *, tm):
    m = xb.shape[0]
    return pl.pallas_call(
        _mgate_kernel,
        grid=(m // tm,),
        in_specs=[
            pl.BlockSpec((tm, D_MODEL), lambda i: (i, 0)),
            pl.BlockSpec((D_MODEL, LANES), lambda i: (0, 0)),
            pl.BlockSpec((1, LANES), lambda i: (0, 0)),
        ],
        out_specs=[
            pl.BlockSpec((tm, LANES), lambda i: (i, 0)),
            pl.BlockSpec((4 * HEADS, tm), lambda i: (0, i)),
        ],
        out_shape=[
            jax.ShapeDtypeStruct((m, LANES), F32),
            jax.ShapeDtypeStruct((4 * HEADS, m), F32),
        ],
        compiler_params=_params("arbitrary"),
        name="mlstm_gates",
    )(xb, w8, b8)


def _prompt_mixer_kernel(qkv_ref, gate_ref, g_ref, gt_ref, intra_ref, dec_ref, rgn_ref, mgn_ref,
                         mix_ref, s_ref, c_ref, n_ref, m_ref, m_scr):
    c = pl.program_id(1)

    @pl.when(c == 0)
    def _():
        s_ref[...] = jnp.zeros_like(s_ref)
        c_ref[...] = jnp.zeros_like(c_ref)
        n_ref[...] = jnp.zeros_like(n_ref)
        m_scr[...] = jnp.zeros_like(m_scr)

    ti = lax.broadcasted_iota(jnp.int32, (CHUNK, CHUNK), 0)
    si = lax.broadcasted_iota(jnp.int32, (CHUNK, CHUNK), 1)
    causal = ti >= si

    def head(group, h):
        lo = group * WIDTH + h * HEAD_DIM
        return qkv_ref[:, lo:lo + HEAD_DIM]

    gw = []
    for h in range(HEADS):
        b_col = g_ref[:, 2 * HEADS + h:2 * HEADS + h + 1]
        u_col = g_ref[:, 3 * HEADS + h:3 * HEADS + h + 1]
        u_row = gt_ref[3 * HEADS + h:3 * HEADS + h + 1, :]
        m_prev = m_scr[h:h + 1, 0:1]
        dlog = jnp.where(causal, b_col + u_row, -jnp.inf)
        inter = b_col + m_prev
        mt = jnp.maximum(inter, jnp.max(dlog, axis=1, keepdims=True))
        m_new = mt[CHUNK - 1:CHUNK, :]
        b_last = b_col[CHUNK - 1:CHUNK, :]
        gw.append(dict(
            dw=jnp.exp(dlog - mt), iw=jnp.exp(inter - mt), emt=jnp.exp(-mt),
            sw=jnp.exp(b_last - m_new + u_col), sd=jnp.exp(b_last + m_prev - m_new)))
        m_scr[h:h + 1, :] = jnp.broadcast_to(m_new, (1, LANES))

    st = []
    for h in range(HEADS):
        for kind, (qg, state_ref) in enumerate(((0, s_ref), (3, c_ref))):
            q = head(qg, h)
            k = head(qg + 1, h)
            old = state_ref[0, 0, h]
            st.append(dict(
                h=h, kind=kind, q=q, k=k, v=head(qg + 2, h), old=old,
                sc=lax.dot_general(q, k, _NT, preferred_element_type=F32),
                inter=jnp.dot(q, old.astype(BF16), preferred_element_type=F32)))

    for e in st:
        h = e["h"]
        if e["kind"] == 0:
            kd = e["k"].astype(F32) * dec_ref[HEADS + h]
            s_ref[0, 0, h] = (e["old"] * RET_STATE_DECAY[h]
                              + lax.dot_general(kd.astype(BF16), e["v"], _TN, preferred_element_type=F32))
        else:
            w = gw[h]
            ksw = e["k"].astype(F32) * w["sw"]
            c_ref[0, 0, h] = w["sd"] * e["old"] + lax.dot_general(
                ksw.astype(BF16), e["v"], _TN, preferred_element_type=F32)
            n_old = n_ref[0, 0, h:h + 1, :]
            e["qn"] = jnp.sum(e["q"].astype(F32) * n_old, axis=1, keepdims=True)
            n_ref[0, 0, h:h + 1, :] = w["sd"] * n_old + jnp.sum(ksw, axis=0, keepdims=True)

    for e in st:
        h = e["h"]
        e["sc"] = e["sc"] * (intra_ref[h] if e["kind"] == 0 else gw[h]["dw"])
        e["pv"] = jnp.dot(e["sc"].astype(BF16), e["v"], preferred_element_type=F32)

    for e in st:
        h = e["h"]
        lo = h * HEAD_DIM
        if e["kind"] == 0:
            o = e["pv"] + e["inter"] * dec_ref[h]
            y = _head_norm(o) * rgn_ref[:, lo:lo + HEAD_DIM] * gate_ref[:, lo:lo + HEAD_DIM]
            mix_ref[:, lo:lo + HEAD_DIM] = y.astype(mix_ref.dtype)
        else:
            w = gw[h]
            num = e["pv"] + w["iw"] * e["inter"]
            den = jnp.sum(e["sc"], axis=1, keepdims=True) + w["iw"] * e["qn"]
            hid = num * (1.0 / jnp.maximum(jnp.abs(den), w["emt"]))
            y = (_head_norm(hid) * mgn_ref[:, lo:lo + HEAD_DIM]
                 * gate_ref[:, WIDTH + lo:WIDTH + lo + HEAD_DIM])
            mix_ref[:, WIDTH + lo:WIDTH + lo + HEAD_DIM] = y.astype(mix_ref.dtype)

    @pl.when(c == pl.num_programs(1) - 1)
    def _():
        m_ref[0] = m_scr[...]


def _prompt_mixer(qkv, gates, g_tok, g_t, intra, dec, rgn, mgn, *, batch, seq):
    nc = seq // CHUNK
    row = lambda b, c: (b * nc + c, 0)
    state = pl.BlockSpec((1, 1, HEADS, HEAD_DIM, HEAD_DIM), lambda b, c: (0, b, 0, 0, 0))
    return pl.pallas_call(
        _prompt_mixer_kernel,
        grid=(batch, nc),
        in_specs=[
            pl.BlockSpec((CHUNK, 6 * WIDTH), row),
            pl.BlockSpec((CHUNK, 2 * WIDTH), row),
            pl.BlockSpec((CHUNK, LANES), row),
            pl.BlockSpec((4 * HEADS, CHUNK), lambda b, c: (0, b * nc + c)),
            pl.BlockSpec((HEADS, CHUNK, CHUNK), lambda b, c: (0, 0, 0)),
            pl.BlockSpec((2 * HEADS, CHUNK, HEAD_DIM), lambda b, c: (0, 0, 0)),
            pl.BlockSpec((1, WIDTH), lambda b, c: (0, 0)),
            pl.BlockSpec((1, WIDTH), lambda b, c: (0, 0)),
        ],
        out_specs=[
            pl.BlockSpec((CHUNK, 2 * WIDTH), row),
            state,
            state,
            pl.BlockSpec((1, 1, HEADS, HEAD_DIM), lambda b, c: (0, b, 0, 0)),
            pl.BlockSpec((1, 8, LANES), lambda b, c: (b, 0, 0)),
        ],
        out_shape=[
            jax.ShapeDtypeStruct((batch * seq, 2 * WIDTH), BF16),
            jax.ShapeDtypeStruct((1, batch, HEADS, HEAD_DIM, HEAD_DIM), F32),
            jax.ShapeDtypeStruct((1, batch, HEADS, HEAD_DIM, HEAD_DIM), F32),
            jax.ShapeDtypeStruct((1, batch, HEADS, HEAD_DIM), F32),
            jax.ShapeDtypeStruct((batch, 8, LANES), F32),
        ],
        scratch_shapes=[pltpu.VMEM((8, LANES), F32)],
        compiler_params=_params("arbitrary", "arbitrary"),
        name="prompt_mixer",
    )(qkv, gates, g_tok, g_t, intra, dec, rgn, mgn)


def _sample_gate_kernel(g_ref, m0_ref, mt_ref, dw_ref, iw_ref, emt_ref):
    ig = g_ref[...]
    lf = pltpu.roll(ig, LANES - HEADS, 1)
    inter = lf + m0_ref[...]
    mt = jnp.maximum(inter, ig)
    mt_ref[...] = mt
    dw_ref[...] = jnp.exp(ig - mt)
    iw_ref[...] = jnp.exp(inter - mt)
    emt_ref[...] = jnp.exp(-mt)


def _sample_gates(g_tok, m0_pad):
    m = g_tok.shape[0]
    spec = pl.BlockSpec((m, LANES), lambda: (0, 0))
    return pl.pallas_call(
        _sample_gate_kernel,
        in_specs=[spec, spec],
        out_specs=[spec] * 4,
        out_shape=[jax.ShapeDtypeStruct((m, LANES), F32)] * 4,
        name="sample_gates",
    )(g_tok, m0_pad)


def _sample_mixer_kernel(scal_ref, cols_ref, rows_ref, gate_ref, s0_ref, c0_ref, n0_ref,
                         rgn_ref, mgn_ref, mix_ref, s_ref, c_ref, n_ref):
    b = pl.program_id(0)
    kcols = cols_ref[0].astype(BF16)
    sub = lax.broadcasted_iota(jnp.int32, (4 * HEADS, HEAD_DIM), 0)

    def row(group, h):
        lo = group * WIDTH + h * HEAD_DIM
        return rows_ref[0, :, lo:lo + HEAD_DIM]

    def outer(col, v):
        sel = jnp.where(sub == col, jnp.broadcast_to(v, (4 * HEADS, HEAD_DIM)), 0.0).astype(BF16)
        return jnp.dot(kcols, sel, preferred_element_type=F32)

    def apply(q_row, state):
        q8 = jnp.broadcast_to(q_row, (8, HEAD_DIM)).astype(BF16)
        return jnp.dot(q8, state.astype(BF16), preferred_element_type=F32)[0:1, :]

    dw = [scal_ref[b, HEADS + h] for h in range(HEADS)]
    iw = [scal_ref[b, 2 * HEADS + h] for h in range(HEADS)]
    emt = [scal_ref[b, 3 * HEADS + h] for h in range(HEADS)]
    kv_s = [outer(h, row(2, h)) for h in range(HEADS)]
    kv_c = [outer(HEADS + h, dw[h] * row(5, h)) for h in range(HEADS)]
    s_new, c_new = [], []
    for h in range(HEADS):
        s_new.append(RET_G[h] * s0_ref[0, 0, h] + kv_s[h])
        s_ref[0, 0, h] = s_new[h]
        c_new.append(iw[h] * c0_ref[0, 0, h] + kv_c[h])
        c_ref[0, 0, h] = c_new[h]
    o = [apply(row(0, h), s_new[h]) for h in range(HEADS)]
    num = [apply(row(3, h), c_new[h]) for h in range(HEADS)]
    for h in range(HEADS):
        lo = h * HEAD_DIM
        mix_ref[0, :, lo:lo + HEAD_DIM] = (_head_norm(o[h]) * rgn_ref[:, lo:lo + HEAD_DIM]
                                           * gate_ref[0, :, lo:lo + HEAD_DIM])
        n_new = iw[h] * n0_ref[0, 0, h:h + 1, :] + dw[h] * row(4, h)
        n_ref[0, 0, h:h + 1, :] = n_new
        den = jnp.sum(row(3, h) * n_new, axis=1, keepdims=True)
        hid = num[h] * (1.0 / jnp.maximum(jnp.abs(den), emt[h]))
        mix_ref[0, :, WIDTH + lo:WIDTH + lo + HEAD_DIM] = (
            _head_norm(hid) * mgn_ref[:, lo:lo + HEAD_DIM]
            * gate_ref[0, :, WIDTH + lo:WIDTH + lo + HEAD_DIM])


def _sample_mixer(scal, cols, rows, gates, s0, c0, n0, rgn, mgn):
    nb = cols.shape[0]
    state = pl.BlockSpec((1, 1, HEADS, HEAD_DIM, HEAD_DIM), lambda b: (0, b, 0, 0, 0))
    nspec = pl.BlockSpec((1, 1, HEADS, HEAD_DIM), lambda b: (0, b, 0, 0))
    return pl.pallas_call(
        _sample_mixer_kernel,
        grid=(nb,),
        in_specs=[
            pl.BlockSpec(memory_space=pltpu.SMEM),
            pl.BlockSpec((1, HEAD_DIM, 4 * HEADS), lambda b: (b, 0, 0)),
            pl.BlockSpec((1, 1, 6 * WIDTH), lambda b: (b, 0, 0)),
            pl.BlockSpec((1, 1, 2 * WIDTH), lambda b: (b, 0, 0)),
            state, state, nspec,
            pl.BlockSpec((1, WIDTH), lambda b: (0, 0)),
            pl.BlockSpec((1, WIDTH), lambda b: (0, 0)),
        ],
        out_specs=[pl.BlockSpec((1, 1, 2 * WIDTH), lambda b: (b, 0, 0)), state, state, nspec],
        out_shape=[
            jax.ShapeDtypeStruct((nb, 1, 2 * WIDTH), F32),
            jax.ShapeDtypeStruct(s0.shape, F32),
            jax.ShapeDtypeStruct(c0.shape, F32),
            jax.ShapeDtypeStruct(n0.shape, F32),
        ],
        compiler_params=_params("arbitrary"),
        name="sample_mixer",
    )(scal, cols, rows, gates, s0, c0, n0, rgn, mgn)


def _outproj_kernel(a_ref, w_ref, x_ref, g_ref, b_ref, x1_ref, x1b_ref):
    mix = jnp.dot(a_ref[...].astype(BF16), w_ref[...], preferred_element_type=F32)
    x1 = _layer_norm(ALPHA * x_ref[...] + mix, g_ref[...], b_ref[...])
    x1_ref[...] = x1
    x1b_ref[...] = x1.astype(BF16)


def _outproj(a, wb, x, g, b, *, tm):
    m = a.shape[0]
    row = pl.BlockSpec((tm, D_MODEL), lambda i: (i, 0))
    vec = pl.BlockSpec((1, D_MODEL), lambda i: (0, 0))
    return pl.pallas_call(
        _outproj_kernel,
        grid=(m // tm,),
        in_specs=[row, pl.BlockSpec((D_MODEL, D_MODEL), lambda i: (0, 0), pipeline_mode=pl.Buffered(1)),
                  row, vec, vec],
        out_specs=[row, row],
        out_shape=[jax.ShapeDtypeStruct((m, D_MODEL), F32), jax.ShapeDtypeStruct((m, D_MODEL), BF16)],
        compiler_params=_params("arbitrary"),
        name="outproj_ln1",
    )(a, wb, x, g, b)


def _ffn_kernel(x1b_ref, w1_ref, w2_ref, o_ref):
    f = pl.program_id(1)
    hid = jnp.dot(x1b_ref[...], w1_ref[...], preferred_element_type=F32)
    hid = jnp.square(jnp.maximum(hid, 0.0)).astype(BF16)
    ff = jnp.dot(hid, w2_ref[...], preferred_element_type=F32)

    @pl.when(f == 0)
    def _():
        o_ref[...] = ff

    @pl.when(f > 0)
    def _():
        o_ref[...] += ff


def _ffn(x1b, w1, w2, *, tm, tf):
    m = x1b.shape[0]
    row = pl.BlockSpec((tm, D_MODEL), lambda i, f: (i, 0))
    return pl.pallas_call(
        _ffn_kernel,
        grid=(m // tm, D_FF // tf),
        in_specs=[row,
                  pl.BlockSpec((D_MODEL, tf), lambda i, f: (0, f)),
                  pl.BlockSpec((tf, D_MODEL), lambda i, f: (f, 0))],
        out_specs=row,
        out_shape=jax.ShapeDtypeStruct((m, D_MODEL), F32),
        compiler_params=_params("arbitrary", "arbitrary"),
        name="ffn",
    )(x1b, w1, w2)


def _pe_ln2_kernel(x1_ref, x1b_ref, p_ref, ff_ref, wpe_ref, wg_ref, g_ref, b_ref, o_ref):
    gate = _sigmoid(jnp.dot(x1b_ref[...], wg_ref[...], preferred_element_type=F32))
    pe = jnp.dot(p_ref[...].astype(BF16), wpe_ref[...], preferred_element_type=F32)
    y = ALPHA * x1_ref[...] + ff_ref[...] + pe * gate
    o_ref[...] = _layer_norm(y, g_ref[...], b_ref[...])


def _pe_ln2(x1, x1b, p, ff, wpe, wg, g, b, *, tm):
    m = x1.shape[0]
    row = pl.BlockSpec((tm, D_MODEL), lambda i: (i, 0))
    vec = pl.BlockSpec((1, D_MODEL), lambda i: (0, 0))
    return pl.pallas_call(
        _pe_ln2_kernel,
        grid=(m // tm,),
        in_specs=[row, row, pl.BlockSpec((tm, PLE_DIM), lambda i: (i, 0)), row,
                  pl.BlockSpec((PLE_DIM, D_MODEL), lambda i: (0, 0), pipeline_mode=pl.Buffered(1)),
                  pl.BlockSpec((D_MODEL, D_MODEL), lambda i: (0, 0), pipeline_mode=pl.Buffered(1)),
                  vec, vec],
        out_specs=row,
        out_shape=jax.ShapeDtypeStruct((m, D_MODEL), F32),
        compiler_params=_params("arbitrary"),
        name="pe_ln2",
    )(x1, x1b, p, ff, wpe, wg, g, b)


def _rope_tables(pos):
    inv = ROPE_BASE ** (-jnp.arange(HALF, dtype=F32) * 2.0 / HEAD_DIM)
    ang = pos[:, None] * inv[None, :]
    return jnp.cos(ang), jnp.sin(ang)


def _decay_tables():
    lg = jnp.log(1.0 - 2.0 ** (-5.0 - jnp.arange(HEADS, dtype=F32)))
    t = jnp.arange(CHUNK, dtype=F32)
    causal = t[:, None] >= t[None, :]
    intra = jnp.exp(jnp.where(causal, (t[:, None] - t[None, :]) * lg[:, None, None], -jnp.inf))
    q_decay = jnp.exp(lg[:, None] * (t + 1.0))
    k_decay = jnp.exp(lg[:, None] * (CHUNK - 1.0 - t))
    dec = jnp.concatenate([q_decay, k_decay], axis=0)
    return intra, jnp.broadcast_to(dec[:, :, None], (2 * HEADS, CHUNK, HEAD_DIM))


def _tail(x, mixed, p, w, *, tm, tm_ffn, tf):
    x1, x1b = _outproj(mixed, w["out"], x, w["ln1_g"], w["ln1_b"], tm=tm)
    ff = _ffn(x1b, w["ff1"], w["ff2"], tm=tm_ffn, tf=tf)
    return _pe_ln2(x1, x1b, p, ff, w["pe"], w["pe_gate"], w["ln2_g"], w["ln2_b"], tm=tm)


def kernel(x_prompt, x_sample, state_ret, state_mlstm_C, state_mlstm_n, state_mlstm_m, p_prompt, p_sample, w_in, b_gate, ret_gn_w, mlstm_gn_w, w_out, ln1_g, ln1_b, w_ff1, w_ff2, w_pe, w_pe_gate, ln2_g, ln2_b):
    batch, seq, _ = x_prompt.shape
    nb = x_sample.shape[0]

    w_in_b = w_in[0].astype(BF16)
    w8 = jnp.pad(w_in[0][:, 8 * WIDTH:], ((0, 0), (0, LANES - 2 * HEADS))).astype(BF16)
    b8 = jnp.pad(b_gate[0].astype(F32), (0, LANES - 2 * HEADS)).reshape(1, LANES)
    w = {
        "out": w_out[0].astype(BF16), "ff1": w_ff1[0].astype(BF16), "ff2": w_ff2[0].astype(BF16),
        "pe": w_pe[0].astype(BF16), "pe_gate": w_pe_gate[0].astype(BF16),
        "ln1_g": ln1_g[0].reshape(1, D_MODEL), "ln1_b": ln1_b[0].reshape(1, D_MODEL),
        "ln2_g": ln2_g[0].reshape(1, D_MODEL), "ln2_b": ln2_b[0].reshape(1, D_MODEL),
    }
    rgn = ret_gn_w[0].reshape(1, WIDTH)
    mgn = mlstm_gn_w[0].reshape(1, WIDTH)
    intra, dec = _decay_tables()

    xp = x_prompt.reshape(batch * seq, D_MODEL)
    xpb = xp.astype(BF16)
    cos_p, sin_p = _rope_tables(jnp.arange(seq, dtype=F32))
    qkv_p = _inproj(xpb, w_in_b, cos_p, sin_p, tm=1024, qkv=True, out_dtype=BF16)
    gates_p = _inproj(xpb, w_in_b, cos_p, sin_p, tm=1024, qkv=False, out_dtype=F32)
    g_tok, g_t = _mgate(xpb, w8, b8, tm=1024)
    mixed_p, s_p, c_p, n_p, m_rows = _prompt_mixer(qkv_p, gates_p, g_tok, g_t, intra, dec, rgn, mgn,
                                                   batch=batch, seq=seq)
    m_p = m_rows[:, :HEADS, 0].reshape(1, batch, HEADS)
    y_p = _tail(xp, mixed_p, p_prompt[0].reshape(batch * seq, PLE_DIM), w, tm=512, tm_ffn=1024, tf=1024)

    xs = x_sample.reshape(nb, D_MODEL)
    xsb = xs.astype(BF16)
    cos_s, sin_s = _rope_tables(jnp.full((nb,), PAST_LEN, dtype=F32))
    qkv_s = _inproj(xsb, w_in_b, cos_s, sin_s, tm=nb, qkv=True, out_dtype=F32)
    gates_s = _inproj(xsb, w_in_b, cos_s, sin_s, tm=nb, qkv=False, out_dtype=F32)
    gs_tok, _ = _mgate(xsb, w8, b8, tm=nb)
    m0 = jnp.pad(state_mlstm_m[0], ((0, 0), (0, LANES - HEADS)))
    mt, dw, iw, emt = _sample_gates(gs_tok, m0)
    scal = jnp.concatenate([mt[:, :HEADS], dw[:, :HEADS], iw[:, :HEADS], emt[:, :HEADS]], axis=1)
    kk = jnp.concatenate([qkv_s[:, WIDTH:2 * WIDTH], qkv_s[:, 4 * WIDTH:5 * WIDTH],
                          jnp.zeros((nb, 2 * WIDTH), F32)], axis=1)
    cols = jnp.transpose(kk.reshape(nb, 4 * HEADS, HEAD_DIM), (0, 2, 1))
    mixed_s, s_s, c_s, n_s = _sample_mixer(
        scal, cols, qkv_s.reshape(nb, 1, 6 * WIDTH), gates_s.reshape(nb, 1, 2 * WIDTH),
        state_ret, state_mlstm_C, state_mlstm_n, rgn, mgn)
    m_s = mt[:, :HEADS].reshape(1, nb, HEADS)
    y_s = _tail(xs, mixed_s.reshape(nb, 2 * WIDTH), p_sample[0].reshape(nb, PLE_DIM), w,
                tm=nb, tm_ffn=nb, tf=1024)

    return (y_p.reshape(batch, seq, D_MODEL), y_s.reshape(nb, 1, D_MODEL),
            s_p, c_p, n_p, m_p, s_s, c_s, n_s, m_s)
```

```python
import jax
import jax.numpy as jnp
import numpy as np
from jax import lax
from jax.experimental import pallas as pl
from jax.experimental.pallas import tpu as pltpu

F32 = jnp.float32
BF16 = jnp.bfloat16

D_MODEL = 2048
HEADS = 4
HEAD_DIM = 256
HALF = HEAD_DIM // 2
WIDTH = HEADS * HEAD_DIM
D_FF = 4 * D_MODEL
PLE_DIM = 256
CHUNK = 128
PAST_LEN = 16384
ROPE_BASE = 10000.0
LN_EPS = 1e-5
GN_EPS = 1e-6
DEPTH = 1
ALPHA = (2 * DEPTH) ** 0.25
QK_SCALE = HEAD_DIM ** -0.5
LANES = 128
VMEM_LIMIT = 56 * 1024 * 1024

RET_G = tuple(1.0 - 2.0 ** (-5.0 - h) for h in range(HEADS))
RET_LOG_G = tuple(float(np.log(np.float32(g))) for g in RET_G)
RET_STATE_DECAY = tuple(float(np.exp(np.float32(lg) * np.float32(CHUNK))) for lg in RET_LOG_G)

_NT = (((1,), (1,)), ((), ()))
_TN = (((0,), (0,)), ((), ()))


def _params(*sem):
    return pltpu.CompilerParams(dimension_semantics=sem, vmem_limit_bytes=VMEM_LIMIT)


def _sigmoid(x):
    return 1.0 / (1.0 + jnp.exp(-x))


def _log_sigmoid(x):
    return jnp.minimum(x, 0.0) - jnp.log(1.0 + jnp.exp(-jnp.abs(x)))


def _layer_norm(y, g, b):
    mu = jnp.mean(y, axis=-1, keepdims=True)
    yc = y - mu
    var = jnp.mean(jnp.square(yc), axis=-1, keepdims=True)
    return yc * lax.rsqrt(var + LN_EPS) * g + b


def _head_norm(o):
    mu = jnp.mean(o, axis=-1, keepdims=True)
    oc = o - mu
    var = jnp.mean(jnp.square(oc), axis=-1, keepdims=True)
    return oc * lax.rsqrt(var + GN_EPS)


RG_GROUP, MO_GROUP = 3, 7


def _inproj_kernel(x_ref, wt_ref, c_ref, s_ref, o_ref, wb_ref):
    j = pl.program_id(0)

    @pl.when(pl.program_id(1) == 0)
    def _():
        wb_ref[...] = wt_ref[...].astype(BF16)

    x = x_ref[...]
    is_gate = (j == RG_GROUP) | (j == MO_GROUP)

    def head_acc(h):
        return lax.dot_general(x, wb_ref[h * HEAD_DIM:(h + 1) * HEAD_DIM, :], _NT,
                               preferred_element_type=F32)

    @pl.when(is_gate)
    def _():
        for h in range(HEADS):
            acc = head_acc(h)
            sig = _sigmoid(acc)
            o_ref[:, h * HEAD_DIM:(h + 1) * HEAD_DIM] = jnp.where(
                j == RG_GROUP, acc * sig, sig).astype(o_ref.dtype)

    @pl.when(jnp.logical_not(is_gate))
    def _():
        c = c_ref[0]
        s = s_ref[0]
        for h in range(HEADS):
            lo = h * HEAD_DIM
            acc = head_acc(h)
            x1 = acc[:, :HALF]
            x2 = acc[:, HALF:]
            o_ref[:, lo:lo + HALF] = (x1 * c - x2 * s).astype(o_ref.dtype)
            o_ref[:, lo + HALF:lo + HEAD_DIM] = (x1 * s + x2 * c).astype(o_ref.dtype)


def _inproj(xb, wt, ctab, stab, *, tm):
    m = xb.shape[0]
    n_pos = ctab.shape[1] // tm

    def tab_map(j, i):
        kind = jnp.where(j == 0, 0, jnp.where(j == 1, 1, jnp.where(j == 5, 3, 2)))
        return (kind, i % n_pos, 0)

    return pl.pallas_call(
        _inproj_kernel,
        grid=(8, m // tm),
        in_specs=[
            pl.BlockSpec((tm, D_MODEL), lambda j, i: (i, 0)),
            pl.BlockSpec((WIDTH, D_MODEL), lambda j, i: (j, 0)),
            pl.BlockSpec((1, tm, HALF), tab_map),
            pl.BlockSpec((1, tm, HALF), tab_map),
        ],
        out_specs=pl.BlockSpec((tm, WIDTH), lambda j, i: (i, j)),
        out_shape=jax.ShapeDtypeStruct((m, 8 * WIDTH), BF16),
        scratch_shapes=[pltpu.VMEM((WIDTH, D_MODEL), BF16)],
        compiler_params=_params("arbitrary", "arbitrary"),
        name="inproj",
    )(xb, wt, ctab, stab)


def _mgate_kernel(x_ref, w_ref, b_ref, xb_ref, g_ref, gt_ref):
    xb = x_ref[...].astype(BF16)
    xb_ref[...] = xb
    g = jnp.dot(xb, w_ref[...], preferred_element_type=F32) + b_ref[...]
    lane = lax.broadcasted_iota(jnp.int32, g.shape, 1)
    g = jnp.where((lane >= HEADS) & (lane < 2 * HEADS), _log_sigmoid(g), g)
    r = lax.broadcasted_iota(jnp.int32, (CHUNK, CHUNK), 0)
    s = lax.broadcasted_iota(jnp.int32, (CHUNK, CHUNK), 1)
    tril = (r >= s).astype(BF16)
    g1 = g.astype(BF16)
    res = g - g1.astype(F32)
    g2 = res.astype(BF16)
    g3 = (res - g2.astype(F32)).astype(BF16)
    parts = []
    for c in range(g.shape[0] // CHUNK):
        sl = slice(c * CHUNK, (c + 1) * CHUNK)
        parts.append(jnp.dot(tril, g1[sl], preferred_element_type=F32)
                     + jnp.dot(tril, g2[sl], preferred_element_type=F32)
                     + jnp.dot(tril, g3[sl], preferred_element_type=F32))
    cs = parts[0] if len(parts) == 1 else jnp.concatenate(parts, axis=0)
    b = pltpu.roll(cs, HEADS, 1)
    u = pltpu.roll(g, 3 * HEADS, 1) - pltpu.roll(cs, 2 * HEADS, 1)
    g = jnp.where((lane >= 2 * HEADS) & (lane < 3 * HEADS), b, g)
    g = jnp.where((lane >= 3 * HEADS) & (lane < 4 * HEADS), u, g)
    g_ref[...] = g
    gt_ref[...] = g.T[0:4 * HEADS, :]


def _mgate(x, w8, b8, *, tm):
    m = x.shape[0]
    return pl.pallas_call(
        _mgate_kernel,
        grid=(m // tm,),
        in_specs=[
            pl.BlockSpec((tm, D_MODEL), lambda i: (i, 0)),
            pl.BlockSpec((D_MODEL, LANES), lambda i: (0, 0)),
            pl.BlockSpec((1, LANES), lambda i: (0, 0)),
        ],
        out_specs=[
            pl.BlockSpec((tm, D_MODEL), lambda i: (i, 0)),
            pl.BlockSpec((tm, LANES), lambda i: (i, 0)),
            pl.BlockSpec((4 * HEADS, tm), lambda i: (0, i)),
        ],
        out_shape=[
            jax.ShapeDtypeStruct((m, D_MODEL), BF16),
            jax.ShapeDtypeStruct((m, LANES), F32),
            jax.ShapeDtypeStruct((4 * HEADS, m), F32),
        ],
        compiler_params=_params("arbitrary"),
        name="mlstm_gates",
    )(x, w8, b8)


def _prompt_mixer_kernel(qkv_ref, g_ref, gt_ref, intra_ref, dec_ref, rgn_ref, mgn_ref,
                         mix_ref, s_ref, c_ref, n_ref, m_ref, m_scr):
    c = pl.program_id(1)

    @pl.when(c == 0)
    def _():
        s_ref[...] = jnp.zeros_like(s_ref)
        c_ref[...] = jnp.zeros_like(c_ref)
        n_ref[...] = jnp.zeros_like(n_ref)
        m_scr[...] = jnp.zeros_like(m_scr)

    ti = lax.broadcasted_iota(jnp.int32, (CHUNK, CHUNK), 0)
    si = lax.broadcasted_iota(jnp.int32, (CHUNK, CHUNK), 1)
    causal = ti >= si

    def head(group, h):
        lo = group * WIDTH + h * HEAD_DIM
        return qkv_ref[:, lo:lo + HEAD_DIM]

    gw = []
    for h in range(HEADS):
        b_col = g_ref[:, 2 * HEADS + h:2 * HEADS + h + 1]
        u_col = g_ref[:, 3 * HEADS + h:3 * HEADS + h + 1]
        u_row = gt_ref[3 * HEADS + h:3 * HEADS + h + 1, :]
        m_prev = m_scr[h:h + 1, 0:1]
        dlog = jnp.where(causal, b_col + u_row, -jnp.inf)
        inter = b_col + m_prev
        mt = jnp.maximum(inter, jnp.max(dlog, axis=1, keepdims=True))
        m_new = mt[CHUNK - 1:CHUNK, :]
        b_last = b_col[CHUNK - 1:CHUNK, :]
        gw.append(dict(
            dw=jnp.exp(dlog - mt), iw=jnp.exp(inter - mt), emt=jnp.exp(-mt),
            sw=jnp.exp(b_last - m_new + u_col), sd=jnp.exp(b_last + m_prev - m_new)))
        m_scr[h:h + 1, :] = jnp.broadcast_to(m_new, (1, LANES))

    st = []
    for h in range(HEADS):
        for kind, (qg, state_ref) in enumerate(((0, s_ref), (MO_GROUP - 3, c_ref))):
            q = head(qg, h)
            k = head(qg + 1, h)
            old = state_ref[0, 0, h]
            st.append(dict(
                h=h, kind=kind, q=q, k=k, v=head(qg + 2, h), old=old,
                sc=lax.dot_general(q, k, _NT, preferred_element_type=F32),
                inter=jnp.dot(q, old.astype(BF16), preferred_element_type=F32)))

    for e in st:
        h = e["h"]
        if e["kind"] == 0:
            kd = e["k"].astype(F32) * dec_ref[HEADS + h]
            s_ref[0, 0, h] = (e["old"] * RET_STATE_DECAY[h]
                              + lax.dot_general(kd.astype(BF16), e["v"], _TN, preferred_element_type=F32))
        else:
            w = gw[h]
            ksw = e["k"].astype(F32) * w["sw"]
            c_ref[0, 0, h] = w["sd"] * e["old"] + lax.dot_general(
                ksw.astype(BF16), e["v"], _TN, preferred_element_type=F32)
            n_old = n_ref[0, 0, h:h + 1, :]
            e["qn"] = jnp.sum(e["q"].astype(F32) * n_old, axis=1, keepdims=True)
            n_ref[0, 0, h:h + 1, :] = w["sd"] * n_old + jnp.sum(ksw, axis=0, keepdims=True)

    for e in st:
        h = e["h"]
        e["sc"] = e["sc"] * (intra_ref[h] if e["kind"] == 0 else gw[h]["dw"])
        e["pv"] = jnp.dot(e["sc"].astype(BF16), e["v"], preferred_element_type=F32)

    for e in st:
        h = e["h"]
        lo = h * HEAD_DIM
        if e["kind"] == 0:
            o = e["pv"] + e["inter"] * dec_ref[h]
            y = _head_norm(o) * rgn_ref[:, lo:lo + HEAD_DIM] * head(RG_GROUP, h).astype(F32)
            mix_ref[:, lo:lo + HEAD_DIM] = y.astype(mix_ref.dtype)
        else:
            w = gw[h]
            num = e["pv"] + w["iw"] * e["inter"]
            den = jnp.sum(e["sc"], axis=1, keepdims=True) + w["iw"] * e["qn"]
            hid = num * (1.0 / jnp.maximum(jnp.abs(den), w["emt"]))
            y = _head_norm(hid) * mgn_ref[:, lo:lo + HEAD_DIM] * head(MO_GROUP, h).astype(F32)
            mix_ref[:, WIDTH + lo:WIDTH + lo + HEAD_DIM] = y.astype(mix_ref.dtype)

    @pl.when(c == pl.num_programs(1) - 1)
    def _():
        m_ref[0] = m_scr[...]


def _prompt_mixer(qkv, g_tok, g_t, intra, dec, rgn, mgn, *, batch, seq):
    nc = seq // CHUNK
    row = lambda b, c: (b * nc + c, 0)
    state = pl.BlockSpec((1, 1, HEADS, HEAD_DIM, HEAD_DIM), lambda b, c: (0, b, 0, 0, 0))
    return pl.pallas_call(
        _prompt_mixer_kernel,
        grid=(batch, nc),
        in_specs=[
            pl.BlockSpec((CHUNK, 8 * WIDTH), row),
            pl.BlockSpec((CHUNK, LANES), row),
            pl.BlockSpec((4 * HEADS, CHUNK), lambda b, c: (0, b * nc + c)),
            pl.BlockSpec((HEADS, CHUNK, CHUNK), lambda b, c: (0, 0, 0)),
            pl.BlockSpec((2 * HEADS, CHUNK, HEAD_DIM), lambda b, c: (0, 0, 0)),
            pl.BlockSpec((1, WIDTH), lambda b, c: (0, 0)),
            pl.BlockSpec((1, WIDTH), lambda b, c: (0, 0)),
        ],
        out_specs=[
            pl.BlockSpec((CHUNK, 2 * WIDTH), row),
            state,
            state,
            pl.BlockSpec((1, 1, HEADS, HEAD_DIM), lambda b, c: (0, b, 0, 0)),
            pl.BlockSpec((1, 8, LANES), lambda b, c: (b, 0, 0)),
        ],
        out_shape=[
            jax.ShapeDtypeStruct((batch * seq, 2 * WIDTH), BF16),
            jax.ShapeDtypeStruct((1, batch, HEADS, HEAD_DIM, HEAD_DIM), F32),
            jax.ShapeDtypeStruct((1, batch, HEADS, HEAD_DIM, HEAD_DIM), F32),
            jax.ShapeDtypeStruct((1, batch, HEADS, HEAD_DIM), F32),
            jax.ShapeDtypeStruct((batch, 8, LANES), F32),
        ],
        scratch_shapes=[pltpu.VMEM((8, LANES), F32)],
        compiler_params=_params("arbitrary", "arbitrary"),
        name="prompt_mixer",
    )(qkv, g_tok, g_t, intra, dec, rgn, mgn)


def _sample_gate_kernel(g_ref, m0_ref, mt_ref, dw_ref, iw_ref, emt_ref):
    ig = g_ref[...]
    lf = pltpu.roll(ig, LANES - HEADS, 1)
    inter = lf + m0_ref[...]
    mt = jnp.maximum(inter, ig)
    mt_ref[...] = mt
    dw_ref[...] = jnp.exp(ig - mt)
    iw_ref[...] = jnp.exp(inter - mt)
    emt_ref[...] = jnp.exp(-mt)


def _sample_gates(g_tok, m0_pad):
    m = g_tok.shape[0]
    spec = pl.BlockSpec((m, LANES), lambda: (0, 0))
    return pl.pallas_call(
        _sample_gate_kernel,
        in_specs=[spec, spec],
        out_specs=[spec] * 4,
        out_shape=[jax.ShapeDtypeStruct((m, LANES), F32)] * 4,
        name="sample_gates",
    )(g_tok, m0_pad)


def _sample_mixer_kernel(scal_ref, cols_ref, rows_ref, s0_ref, c0_ref, n0_ref,
                         rgn_ref, mgn_ref, mix_ref, s_ref, c_ref, n_ref):
    b = pl.program_id(0)
    kcols = cols_ref[0].astype(BF16)
    sub = lax.broadcasted_iota(jnp.int32, (4 * HEADS, HEAD_DIM), 0)

    def row(group, h):
        lo = group * WIDTH + h * HEAD_DIM
        return rows_ref[0, :, lo:lo + HEAD_DIM]

    def outer(col, v):
        sel = jnp.where(sub == col, jnp.broadcast_to(v, (4 * HEADS, HEAD_DIM)), 0.0).astype(BF16)
        return jnp.dot(kcols, sel, preferred_element_type=F32)

    def apply(q_row, state):
        q8 = jnp.broadcast_to(q_row, (8, HEAD_DIM)).astype(BF16)
        return jnp.dot(q8, state.astype(BF16), preferred_element_type=F32)[0:1, :]

    dw = [scal_ref[b, HEADS + h] for h in range(HEADS)]
    iw = [scal_ref[b, 2 * HEADS + h] for h in range(HEADS)]
    emt = [scal_ref[b, 3 * HEADS + h] for h in range(HEADS)]
    kv_s = [outer(h, row(2, h)) for h in range(HEADS)]
    kv_c = [outer(HEADS + h, dw[h] * row(6, h)) for h in range(HEADS)]
    s_new, c_new = [], []
    for h in range(HEADS):
        s_new.append(RET_G[h] * s0_ref[0, 0, h] + kv_s[h])
        s_ref[0, 0, h] = s_new[h]
        c_new.append(iw[h] * c0_ref[0, 0, h] + kv_c[h])
        c_ref[0, 0, h] = c_new[h]
    o = [apply(row(0, h), s_new[h]) for h in range(HEADS)]
    num = [apply(row(4, h), c_new[h]) for h in range(HEADS)]
    for h in range(HEADS):
        lo = h * HEAD_DIM
        mix_ref[0, :, lo:lo + HEAD_DIM] = (_head_norm(o[h]) * rgn_ref[:, lo:lo + HEAD_DIM]
                                           * row(RG_GROUP, h))
        n_new = iw[h] * n0_ref[0, 0, h:h + 1, :] + dw[h] * row(5, h)
        n_ref[0, 0, h:h + 1, :] = n_new
        den = jnp.sum(row(4, h) * n_new, axis=1, keepdims=True)
        hid = num[h] * (1.0 / jnp.maximum(jnp.abs(den), emt[h]))
        mix_ref[0, :, WIDTH + lo:WIDTH + lo + HEAD_DIM] = (
            _head_norm(hid) * mgn_ref[:, lo:lo + HEAD_DIM] * row(MO_GROUP, h))


def _sample_mixer(scal, cols, rows, s0, c0, n0, rgn, mgn):
    nb = cols.shape[0]
    state = pl.BlockSpec((1, 1, HEADS, HEAD_DIM, HEAD_DIM), lambda b: (0, b, 0, 0, 0))
    nspec = pl.BlockSpec((1, 1, HEADS, HEAD_DIM), lambda b: (0, b, 0, 0))
    return pl.pallas_call(
        _sample_mixer_kernel,
        grid=(nb,),
        in_specs=[
            pl.BlockSpec(memory_space=pltpu.SMEM),
            pl.BlockSpec((1, HEAD_DIM, 4 * HEADS), lambda b: (b, 0, 0)),
            pl.BlockSpec((1, 1, 8 * WIDTH), lambda b: (b, 0, 0)),
            state, state, nspec,
            pl.BlockSpec((1, WIDTH), lambda b: (0, 0)),
            pl.BlockSpec((1, WIDTH), lambda b: (0, 0)),
        ],
        out_specs=[pl.BlockSpec((1, 1, 2 * WIDTH), lambda b: (b, 0, 0)), state, state, nspec],
        out_shape=[
            jax.ShapeDtypeStruct((nb, 1, 2 * WIDTH), F32),
            jax.ShapeDtypeStruct(s0.shape, F32),
            jax.ShapeDtypeStruct(c0.shape, F32),
            jax.ShapeDtypeStruct(n0.shape, F32),
        ],
        compiler_params=_params("arbitrary"),
        name="sample_mixer",
    )(scal, cols, rows, s0, c0, n0, rgn, mgn)


def _outproj_kernel(a_ref, w_ref, x_ref, g_ref, b_ref, x1_ref, x1b_ref):
    mix = jnp.dot(a_ref[...].astype(BF16), w_ref[...], preferred_element_type=F32)
    x1 = _layer_norm(ALPHA * x_ref[...] + mix, g_ref[...], b_ref[...])
    x1_ref[...] = x1
    x1b_ref[...] = x1.astype(BF16)


def _outproj(a, wb, x, g, b, *, tm):
    m = a.shape[0]
    row = pl.BlockSpec((tm, D_MODEL), lambda i: (i, 0))
    vec = pl.BlockSpec((1, D_MODEL), lambda i: (0, 0))
    return pl.pallas_call(
        _outproj_kernel,
        grid=(m // tm,),
        in_specs=[row, pl.BlockSpec((D_MODEL, D_MODEL), lambda i: (0, 0), pipeline_mode=pl.Buffered(1)),
                  row, vec, vec],
        out_specs=[row, row],
        out_shape=[jax.ShapeDtypeStruct((m, D_MODEL), F32), jax.ShapeDtypeStruct((m, D_MODEL), BF16)],
        compiler_params=_params("arbitrary"),
        name="outproj_ln1",
    )(a, wb, x, g, b)


def _ffn_kernel(x1b_ref, w1_ref, w2_ref, o_ref):
    @pl.when(pl.program_id(1) == 0)
    def _():
        o_ref[...] = jnp.zeros_like(o_ref)

    hid = jnp.dot(x1b_ref[...], w1_ref[...], preferred_element_type=F32)
    hid = jnp.square(jnp.maximum(hid, 0.0)).astype(BF16)
    o_ref[...] += jnp.dot(hid, w2_ref[...], preferred_element_type=F32)


def _ffn(x1b, w1, w2, *, tm, tf):
    m = x1b.shape[0]
    row = pl.BlockSpec((tm, D_MODEL), lambda i, f: (i, 0))
    return pl.pallas_call(
        _ffn_kernel,
        grid=(m // tm, D_FF // tf),
        in_specs=[row,
                  pl.BlockSpec((D_MODEL, tf), lambda i, f: (0, f)),
                  pl.BlockSpec((tf, D_MODEL), lambda i, f: (f, 0))],
        out_specs=row,
        out_shape=jax.ShapeDtypeStruct((m, D_MODEL), F32),
        compiler_params=_params("arbitrary", "arbitrary"),
        name="ffn",
    )(x1b, w1, w2)


def _pe_ln2_kernel(x1_ref, x1b_ref, p_ref, ff_ref, wpe_ref, wg_ref, g_ref, b_ref, o_ref):
    gate = _sigmoid(jnp.dot(x1b_ref[...], wg_ref[...], preferred_element_type=F32))
    pe = jnp.dot(p_ref[...].astype(BF16), wpe_ref[...], preferred_element_type=F32)
    y = ALPHA * x1_ref[...] + ff_ref[...] + pe * gate
    o_ref[...] = _layer_norm(y, g_ref[...], b_ref[...])


def _pe_ln2(x1, x1b, p, ff, wpe, wg, g, b, *, tm):
    m = x1.shape[0]
    row = pl.BlockSpec((tm, D_MODEL), lambda i: (i, 0))
    vec = pl.BlockSpec((1, D_MODEL), lambda i: (0, 0))
    return pl.pallas_call(
        _pe_ln2_kernel,
        grid=(m // tm,),
        in_specs=[row, row, pl.BlockSpec((tm, PLE_DIM), lambda i: (i, 0)), row,
                  pl.BlockSpec((PLE_DIM, D_MODEL), lambda i: (0, 0), pipeline_mode=pl.Buffered(1)),
                  pl.BlockSpec((D_MODEL, D_MODEL), lambda i: (0, 0), pipeline_mode=pl.Buffered(1)),
                  vec, vec],
        out_specs=row,
        out_shape=jax.ShapeDtypeStruct((m, D_MODEL), F32),
        compiler_params=_params("arbitrary"),
        name="pe_ln2",
    )(x1, x1b, p, ff, wpe, wg, g, b)


def _rope_tables(pos):
    inv = ROPE_BASE ** (-jnp.arange(HALF, dtype=F32) * 2.0 / HEAD_DIM)
    ang = pos[:, None] * inv[None, :]
    cos, sin = jnp.cos(ang), jnp.sin(ang)
    one, zero = jnp.ones_like(cos), jnp.zeros_like(cos)
    return (jnp.stack([cos, cos * QK_SCALE, one, one * QK_SCALE]),
            jnp.stack([sin, sin * QK_SCALE, zero, zero]))


def _decay_tables():
    lg = jnp.log(1.0 - 2.0 ** (-5.0 - jnp.arange(HEADS, dtype=F32)))
    t = jnp.arange(CHUNK, dtype=F32)
    causal = t[:, None] >= t[None, :]
    intra = jnp.exp(jnp.where(causal, (t[:, None] - t[None, :]) * lg[:, None, None], -jnp.inf))
    q_decay = jnp.exp(lg[:, None] * (t + 1.0))
    k_decay = jnp.exp(lg[:, None] * (CHUNK - 1.0 - t))
    dec = jnp.concatenate([q_decay, k_decay], axis=0)
    return intra, jnp.broadcast_to(dec[:, :, None], (2 * HEADS, CHUNK, HEAD_DIM))


def _tail(x, mixed, p, w, *, tm, tm_ffn, tf):
    x1, x1b = _outproj(mixed, w["out"], x, w["ln1_g"], w["ln1_b"], tm=tm)
    ff = _ffn(x1b, w["ff1"], w["ff2"], tm=tm_ffn, tf=tf)
    return _pe_ln2(x1, x1b, p, ff, w["pe"], w["pe_gate"], w["ln2_g"], w["ln2_b"], tm=tm)


def kernel(x_prompt, x_sample, state_ret, state_mlstm_C, state_mlstm_n, state_mlstm_m, p_prompt, p_sample, w_in, b_gate, ret_gn_w, mlstm_gn_w, w_out, ln1_g, ln1_b, w_ff1, w_ff2, w_pe, w_pe_gate, ln2_g, ln2_b):
    batch, seq, _ = x_prompt.shape
    nb = x_sample.shape[0]

    w_in_t = w_in[0].T
    w8 = jnp.pad(w_in[0][:, 8 * WIDTH:], ((0, 0), (0, LANES - 2 * HEADS))).astype(BF16)
    b8 = jnp.pad(b_gate[0].astype(F32), (0, LANES - 2 * HEADS)).reshape(1, LANES)
    w = {
        "out": w_out[0].astype(BF16), "ff1": w_ff1[0].astype(BF16), "ff2": w_ff2[0].astype(BF16),
        "pe": w_pe[0].astype(BF16), "pe_gate": w_pe_gate[0].astype(BF16),
        "ln1_g": ln1_g[0].reshape(1, D_MODEL), "ln1_b": ln1_b[0].reshape(1, D_MODEL),
        "ln2_g": ln2_g[0].reshape(1, D_MODEL), "ln2_b": ln2_b[0].reshape(1, D_MODEL),
    }
    rgn = ret_gn_w[0].reshape(1, WIDTH)
    mgn = mlstm_gn_w[0].reshape(1, WIDTH)
    intra, dec = _decay_tables()

    xp = x_prompt.reshape(batch * seq, D_MODEL)
    xpb, g_tok, g_t = _mgate(xp, w8, b8, tm=1024)
    ctab, stab = _rope_tables(jnp.arange(seq, dtype=F32))
    proj_p = _inproj(xpb, w_in_t, ctab, stab, tm=1024)
    mixed_p, s_p, c_p, n_p, m_rows = _prompt_mixer(proj_p, g_tok, g_t, intra, dec, rgn, mgn,
                                                   batch=batch, seq=seq)
    m_p = m_rows[:, :HEADS, 0].reshape(1, batch, HEADS)
    y_p = _tail(xp, mixed_p, p_prompt[0].reshape(batch * seq, PLE_DIM), w, tm=512, tm_ffn=1024, tf=1024)

    xs = x_sample.reshape(nb, D_MODEL)
    xsb, gs_tok, _ = _mgate(xs, w8, b8, tm=nb)
    ctab, stab = _rope_tables(jnp.full((nb,), PAST_LEN, dtype=F32))
    proj_s = _inproj(xsb, w_in_t, ctab, stab, tm=nb).astype(F32)
    m0 = jnp.pad(state_mlstm_m[0], ((0, 0), (0, LANES - HEADS)))
    mt, dw, iw, emt = _sample_gates(gs_tok, m0)
    scal = jnp.concatenate([mt[:, :HEADS], dw[:, :HEADS], iw[:, :HEADS], emt[:, :HEADS]], axis=1)
    kk = jnp.concatenate([proj_s[:, WIDTH:2 * WIDTH], proj_s[:, 5 * WIDTH:6 * WIDTH],
                          jnp.zeros((nb, 2 * WIDTH), F32)], axis=1)
    cols = jnp.transpose(kk.reshape(nb, 4 * HEADS, HEAD_DIM), (0, 2, 1))
    mixed_s, s_s, c_s, n_s = _sample_mixer(
        scal, cols, proj_s.reshape(nb, 1, 8 * WIDTH),
        state_ret, state_mlstm_C, state_mlstm_n, rgn, mgn)
    m_s = mt[:, :HEADS].reshape(1, nb, HEADS)
    y_s = _tail(xs, mixed_s.reshape(nb, 2 * WIDTH), p_sample[0].reshape(nb, PLE_DIM), w,
                tm=nb, tm_ffn=nb, tf=1024)

    return (y_p.reshape(batch, seq, D_MODEL), y_s.reshape(nb, 1, D_MODEL),
            s_p, c_p, n_p, m_p, s_s, c_s, n_s, m_s)
```

```python
import jax
import jax.numpy as jnp
import numpy as np
from jax import lax
from jax.experimental import pallas as pl
from jax.experimental.pallas import tpu as pltpu

F32 = jnp.float32
BF16 = jnp.bfloat16

D_MODEL = 2048
HEADS = 4
HEAD_DIM = 256
HALF = HEAD_DIM // 2
WIDTH = HEADS * HEAD_DIM
D_FF = 4 * D_MODEL
PLE_DIM = 256
CHUNK = 128
PAST_LEN = 16384
ROPE_BASE = 10000.0
LN_EPS = 1e-5
GN_EPS = 1e-6
DEPTH = 1
ALPHA = (2 * DEPTH) ** 0.25
QK_SCALE = HEAD_DIM ** -0.5
LANES = 128
VMEM_LIMIT = 56 * 1024 * 1024
SAMPLE_TOKENS_PER_STEP = 4

RET_G = tuple(1.0 - 2.0 ** (-5.0 - h) for h in range(HEADS))
RET_LOG_G = tuple(float(np.log(np.float32(g))) for g in RET_G)
RET_STATE_DECAY = tuple(float(np.exp(np.float32(lg) * np.float32(CHUNK))) for lg in RET_LOG_G)

_NT = (((1,), (1,)), ((), ()))
_TN = (((0,), (0,)), ((), ()))


def _params(*sem):
    return pltpu.CompilerParams(dimension_semantics=sem, vmem_limit_bytes=VMEM_LIMIT)


def _sigmoid(x):
    return 1.0 / (1.0 + jnp.exp(-x))


def _log_sigmoid(x):
    return jnp.minimum(x, 0.0) - jnp.log(1.0 + jnp.exp(-jnp.abs(x)))


def _layer_norm(y, g, b):
    mu = jnp.mean(y, axis=-1, keepdims=True)
    yc = y - mu
    var = jnp.mean(jnp.square(yc), axis=-1, keepdims=True)
    return yc * lax.rsqrt(var + LN_EPS) * g + b


def _head_norm(o):
    mu = jnp.mean(o, axis=-1, keepdims=True)
    oc = o - mu
    var = jnp.mean(jnp.square(oc), axis=-1, keepdims=True)
    return oc * lax.rsqrt(var + GN_EPS)


RG_GROUP, MO_GROUP = 3, 7


def _inproj_kernel(x_ref, wt_ref, c_ref, s_ref, o_ref, *wb_out):
    j = pl.program_id(0)
    if wb_out:
        wb_ref, = wb_out
        wb_ref[...] = wt_ref[...].astype(BF16)
    else:
        wb_ref = wt_ref

    x = x_ref[...]
    is_gate = (j == RG_GROUP) | (j == MO_GROUP)

    def head_acc(h):
        return lax.dot_general(x, wb_ref[h * HEAD_DIM:(h + 1) * HEAD_DIM, :], _NT,
                               preferred_element_type=F32)

    @pl.when(is_gate)
    def _():
        for h in range(HEADS):
            acc = head_acc(h)
            sig = _sigmoid(acc)
            o_ref[:, h * HEAD_DIM:(h + 1) * HEAD_DIM] = jnp.where(
                j == RG_GROUP, acc * sig, sig).astype(o_ref.dtype)

    @pl.when(jnp.logical_not(is_gate))
    def _():
        c = c_ref[0]
        s = s_ref[0]
        for h in range(HEADS):
            lo = h * HEAD_DIM
            acc = head_acc(h)
            x1 = acc[:, :HALF]
            x2 = acc[:, HALF:]
            o_ref[:, lo:lo + HALF] = (x1 * c - x2 * s).astype(o_ref.dtype)
            o_ref[:, lo + HALF:lo + HEAD_DIM] = (x1 * s + x2 * c).astype(o_ref.dtype)


def _inproj(xb, wt, ctab, stab, *, tm):
    m = xb.shape[0]
    n_pos = ctab.shape[1] // tm
    cast_w = wt.dtype != BF16
    assert not cast_w or m == tm
    wspec = pl.BlockSpec((WIDTH, D_MODEL), lambda j, i: (j, 0))
    ospec = pl.BlockSpec((tm, WIDTH), lambda j, i: (i, j))
    oshape = jax.ShapeDtypeStruct((m, 8 * WIDTH), BF16)

    def tab_map(j, i):
        kind = jnp.where(j == 0, 0, jnp.where(j == 1, 1, jnp.where(j == 5, 3, 2)))
        return (kind, i % n_pos, 0)

    return pl.pallas_call(
        _inproj_kernel,
        grid=(8, m // tm),
        in_specs=[
            pl.BlockSpec((tm, D_MODEL), lambda j, i: (i, 0)),
            wspec,
            pl.BlockSpec((1, tm, HALF), tab_map),
            pl.BlockSpec((1, tm, HALF), tab_map),
        ],
        out_specs=[ospec, wspec] if cast_w else ospec,
        out_shape=[oshape, jax.ShapeDtypeStruct((8 * WIDTH, D_MODEL), BF16)] if cast_w else oshape,
        compiler_params=_params("arbitrary", "arbitrary"),
        name="inproj_cast" if cast_w else "inproj",
    )(xb, wt, ctab, stab)


def _mgate_kernel(x_ref, w_ref, b_ref, xb_ref, g_ref, gt_ref):
    xb = x_ref[...].astype(BF16)
    xb_ref[...] = xb
    g = jnp.dot(xb, w_ref[...], preferred_element_type=F32) + b_ref[...]
    lane = lax.broadcasted_iota(jnp.int32, g.shape, 1)
    g = jnp.where((lane >= HEADS) & (lane < 2 * HEADS), _log_sigmoid(g), g)
    r = lax.broadcasted_iota(jnp.int32, (CHUNK, CHUNK), 0)
    s = lax.broadcasted_iota(jnp.int32, (CHUNK, CHUNK), 1)
    tril = (r >= s).astype(BF16)
    g1 = g.astype(BF16)
    res = g - g1.astype(F32)
    g2 = res.astype(BF16)
    g3 = (res - g2.astype(F32)).astype(BF16)
    parts = []
    for c in range(g.shape[0] // CHUNK):
        sl = slice(c * CHUNK, (c + 1) * CHUNK)
        parts.append(jnp.dot(tril, g1[sl], preferred_element_type=F32)
                     + jnp.dot(tril, g2[sl], preferred_element_type=F32)
                     + jnp.dot(tril, g3[sl], preferred_element_type=F32))
    cs = parts[0] if len(parts) == 1 else jnp.concatenate(parts, axis=0)
    b = pltpu.roll(cs, HEADS, 1)
    u = pltpu.roll(g, 3 * HEADS, 1) - pltpu.roll(cs, 2 * HEADS, 1)
    g = jnp.where((lane >= 2 * HEADS) & (lane < 3 * HEADS), b, g)
    g = jnp.where((lane >= 3 * HEADS) & (lane < 4 * HEADS), u, g)
    g_ref[...] = g
    gt_ref[...] = g.T[0:4 * HEADS, :]


def _mgate(x, w8, b8, *, tm):
    m = x.shape[0]
    return pl.pallas_call(
        _mgate_kernel,
        grid=(m // tm,),
        in_specs=[
            pl.BlockSpec((tm, D_MODEL), lambda i: (i, 0)),
            pl.BlockSpec((D_MODEL, LANES), lambda i: (0, 0)),
            pl.BlockSpec((1, LANES), lambda i: (0, 0)),
        ],
        out_specs=[
            pl.BlockSpec((tm, D_MODEL), lambda i: (i, 0)),
            pl.BlockSpec((tm, LANES), lambda i: (i, 0)),
            pl.BlockSpec((4 * HEADS, tm), lambda i: (0, i)),
        ],
        out_shape=[
            jax.ShapeDtypeStruct((m, D_MODEL), BF16),
            jax.ShapeDtypeStruct((m, LANES), F32),
            jax.ShapeDtypeStruct((4 * HEADS, m), F32),
        ],
        compiler_params=_params("arbitrary"),
        name="mlstm_gates",
    )(x, w8, b8)


def _prompt_mixer_kernel(qkv_ref, g_ref, gt_ref, intra_ref, dec_ref, rgn_ref, mgn_ref,
                         mix_ref, s_ref, c_ref, n_ref, m_ref, m_scr):
    c = pl.program_id(1)

    @pl.when(c == 0)
    def _():
        s_ref[...] = jnp.zeros_like(s_ref)
        c_ref[...] = jnp.zeros_like(c_ref)
        n_ref[...] = jnp.zeros_like(n_ref)
        m_scr[...] = jnp.zeros_like(m_scr)

    ti = lax.broadcasted_iota(jnp.int32, (CHUNK, CHUNK), 0)
    si = lax.broadcasted_iota(jnp.int32, (CHUNK, CHUNK), 1)
    causal = ti >= si

    def head(group, h):
        lo = group * WIDTH + h * HEAD_DIM
        return qkv_ref[:, lo:lo + HEAD_DIM]

    gw = []
    for h in range(HEADS):
        b_col = g_ref[:, 2 * HEADS + h:2 * HEADS + h + 1]
        u_col = g_ref[:, 3 * HEADS + h:3 * HEADS + h + 1]
        u_row = gt_ref[3 * HEADS + h:3 * HEADS + h + 1, :]
        m_prev = m_scr[h:h + 1, 0:1]
        dlog = jnp.where(causal, b_col + u_row, -jnp.inf)
        inter = b_col + m_prev
        mt = jnp.maximum(inter, jnp.max(dlog, axis=1, keepdims=True))
        m_new = mt[CHUNK - 1:CHUNK, :]
        b_last = b_col[CHUNK - 1:CHUNK, :]
        gw.append(dict(
            dw=jnp.exp(dlog - mt), iw=jnp.exp(inter - mt), emt=jnp.exp(-mt),
            sw=jnp.exp(b_last - m_new + u_col), sd=jnp.exp(b_last + m_prev - m_new)))
        m_scr[h:h + 1, :] = jnp.broadcast_to(m_new, (1, LANES))

    st = []
    for h in range(HEADS):
        for kind, (qg, state_ref) in enumerate(((0, s_ref), (MO_GROUP - 3, c_ref))):
            q = head(qg, h)
            k = head(qg + 1, h)
            old = state_ref[0, 0, h]
            st.append(dict(
                h=h, kind=kind, q=q, k=k, v=head(qg + 2, h), old=old,
                sc=lax.dot_general(q, k, _NT, preferred_element_type=F32),
                inter=jnp.dot(q, old.astype(BF16), preferred_element_type=F32)))

    for e in st:
        h = e["h"]
        if e["kind"] == 0:
            kd = e["k"].astype(F32) * dec_ref[HEADS + h]
            s_ref[0, 0, h] = (e["old"] * RET_STATE_DECAY[h]
                              + lax.dot_general(kd.astype(BF16), e["v"], _TN, preferred_element_type=F32))
        else:
            w = gw[h]
            ksw = e["k"].astype(F32) * w["sw"]
            c_ref[0, 0, h] = w["sd"] * e["old"] + lax.dot_general(
                ksw.astype(BF16), e["v"], _TN, preferred_element_type=F32)
            n_old = n_ref[0, 0, h:h + 1, :]
            e["qn"] = jnp.sum(e["q"].astype(F32) * n_old, axis=1, keepdims=True)
            n_ref[0, 0, h:h + 1, :] = w["sd"] * n_old + jnp.sum(ksw, axis=0, keepdims=True)

    for e in st:
        h = e["h"]
        e["sc"] = e["sc"] * (intra_ref[h] if e["kind"] == 0 else gw[h]["dw"])
        e["pv"] = jnp.dot(e["sc"].astype(BF16), e["v"], preferred_element_type=F32)

    for e in st:
        h = e["h"]
        lo = h * HEAD_DIM
        if e["kind"] == 0:
            o = e["pv"] + e["inter"] * dec_ref[h]
            y = _head_norm(o) * rgn_ref[:, lo:lo + HEAD_DIM] * head(RG_GROUP, h).astype(F32)
            mix_ref[:, lo:lo + HEAD_DIM] = y.astype(mix_ref.dtype)
        else:
            w = gw[h]
            num = e["pv"] + w["iw"] * e["inter"]
            den = jnp.sum(e["sc"], axis=1, keepdims=True) + w["iw"] * e["qn"]
            hid = num * (1.0 / jnp.maximum(jnp.abs(den), w["emt"]))
            y = _head_norm(hid) * mgn_ref[:, lo:lo + HEAD_DIM] * head(MO_GROUP, h).astype(F32)
            mix_ref[:, WIDTH + lo:WIDTH + lo + HEAD_DIM] = y.astype(mix_ref.dtype)

    @pl.when(c == pl.num_programs(1) - 1)
    def _():
        m_ref[0] = m_scr[...]


def _prompt_mixer(qkv, g_tok, g_t, intra, dec, rgn, mgn, *, batch, seq):
    nc = seq // CHUNK
    row = lambda b, c: (b * nc + c, 0)
    state = pl.BlockSpec((1, 1, HEADS, HEAD_DIM, HEAD_DIM), lambda b, c: (0, b, 0, 0, 0))
    return pl.pallas_call(
        _prompt_mixer_kernel,
        grid=(batch, nc),
        in_specs=[
            pl.BlockSpec((CHUNK, 8 * WIDTH), row),
            pl.BlockSpec((CHUNK, LANES), row),
            pl.BlockSpec((4 * HEADS, CHUNK), lambda b, c: (0, b * nc + c)),
            pl.BlockSpec((HEADS, CHUNK, CHUNK), lambda b, c: (0, 0, 0)),
            pl.BlockSpec((2 * HEADS, CHUNK, HEAD_DIM), lambda b, c: (0, 0, 0)),
            pl.BlockSpec((1, WIDTH), lambda b, c: (0, 0)),
            pl.BlockSpec((1, WIDTH), lambda b, c: (0, 0)),
        ],
        out_specs=[
            pl.BlockSpec((CHUNK, 2 * WIDTH), row),
            state,
            state,
            pl.BlockSpec((1, 1, HEADS, HEAD_DIM), lambda b, c: (0, b, 0, 0)),
            pl.BlockSpec((1, 8, LANES), lambda b, c: (b, 0, 0)),
        ],
        out_shape=[
            jax.ShapeDtypeStruct((batch * seq, 2 * WIDTH), BF16),
            jax.ShapeDtypeStruct((1, batch, HEADS, HEAD_DIM, HEAD_DIM), F32),
            jax.ShapeDtypeStruct((1, batch, HEADS, HEAD_DIM, HEAD_DIM), F32),
            jax.ShapeDtypeStruct((1, batch, HEADS, HEAD_DIM), F32),
            jax.ShapeDtypeStruct((batch, 8, LANES), F32),
        ],
        scratch_shapes=[pltpu.VMEM((8, LANES), F32)],
        compiler_params=_params("arbitrary", "arbitrary"),
        name="prompt_mixer",
    )(qkv, g_tok, g_t, intra, dec, rgn, mgn)


def _sample_gate_kernel(g_ref, m0_ref, mt_ref, dw_ref, iw_ref, emt_ref):
    ig = g_ref[...]
    lf = pltpu.roll(ig, LANES - HEADS, 1)
    inter = lf + m0_ref[...]
    mt = jnp.maximum(inter, ig)
    mt_ref[...] = mt
    dw_ref[...] = jnp.exp(ig - mt)
    iw_ref[...] = jnp.exp(inter - mt)
    emt_ref[...] = jnp.exp(-mt)


def _sample_gates(g_tok, m0_pad):
    m = g_tok.shape[0]
    spec = pl.BlockSpec((m, LANES), lambda: (0, 0))
    return pl.pallas_call(
        _sample_gate_kernel,
        in_specs=[spec, spec],
        out_specs=[spec] * 4,
        out_shape=[jax.ShapeDtypeStruct((m, LANES), F32)] * 4,
        name="sample_gates",
    )(g_tok, m0_pad)


def _sample_mixer_kernel(scal_ref, cols_ref, rows_ref, s0_ref, c0_ref, n0_ref,
                         rgn_ref, mgn_ref, mix_ref, s_ref, c_ref, n_ref):
    sub = lax.broadcasted_iota(jnp.int32, (4 * HEADS, HEAD_DIM), 0)
    for t in range(SAMPLE_TOKENS_PER_STEP):
        b = pl.program_id(0) * SAMPLE_TOKENS_PER_STEP + t
        kcols = cols_ref[t].astype(BF16)

        def row(group, h):
            lo = group * WIDTH + h * HEAD_DIM
            return rows_ref[t, :, lo:lo + HEAD_DIM]

        def outer(col, v):
            sel = jnp.where(sub == col, jnp.broadcast_to(v, (4 * HEADS, HEAD_DIM)), 0.0).astype(BF16)
            return jnp.dot(kcols, sel, preferred_element_type=F32)

        def apply(q_row, state):
            q8 = jnp.broadcast_to(q_row, (8, HEAD_DIM)).astype(BF16)
            return jnp.dot(q8, state.astype(BF16), preferred_element_type=F32)[0:1, :]

        dw = [scal_ref[b, HEADS + h] for h in range(HEADS)]
        iw = [scal_ref[b, 2 * HEADS + h] for h in range(HEADS)]
        emt = [scal_ref[b, 3 * HEADS + h] for h in range(HEADS)]
        kv_s = [outer(h, row(2, h)) for h in range(HEADS)]
        kv_c = [outer(HEADS + h, dw[h] * row(6, h)) for h in range(HEADS)]
        s_new, c_new = [], []
        for h in range(HEADS):
            s_new.append(RET_G[h] * s0_ref[0, t, h] + kv_s[h])
            s_ref[0, t, h] = s_new[h]
            c_new.append(iw[h] * c0_ref[0, t, h] + kv_c[h])
            c_ref[0, t, h] = c_new[h]
        o = [apply(row(0, h), s_new[h]) for h in range(HEADS)]
        num = [apply(row(4, h), c_new[h]) for h in range(HEADS)]
        for h in range(HEADS):
            lo = h * HEAD_DIM
            mix_ref[t, :, lo:lo + HEAD_DIM] = (_head_norm(o[h]) * rgn_ref[:, lo:lo + HEAD_DIM]
                                               * row(RG_GROUP, h))
            n_new = iw[h] * n0_ref[0, t, h:h + 1, :] + dw[h] * row(5, h)
            n_ref[0, t, h:h + 1, :] = n_new
            den = jnp.sum(row(4, h) * n_new, axis=1, keepdims=True)
            hid = num[h] * (1.0 / jnp.maximum(jnp.abs(den), emt[h]))
            mix_ref[t, :, WIDTH + lo:WIDTH + lo + HEAD_DIM] = (
                _head_norm(hid) * mgn_ref[:, lo:lo + HEAD_DIM] * row(MO_GROUP, h))


def _sample_mixer(scal, cols, rows, s0, c0, n0, rgn, mgn):
    nb = cols.shape[0]
    tb = SAMPLE_TOKENS_PER_STEP
    state = pl.BlockSpec((1, tb, HEADS, HEAD_DIM, HEAD_DIM), lambda b: (0, b, 0, 0, 0))
    nspec = pl.BlockSpec((1, tb, HEADS, HEAD_DIM), lambda b: (0, b, 0, 0))
    return pl.pallas_call(
        _sample_mixer_kernel,
        grid=(nb // tb,),
        in_specs=[
            pl.BlockSpec(memory_space=pltpu.SMEM),
            pl.BlockSpec((tb, HEAD_DIM, 4 * HEADS), lambda b: (b, 0, 0)),
            pl.BlockSpec((tb, 1, 8 * WIDTH), lambda b: (b, 0, 0)),
            state, state, nspec,
            pl.BlockSpec((1, WIDTH), lambda b: (0, 0)),
            pl.BlockSpec((1, WIDTH), lambda b: (0, 0)),
        ],
        out_specs=[pl.BlockSpec((tb, 1, 2 * WIDTH), lambda b: (b, 0, 0)), state, state, nspec],
        out_shape=[
            jax.ShapeDtypeStruct((nb, 1, 2 * WIDTH), F32),
            jax.ShapeDtypeStruct(s0.shape, F32),
            jax.ShapeDtypeStruct(c0.shape, F32),
            jax.ShapeDtypeStruct(n0.shape, F32),
        ],
        compiler_params=_params("arbitrary"),
        name="sample_mixer",
    )(scal, cols, rows, s0, c0, n0, rgn, mgn)


def _outproj_kernel(a_ref, w_ref, x_ref, g_ref, b_ref, x1_ref, x1b_ref, *wb_out):
    wb = w_ref[...]
    if wb_out:
        wb = wb.astype(BF16)
        wb_out[0][...] = wb
    mix = jnp.dot(a_ref[...].astype(BF16), wb, preferred_element_type=F32)
    x1 = _layer_norm(ALPHA * x_ref[...] + mix, g_ref[...], b_ref[...])
    x1_ref[...] = x1
    x1b_ref[...] = x1.astype(BF16)


def _outproj(a, w, x, g, b, *, tm):
    m = a.shape[0]
    cast_w = w.dtype != BF16
    assert not cast_w or m == tm
    row = pl.BlockSpec((tm, D_MODEL), lambda i: (i, 0))
    vec = pl.BlockSpec((1, D_MODEL), lambda i: (0, 0))
    out_specs = [row, row]
    out_shape = [jax.ShapeDtypeStruct((m, D_MODEL), F32), jax.ShapeDtypeStruct((m, D_MODEL), BF16)]
    if cast_w:
        out_specs.append(pl.BlockSpec((D_MODEL, D_MODEL), lambda i: (0, 0)))
        out_shape.append(jax.ShapeDtypeStruct((D_MODEL, D_MODEL), BF16))
    return pl.pallas_call(
        _outproj_kernel,
        grid=(m // tm,),
        in_specs=[row, pl.BlockSpec((D_MODEL, D_MODEL), lambda i: (0, 0), pipeline_mode=pl.Buffered(1)),
                  row, vec, vec],
        out_specs=out_specs,
        out_shape=out_shape,
        compiler_params=_params("arbitrary"),
        name="outproj_ln1_cast" if cast_w else "outproj_ln1",
    )(a, w, x, g, b)


def _ffn_kernel(x1b_ref, w1_ref, w2_ref, o_ref, *wb_out):
    @pl.when(pl.program_id(1) == 0)
    def _():
        o_ref[...] = jnp.zeros_like(o_ref)

    w1 = w1_ref[...]
    w2 = w2_ref[...]
    if wb_out:
        w1 = w1.astype(BF16)
        w2 = w2.astype(BF16)
        wb_out[0][...] = w1
        wb_out[1][...] = w2
    hid = jnp.dot(x1b_ref[...], w1, preferred_element_type=F32)
    hid = jnp.square(jnp.maximum(hid, 0.0)).astype(BF16)
    o_ref[...] += jnp.dot(hid, w2, preferred_element_type=F32)


def _ffn(x1b, w1, w2, *, tm, tf):
    m = x1b.shape[0]
    cast_w = w1.dtype != BF16
    assert not cast_w or m == tm
    row = pl.BlockSpec((tm, D_MODEL), lambda i, f: (i, 0))
    w1spec = pl.BlockSpec((D_MODEL, tf), lambda i, f: (0, f))
    w2spec = pl.BlockSpec((tf, D_MODEL), lambda i, f: (f, 0))
    oshape = jax.ShapeDtypeStruct((m, D_MODEL), F32)
    return pl.pallas_call(
        _ffn_kernel,
        grid=(m // tm, D_FF // tf),
        in_specs=[row, w1spec, w2spec],
        out_specs=[row, w1spec, w2spec] if cast_w else row,
        out_shape=[oshape, jax.ShapeDtypeStruct(w1.shape, BF16),
                   jax.ShapeDtypeStruct(w2.shape, BF16)] if cast_w else oshape,
        compiler_params=_params("arbitrary", "arbitrary"),
        name="ffn_cast" if cast_w else "ffn",
    )(x1b, w1, w2)


def _pe_ln2_kernel(x1_ref, x1b_ref, p_ref, ff_ref, wpe_ref, wg_ref, g_ref, b_ref, o_ref, *wb_out):
    wpe = wpe_ref[...]
    wg = wg_ref[...]
    if wb_out:
        wpe = wpe.astype(BF16)
        wg = wg.astype(BF16)
        wb_out[0][...] = wpe
        wb_out[1][...] = wg
    gate = _sigmoid(jnp.dot(x1b_ref[...], wg, preferred_element_type=F32))
    pe = jnp.dot(p_ref[...].astype(BF16), wpe, preferred_element_type=F32)
    y = ALPHA * x1_ref[...] + ff_ref[...] + pe * gate
    o_ref[...] = _layer_norm(y, g_ref[...], b_ref[...])


def _pe_ln2(x1, x1b, p, ff, wpe, wg, g, b, *, tm):
    m = x1.shape[0]
    cast_w = wg.dtype != BF16
    assert not cast_w or m == tm
    row = pl.BlockSpec((tm, D_MODEL), lambda i: (i, 0))
    vec = pl.BlockSpec((1, D_MODEL), lambda i: (0, 0))
    oshape = jax.ShapeDtypeStruct((m, D_MODEL), F32)
    wb_specs = [pl.BlockSpec((PLE_DIM, D_MODEL), lambda i: (0, 0)),
                pl.BlockSpec((D_MODEL, D_MODEL), lambda i: (0, 0))]
    wb_shapes = [jax.ShapeDtypeStruct(wpe.shape, BF16), jax.ShapeDtypeStruct(wg.shape, BF16)]
    return pl.pallas_call(
        _pe_ln2_kernel,
        grid=(m // tm,),
        in_specs=[row, row, pl.BlockSpec((tm, PLE_DIM), lambda i: (i, 0)), row,
                  pl.BlockSpec((PLE_DIM, D_MODEL), lambda i: (0, 0), pipeline_mode=pl.Buffered(1)),
                  pl.BlockSpec((D_MODEL, D_MODEL), lambda i: (0, 0), pipeline_mode=pl.Buffered(1)),
                  vec, vec],
        out_specs=[row] + wb_specs if cast_w else row,
        out_shape=[oshape] + wb_shapes if cast_w else oshape,
        compiler_params=_params("arbitrary"),
        name="pe_ln2_cast" if cast_w else "pe_ln2",
    )(x1, x1b, p, ff, wpe, wg, g, b)


def _rope_tables(pos):
    inv = ROPE_BASE ** (-jnp.arange(HALF, dtype=F32) * 2.0 / HEAD_DIM)
    ang = pos[:, None] * inv[None, :]
    cos, sin = jnp.cos(ang), jnp.sin(ang)
    one, zero = jnp.ones_like(cos), jnp.zeros_like(cos)
    return (jnp.stack([cos, cos * QK_SCALE, one, one * QK_SCALE]),
            jnp.stack([sin, sin * QK_SCALE, zero, zero]))


def _decay_tables():
    lg = jnp.log(1.0 - 2.0 ** (-5.0 - jnp.arange(HEADS, dtype=F32)))
    t = jnp.arange(CHUNK, dtype=F32)
    causal = t[:, None] >= t[None, :]
    intra = jnp.exp(jnp.where(causal, (t[:, None] - t[None, :]) * lg[:, None, None], -jnp.inf))
    q_decay = jnp.exp(lg[:, None] * (t + 1.0))
    k_decay = jnp.exp(lg[:, None] * (CHUNK - 1.0 - t))
    dec = jnp.concatenate([q_decay, k_decay], axis=0)
    return intra, jnp.broadcast_to(dec[:, :, None], (2 * HEADS, CHUNK, HEAD_DIM))


def _tail(x, mixed, p, w, *, tm, tm_ffn, tf):
    casts = {}
    res = _outproj(mixed, w["out"], x, w["ln1_g"], w["ln1_b"], tm=tm)
    x1, x1b = res[0], res[1]
    if len(res) > 2:
        casts["out"] = res[2]
    res = _ffn(x1b, w["ff1"], w["ff2"], tm=tm_ffn, tf=tf)
    if isinstance(res, (list, tuple)):
        ff, casts["ff1"], casts["ff2"] = res
    else:
        ff = res
    res = _pe_ln2(x1, x1b, p, ff, w["pe"], w["pe_gate"], w["ln2_g"], w["ln2_b"], tm=tm)
    if isinstance(res, (list, tuple)):
        y, casts["pe"], casts["pe_gate"] = res
    else:
        y = res
    return y, casts


def kernel(x_prompt, x_sample, state_ret, state_mlstm_C, state_mlstm_n, state_mlstm_m, p_prompt, p_sample, w_in, b_gate, ret_gn_w, mlstm_gn_w, w_out, ln1_g, ln1_b, w_ff1, w_ff2, w_pe, w_pe_gate, ln2_g, ln2_b):
    batch, seq, _ = x_prompt.shape
    nb = x_sample.shape[0]

    w_in_t = w_in[0].T
    w8 = jnp.pad(w_in[0][:, 8 * WIDTH:], ((0, 0), (0, LANES - 2 * HEADS))).astype(BF16)
    b8 = jnp.pad(b_gate[0].astype(F32), (0, LANES - 2 * HEADS)).reshape(1, LANES)
    w = {
        "out": w_out[0], "ff1": w_ff1[0], "ff2": w_ff2[0], "pe": w_pe[0], "pe_gate": w_pe_gate[0],
        "ln1_g": ln1_g[0].reshape(1, D_MODEL), "ln1_b": ln1_b[0].reshape(1, D_MODEL),
        "ln2_g": ln2_g[0].reshape(1, D_MODEL), "ln2_b": ln2_b[0].reshape(1, D_MODEL),
    }
    rgn = ret_gn_w[0].reshape(1, WIDTH)
    mgn = mlstm_gn_w[0].reshape(1, WIDTH)
    intra, dec = _decay_tables()

    xs = x_sample.reshape(nb, D_MODEL)
    xsb, gs_tok, _ = _mgate(xs, w8, b8, tm=nb)
    ctab, stab = _rope_tables(jnp.full((nb,), PAST_LEN, dtype=F32))
    proj_s, w_in_b = _inproj(xsb, w_in_t, ctab, stab, tm=nb)
    proj_s = proj_s.astype(F32)
    m0 = jnp.pad(state_mlstm_m[0], ((0, 0), (0, LANES - HEADS)))
    mt, dw, iw, emt = _sample_gates(gs_tok, m0)
    scal = jnp.concatenate([mt[:, :HEADS], dw[:, :HEADS], iw[:, :HEADS], emt[:, :HEADS]], axis=1)
    kk = jnp.concatenate([proj_s[:, WIDTH:2 * WIDTH], proj_s[:, 5 * WIDTH:6 * WIDTH],
                          jnp.zeros((nb, 2 * WIDTH), F32)], axis=1)
    cols = jnp.transpose(kk.reshape(nb, 4 * HEADS, HEAD_DIM), (0, 2, 1))
    mixed_s, s_s, c_s, n_s = _sample_mixer(
        scal, cols, proj_s.reshape(nb, 1, 8 * WIDTH),
        state_ret, state_mlstm_C, state_mlstm_n, rgn, mgn)
    m_s = mt[:, :HEADS].reshape(1, nb, HEADS)
    y_s, w_bf16 = _tail(xs, mixed_s.reshape(nb, 2 * WIDTH), p_sample[0].reshape(nb, PLE_DIM), w,
                        tm=nb, tm_ffn=nb, tf=512)

    w = {**w, **w_bf16}
    xp = x_prompt.reshape(batch * seq, D_MODEL)
    xpb, g_tok, g_t = _mgate(xp, w8, b8, tm=1024)
    ctab, stab = _rope_tables(jnp.arange(seq, dtype=F32))
    proj_p = _inproj(xpb, w_in_b, ctab, stab, tm=2048)
    mixed_p, s_p, c_p, n_p, m_rows = _prompt_mixer(proj_p, g_tok, g_t, intra, dec, rgn, mgn,
                                                   batch=batch, seq=seq)
    m_p = m_rows[:, :HEADS, 0].reshape(1, batch, HEADS)
    y_p, _ = _tail(xp, mixed_p, p_prompt[0].reshape(batch * seq, PLE_DIM), w,
                   tm=512, tm_ffn=1024, tf=1024)

    return (y_p.reshape(batch, seq, D_MODEL), y_s.reshape(nb, 1, D_MODEL),
            s_p, c_p, n_p, m_p, s_s, c_s, n_s, m_s)
```

```python
import jax
import jax.numpy as jnp
import numpy as np
from jax import lax
from jax.experimental import pallas as pl
from jax.experimental.pallas import tpu as pltpu

F32 = jnp.float32
BF16 = jnp.bfloat16

D_MODEL = 2048
HEADS = 4
HEAD_DIM = 256
HALF = HEAD_DIM // 2
WIDTH = HEADS * HEAD_DIM
D_FF = 4 * D_MODEL
PLE_DIM = 256
CHUNK = 128
PAST_LEN = 16384
ROPE_BASE = 10000.0
LN_EPS = 1e-5
GN_EPS = 1e-6
DEPTH = 1
ALPHA = (2 * DEPTH) ** 0.25
QK_SCALE = HEAD_DIM ** -0.5
LANES = 128
MXU_ROWS = 256
VMEM_LIMIT = 56 * 1024 * 1024
SAMPLE_TOKENS_PER_STEP = 4
MIXER_CHUNKS_PER_STEP = 2

RET_G = tuple(1.0 - 2.0 ** (-5.0 - h) for h in range(HEADS))
RET_LOG_G = tuple(float(np.log(np.float32(g))) for g in RET_G)
RET_STATE_DECAY = tuple(float(np.exp(np.float32(lg) * np.float32(CHUNK))) for lg in RET_LOG_G)

_NT = (((1,), (1,)), ((), ()))
_TN = (((0,), (0,)), ((), ()))


def _params(*sem):
    return pltpu.CompilerParams(dimension_semantics=sem, vmem_limit_bytes=VMEM_LIMIT)


def _sigmoid(x):
    return 0.5 * jnp.tanh(0.5 * x) + 0.5


def _log_sigmoid(x):
    return jnp.minimum(x, 0.0) - jnp.log(1.0 + jnp.exp(-jnp.abs(x)))


def _layer_norm(y, g, b):
    mu = jnp.mean(y, axis=-1, keepdims=True)
    yc = y - mu
    var = jnp.mean(jnp.square(yc), axis=-1, keepdims=True)
    return yc * lax.rsqrt(var + LN_EPS) * g + b


def _row_splits(tm):
    if tm < 2 * MXU_ROWS:
        return [slice(0, tm)]
    return [slice(0, tm // 2), slice(tm // 2, tm)]


def _head_norm(o):
    mu = jnp.mean(o, axis=-1, keepdims=True)
    oc = o - mu
    var = jnp.mean(jnp.square(oc), axis=-1, keepdims=True)
    return oc * lax.rsqrt(var + GN_EPS)


RG_GROUP, MO_GROUP = 3, 7


def _inproj_kernel(x_ref, wt_ref, c_ref, s_ref, o_ref, *wb_out):
    j = pl.program_id(0)
    if wb_out:
        wb_ref, = wb_out
        wb_ref[...] = wt_ref[...].astype(BF16)
    else:
        wb_ref = wt_ref

    is_gate = (j == RG_GROUP) | (j == MO_GROUP)

    def head_acc(h):
        return lax.dot_general(x_ref[...], wb_ref[h * HEAD_DIM:(h + 1) * HEAD_DIM, :], _NT,
                               preferred_element_type=F32)

    @pl.when(is_gate)
    def _():
        for h in range(HEADS):
            acc = head_acc(h)
            sig = _sigmoid(acc)
            o_ref[:, h * HEAD_DIM:(h + 1) * HEAD_DIM] = jnp.where(
                j == RG_GROUP, acc * sig, sig).astype(o_ref.dtype)

    @pl.when(jnp.logical_not(is_gate))
    def _():
        c = c_ref[0]
        s = s_ref[0]
        for h in range(HEADS):
            lo = h * HEAD_DIM
            acc = head_acc(h)
            x1 = acc[:, :HALF]
            x2 = acc[:, HALF:]
            o_ref[:, lo:lo + HALF] = (x1 * c - x2 * s).astype(o_ref.dtype)
            o_ref[:, lo + HALF:lo + HEAD_DIM] = (x1 * s + x2 * c).astype(o_ref.dtype)


def _inproj(xb, wt, ctab, stab, *, tm):
    m = xb.shape[0]
    n_pos = ctab.shape[1] // tm
    cast_w = wt.dtype != BF16
    assert not cast_w or m == tm
    wspec = pl.BlockSpec((WIDTH, D_MODEL), lambda j, i: (j, 0))
    ospec = pl.BlockSpec((tm, WIDTH), lambda j, i: (i, j))
    oshape = jax.ShapeDtypeStruct((m, 8 * WIDTH), BF16)

    def tab_map(j, i):
        kind = jnp.where(j == 0, 0, jnp.where(j == 1, 1, jnp.where(j == 5, 3, 2)))
        return (kind, i % n_pos, 0)

    return pl.pallas_call(
        _inproj_kernel,
        grid=(8, m // tm),
        in_specs=[
            pl.BlockSpec((tm, D_MODEL), lambda j, i: (i, 0)),
            wspec,
            pl.BlockSpec((1, tm, HALF), tab_map),
            pl.BlockSpec((1, tm, HALF), tab_map),
        ],
        out_specs=[ospec, wspec] if cast_w else ospec,
        out_shape=[oshape, jax.ShapeDtypeStruct((8 * WIDTH, D_MODEL), BF16)] if cast_w else oshape,
        compiler_params=_params("arbitrary", "arbitrary"),
        name="inproj_cast" if cast_w else "inproj",
    )(xb, wt, ctab, stab)


def _mgate_kernel(x_ref, w_ref, b_ref, xb_ref, g_ref, gt_ref):
    xb = x_ref[...].astype(BF16)
    xb_ref[...] = xb
    g = jnp.dot(xb, w_ref[...], preferred_element_type=F32) + b_ref[...]
    lane = lax.broadcasted_iota(jnp.int32, g.shape, 1)
    g = jnp.where((lane >= HEADS) & (lane < 2 * HEADS), _log_sigmoid(g), g)
    r = lax.broadcasted_iota(jnp.int32, (CHUNK, CHUNK), 0)
    s = lax.broadcasted_iota(jnp.int32, (CHUNK, CHUNK), 1)
    tril = (r >= s).astype(BF16)
    g1 = g.astype(BF16)
    res = g - g1.astype(F32)
    g2 = res.astype(BF16)
    g3 = (res - g2.astype(F32)).astype(BF16)
    parts = []
    for c in range(g.shape[0] // CHUNK):
        sl = slice(c * CHUNK, (c + 1) * CHUNK)
        parts.append(jnp.dot(tril, g1[sl], preferred_element_type=F32)
                     + jnp.dot(tril, g2[sl], preferred_element_type=F32)
                     + jnp.dot(tril, g3[sl], preferred_element_type=F32))
    cs = parts[0] if len(parts) == 1 else jnp.concatenate(parts, axis=0)
    b = pltpu.roll(cs, HEADS, 1)
    u = pltpu.roll(g, 3 * HEADS, 1) - pltpu.roll(cs, 2 * HEADS, 1)
    g = jnp.where((lane >= 2 * HEADS) & (lane < 3 * HEADS), b, g)
    g = jnp.where((lane >= 3 * HEADS) & (lane < 4 * HEADS), u, g)
    g_ref[...] = g
    gt_ref[...] = g.T[0:4 * HEADS, :]


def _mgate(x, w8, b8, *, tm):
    m = x.shape[0]
    return pl.pallas_call(
        _mgate_kernel,
        grid=(m // tm,),
        in_specs=[
            pl.BlockSpec((tm, D_MODEL), lambda i: (i, 0)),
            pl.BlockSpec((D_MODEL, LANES), lambda i: (0, 0)),
            pl.BlockSpec((1, LANES), lambda i: (0, 0)),
        ],
        out_specs=[
            pl.BlockSpec((tm, D_MODEL), lambda i: (i, 0)),
            pl.BlockSpec((tm, LANES), lambda i: (i, 0)),
            pl.BlockSpec((4 * HEADS, tm), lambda i: (0, i)),
        ],
        out_shape=[
            jax.ShapeDtypeStruct((m, D_MODEL), BF16),
            jax.ShapeDtypeStruct((m, LANES), F32),
            jax.ShapeDtypeStruct((4 * HEADS, m), F32),
        ],
        compiler_params=_params("arbitrary"),
        name="mlstm_gates",
    )(x, w8, b8)


def _mixer_chunk(rows, causal, qkv_ref, g_ref, gt_ref, intra_ref, dec_ref, kdec_ref, rgn_ref, mgn_ref,
                 mix_ref, s_ref, c_ref, n_ref, m_scr):
    def head(group, h):
        lo = group * WIDTH + h * HEAD_DIM
        return qkv_ref[rows, lo:lo + HEAD_DIM]

    gw = []
    for h in range(HEADS):
        b_col = g_ref[rows, 2 * HEADS + h:2 * HEADS + h + 1]
        u_row = gt_ref[3 * HEADS + h:3 * HEADS + h + 1, rows]
        m_prev = m_scr[h:h + 1, 0:1]
        dlog = jnp.where(causal, b_col + u_row, -jnp.inf)
        inter = b_col + m_prev
        mt = jnp.maximum(inter, jnp.max(dlog, axis=1, keepdims=True))
        m_new = mt[CHUNK - 1:CHUNK, :]
        b_last = b_col[CHUNK - 1:CHUNK, :]
        gw.append(dict(
            dw=jnp.exp(dlog - mt), iw=jnp.exp(inter - mt), emt=jnp.exp(-mt),
            sw=jnp.exp(b_last - m_new + u_row), sd=jnp.exp(b_last + m_prev - m_new)))
        m_scr[h:h + 1, :] = jnp.broadcast_to(m_new, (1, LANES))

    st = []
    for h in range(HEADS):
        for kind, (qg, state_ref) in enumerate(((0, s_ref), (MO_GROUP - 3, c_ref))):
            q = head(qg, h)
            k = head(qg + 1, h)
            old = state_ref[0, 0, h]
            kt = k.T
            st.append(dict(
                h=h, kind=kind, q=q, k=k, kt=kt, v=head(qg + 2, h), old=old,
                sc=jnp.dot(q, kt, preferred_element_type=F32),
                inter=jnp.dot(q, old.astype(BF16), preferred_element_type=F32)))

    for e in st:
        h = e["h"]
        if e["kind"] == 0:
            kd = (e["kt"].astype(F32) * kdec_ref[h:h + 1, :]).astype(BF16)
            s_ref[0, 0, h] = (e["old"] * RET_STATE_DECAY[h]
                              + jnp.dot(kd, e["v"], preferred_element_type=F32))
        else:
            w = gw[h]
            ksw = (e["kt"].astype(F32) * w["sw"]).astype(BF16)
            c_ref[0, 0, h] = w["sd"] * e["old"] + jnp.dot(ksw, e["v"], preferred_element_type=F32)
            n_old = n_ref[0, 0, h:h + 1, :]
            e["qn"] = jnp.sum(e["q"].astype(F32) * n_old, axis=1, keepdims=True)
            sw8 = jnp.broadcast_to(w["sw"], (8, CHUNK)).astype(BF16)
            n_ref[0, 0, h:h + 1, :] = (w["sd"] * n_old
                                       + jnp.dot(sw8, e["k"], preferred_element_type=F32)[0:1, :])

    for e in st:
        h = e["h"]
        e["sc"] = e["sc"] * (intra_ref[h] if e["kind"] == 0 else gw[h]["dw"])
        e["pv"] = jnp.dot(e["sc"].astype(BF16), e["v"], preferred_element_type=F32)

    def tail():
        dens = [jnp.sum(e["sc"], axis=1, keepdims=True) if e["kind"] else None for e in st]
        outs = []
        for e, den in zip(st, dens):
            h = e["h"]
            if e["kind"] == 0:
                outs.append(e["pv"] + e["inter"] * dec_ref[h])
            else:
                w = gw[h]
                num = e["pv"] + w["iw"] * e["inter"]
                den = den + w["iw"] * e["qn"]
                outs.append(num * (1.0 / jnp.maximum(jnp.abs(den), w["emt"])))
        mus = [jnp.mean(o, axis=-1, keepdims=True) for o in outs]
        cen = [o - mu for o, mu in zip(outs, mus)]
        var = [jnp.mean(jnp.square(oc), axis=-1, keepdims=True) for oc in cen]
        for e, oc, v in zip(st, cen, var):
            h = e["h"]
            lo = h * HEAD_DIM
            y = oc * lax.rsqrt(v + GN_EPS)
            if e["kind"] == 0:
                y = y * rgn_ref[:, lo:lo + HEAD_DIM] * head(RG_GROUP, h).astype(F32)
                mix_ref[rows, lo:lo + HEAD_DIM] = y.astype(mix_ref.dtype)
            else:
                y = y * mgn_ref[:, lo:lo + HEAD_DIM] * head(MO_GROUP, h).astype(F32)
                mix_ref[rows, WIDTH + lo:WIDTH + lo + HEAD_DIM] = y.astype(mix_ref.dtype)

    return tail


def _prompt_mixer_kernel(qkv_ref, g_ref, gt_ref, intra_ref, dec_ref, kdec_ref, rgn_ref, mgn_ref,
                         mix_ref, s_ref, c_ref, n_ref, m_ref, m_scr):
    c = pl.program_id(1)

    @pl.when(c == 0)
    def _():
        s_ref[...] = jnp.zeros_like(s_ref)
        c_ref[...] = jnp.zeros_like(c_ref)
        n_ref[...] = jnp.zeros_like(n_ref)
        m_scr[...] = jnp.zeros_like(m_scr)

    ti = lax.broadcasted_iota(jnp.int32, (CHUNK, CHUNK), 0)
    si = lax.broadcasted_iota(jnp.int32, (CHUNK, CHUNK), 1)
    causal = ti >= si

    pending = None
    for ci in range(MIXER_CHUNKS_PER_STEP):
        tail = _mixer_chunk(slice(ci * CHUNK, (ci + 1) * CHUNK), causal, qkv_ref, g_ref, gt_ref,
                            intra_ref, dec_ref, kdec_ref, rgn_ref, mgn_ref, mix_ref, s_ref, c_ref,
                            n_ref, m_scr)
        if pending is not None:
            pending()
        pending = tail
    pending()

    @pl.when(c == pl.num_programs(1) - 1)
    def _():
        m_ref[0] = m_scr[...]


def _prompt_mixer(qkv, g_tok, g_t, intra, qdec, kdec, rgn, mgn, *, batch, seq):
    rows = MIXER_CHUNKS_PER_STEP * CHUNK
    nc = seq // rows
    row = lambda b, c: (b * nc + c, 0)
    state = pl.BlockSpec((1, 1, HEADS, HEAD_DIM, HEAD_DIM), lambda b, c: (0, b, 0, 0, 0))
    return pl.pallas_call(
        _prompt_mixer_kernel,
        grid=(batch, nc),
        in_specs=[
            pl.BlockSpec((rows, 8 * WIDTH), row),
            pl.BlockSpec((rows, LANES), row),
            pl.BlockSpec((4 * HEADS, rows), lambda b, c: (0, b * nc + c)),
            pl.BlockSpec((HEADS, CHUNK, CHUNK), lambda b, c: (0, 0, 0)),
            pl.BlockSpec((HEADS, CHUNK, HEAD_DIM), lambda b, c: (0, 0, 0)),
            pl.BlockSpec((HEADS, CHUNK), lambda b, c: (0, 0)),
            pl.BlockSpec((1, WIDTH), lambda b, c: (0, 0)),
            pl.BlockSpec((1, WIDTH), lambda b, c: (0, 0)),
        ],
        out_specs=[
            pl.BlockSpec((rows, 2 * WIDTH), row),
            state,
            state,
            pl.BlockSpec((1, 1, HEADS, HEAD_DIM), lambda b, c: (0, b, 0, 0)),
            pl.BlockSpec((1, 8, LANES), lambda b, c: (b, 0, 0)),
        ],
        out_shape=[
            jax.ShapeDtypeStruct((batch * seq, 2 * WIDTH), BF16),
            jax.ShapeDtypeStruct((1, batch, HEADS, HEAD_DIM, HEAD_DIM), F32),
            jax.ShapeDtypeStruct((1, batch, HEADS, HEAD_DIM, HEAD_DIM), F32),
            jax.ShapeDtypeStruct((1, batch, HEADS, HEAD_DIM), F32),
            jax.ShapeDtypeStruct((batch, 8, LANES), F32),
        ],
        scratch_shapes=[pltpu.VMEM((8, LANES), F32)],
        compiler_params=_params("arbitrary", "arbitrary"),
        name="prompt_mixer",
    )(qkv, g_tok, g_t, intra, qdec, kdec, rgn, mgn)


def _sample_gate_kernel(g_ref, m0_ref, mt_ref, dw_ref, iw_ref, emt_ref):
    ig = g_ref[...]
    lf = pltpu.roll(ig, LANES - HEADS, 1)
    inter = lf + m0_ref[...]
    mt = jnp.maximum(inter, ig)
    mt_ref[...] = mt
    dw_ref[...] = jnp.exp(ig - mt)
    iw_ref[...] = jnp.exp(inter - mt)
    emt_ref[...] = jnp.exp(-mt)


def _sample_gates(g_tok, m0_pad):
    m = g_tok.shape[0]
    spec = pl.BlockSpec((m, LANES), lambda: (0, 0))
    return pl.pallas_call(
        _sample_gate_kernel,
        in_specs=[spec, spec],
        out_specs=[spec] * 4,
        out_shape=[jax.ShapeDtypeStruct((m, LANES), F32)] * 4,
        name="sample_gates",
    )(g_tok, m0_pad)


def _sample_mixer_kernel(scal_ref, cols_ref, rows_ref, s0_ref, c0_ref, n0_ref,
                         rgn_ref, mgn_ref, mix_ref, s_ref, c_ref, n_ref):
    sub = lax.broadcasted_iota(jnp.int32, (4 * HEADS, HEAD_DIM), 0)
    for t in range(SAMPLE_TOKENS_PER_STEP):
        b = pl.program_id(0) * SAMPLE_TOKENS_PER_STEP + t
        kcols = cols_ref[t].astype(BF16)

        def row(group, h):
            lo = group * WIDTH + h * HEAD_DIM
            return rows_ref[t, :, lo:lo + HEAD_DIM]

        def outer(col, v):
            sel = jnp.where(sub == col, jnp.broadcast_to(v, (4 * HEADS, HEAD_DIM)), 0.0).astype(BF16)
            return jnp.dot(kcols, sel, preferred_element_type=F32)

        def apply(q_row, state):
            q8 = jnp.broadcast_to(q_row, (8, HEAD_DIM)).astype(BF16)
            return jnp.dot(q8, state.astype(BF16), preferred_element_type=F32)[0:1, :]

        dw = [scal_ref[b, HEADS + h] for h in range(HEADS)]
        iw = [scal_ref[b, 2 * HEADS + h] for h in range(HEADS)]
        emt = [scal_ref[b, 3 * HEADS + h] for h in range(HEADS)]
        kv_s = [outer(h, row(2, h)) for h in range(HEADS)]
        kv_c = [outer(HEADS + h, dw[h] * row(6, h)) for h in range(HEADS)]
        s_new, c_new = [], []
        for h in range(HEADS):
            s_new.append(RET_G[h] * s0_ref[0, t, h] + kv_s[h])
            s_ref[0, t, h] = s_new[h]
            c_new.append(iw[h] * c0_ref[0, t, h] + kv_c[h])
            c_ref[0, t, h] = c_new[h]
        o = [apply(row(0, h), s_new[h]) for h in range(HEADS)]
        num = [apply(row(4, h), c_new[h]) for h in range(HEADS)]
        for h in range(HEADS):
            lo = h * HEAD_DIM
            mix_ref[t, :, lo:lo + HEAD_DIM] = (_head_norm(o[h]) * rgn_ref[:, lo:lo + HEAD_DIM]
                                               * row(RG_GROUP, h))
            n_new = iw[h] * n0_ref[0, t, h:h + 1, :] + dw[h] * row(5, h)
            n_ref[0, t, h:h + 1, :] = n_new
            den = jnp.sum(row(4, h) * n_new, axis=1, keepdims=True)
            hid = num[h] * (1.0 / jnp.maximum(jnp.abs(den), emt[h]))
            mix_ref[t, :, WIDTH + lo:WIDTH + lo + HEAD_DIM] = (
                _head_norm(hid) * mgn_ref[:, lo:lo + HEAD_DIM] * row(MO_GROUP, h))


def _sample_mixer(scal, cols, rows, s0, c0, n0, rgn, mgn):
    nb = cols.shape[0]
    tb = SAMPLE_TOKENS_PER_STEP
    state = pl.BlockSpec((1, tb, HEADS, HEAD_DIM, HEAD_DIM), lambda b: (0, b, 0, 0, 0))
    nspec = pl.BlockSpec((1, tb, HEADS, HEAD_DIM), lambda b: (0, b, 0, 0))
    return pl.pallas_call(
        _sample_mixer_kernel,
        grid=(nb // tb,),
        in_specs=[
            pl.BlockSpec(memory_space=pltpu.SMEM),
            pl.BlockSpec((tb, HEAD_DIM, 4 * HEADS), lambda b: (b, 0, 0)),
            pl.BlockSpec((tb, 1, 8 * WIDTH), lambda b: (b, 0, 0)),
            state, state, nspec,
            pl.BlockSpec((1, WIDTH), lambda b: (0, 0)),
            pl.BlockSpec((1, WIDTH), lambda b: (0, 0)),
        ],
        out_specs=[pl.BlockSpec((tb, 1, 2 * WIDTH), lambda b: (b, 0, 0)), state, state, nspec],
        out_shape=[
            jax.ShapeDtypeStruct((nb, 1, 2 * WIDTH), F32),
            jax.ShapeDtypeStruct(s0.shape, F32),
            jax.ShapeDtypeStruct(c0.shape, F32),
            jax.ShapeDtypeStruct(n0.shape, F32),
        ],
        compiler_params=_params("arbitrary"),
        name="sample_mixer",
    )(scal, cols, rows, s0, c0, n0, rgn, mgn)


def _outproj_kernel(a_ref, w_ref, x_ref, g_ref, b_ref, x1_ref, x1b_ref, *wb_out):
    if wb_out:
        wb_out[0][...] = w_ref[...].astype(BF16)
        w_ref = wb_out[0]
    tm = a_ref.shape[0]
    for rows in _row_splits(tm):
        mix = jnp.dot(a_ref[rows, :].astype(BF16), w_ref[...], preferred_element_type=F32)
        x1 = _layer_norm(ALPHA * x_ref[rows, :] + mix, g_ref[...], b_ref[...])
        x1_ref[rows, :] = x1
        x1b_ref[rows, :] = x1.astype(BF16)


def _outproj(a, w, x, g, b, *, tm):
    m = a.shape[0]
    cast_w = w.dtype != BF16
    assert not cast_w or m == tm
    row = pl.BlockSpec((tm, D_MODEL), lambda i: (i, 0))
    vec = pl.BlockSpec((1, D_MODEL), lambda i: (0, 0))
    out_specs = [row, row]
    out_shape = [jax.ShapeDtypeStruct((m, D_MODEL), F32), jax.ShapeDtypeStruct((m, D_MODEL), BF16)]
    if cast_w:
        out_specs.append(pl.BlockSpec((D_MODEL, D_MODEL), lambda i: (0, 0)))
        out_shape.append(jax.ShapeDtypeStruct((D_MODEL, D_MODEL), BF16))
    return pl.pallas_call(
        _outproj_kernel,
        grid=(m // tm,),
        in_specs=[row, pl.BlockSpec((D_MODEL, D_MODEL), lambda i: (0, 0), pipeline_mode=pl.Buffered(1)),
                  row, vec, vec],
        out_specs=out_specs,
        out_shape=out_shape,
        compiler_params=_params("arbitrary"),
        name="outproj_ln1_cast" if cast_w else "outproj_ln1",
    )(a, w, x, g, b)


def _ffn_kernel(x1b_ref, w1_ref, w2_ref, o_ref, *wb_out):
    @pl.when(pl.program_id(1) == 0)
    def _():
        o_ref[...] = jnp.zeros_like(o_ref)

    w1 = w1_ref[...]
    w2 = w2_ref[...]
    if wb_out:
        w1 = w1.astype(BF16)
        w2 = w2.astype(BF16)
        wb_out[0][...] = w1
        wb_out[1][...] = w2
    hid = jnp.dot(x1b_ref[...], w1, preferred_element_type=F32)
    hid = jnp.square(jnp.maximum(hid, 0.0)).astype(BF16)
    o_ref[...] += jnp.dot(hid, w2, preferred_element_type=F32)


def _ffn(x1b, w1, w2, *, tm, tf):
    m = x1b.shape[0]
    cast_w = w1.dtype != BF16
    assert not cast_w or m == tm
    row = pl.BlockSpec((tm, D_MODEL), lambda i, f: (i, 0))
    w1spec = pl.BlockSpec((D_MODEL, tf), lambda i, f: (0, f))
    w2spec = pl.BlockSpec((tf, D_MODEL), lambda i, f: (f, 0))
    oshape = jax.ShapeDtypeStruct((m, D_MODEL), F32)
    return pl.pallas_call(
        _ffn_kernel,
        grid=(m // tm, D_FF // tf),
        in_specs=[row, w1spec, w2spec],
        out_specs=[row, w1spec, w2spec] if cast_w else row,
        out_shape=[oshape, jax.ShapeDtypeStruct(w1.shape, BF16),
                   jax.ShapeDtypeStruct(w2.shape, BF16)] if cast_w else oshape,
        compiler_params=_params("arbitrary", "arbitrary"),
        name="ffn_cast" if cast_w else "ffn",
    )(x1b, w1, w2)


def _pe_ln2_kernel(x1_ref, x1b_ref, p_ref, ff_ref, wpe_ref, wg_ref, g_ref, b_ref, o_ref, *wb_out):
    if wb_out:
        wb_out[0][...] = wpe_ref[...].astype(BF16)
        wb_out[1][...] = wg_ref[...].astype(BF16)
        wpe_ref, wg_ref = wb_out
    for rows in _row_splits(x1_ref.shape[0]):
        gate = _sigmoid(jnp.dot(x1b_ref[rows, :], wg_ref[...], preferred_element_type=F32))
        pe = jnp.dot(p_ref[rows, :].astype(BF16), wpe_ref[...], preferred_element_type=F32)
        y = ALPHA * x1_ref[rows, :] + ff_ref[rows, :] + pe * gate
        o_ref[rows, :] = _layer_norm(y, g_ref[...], b_ref[...])


def _pe_ln2(x1, x1b, p, ff, wpe, wg, g, b, *, tm):
    m = x1.shape[0]
    cast_w = wg.dtype != BF16
    assert not cast_w or m == tm
    row = pl.BlockSpec((tm, D_MODEL), lambda i: (i, 0))
    vec = pl.BlockSpec((1, D_MODEL), lambda i: (0, 0))
    oshape = jax.ShapeDtypeStruct((m, D_MODEL), F32)
    wb_specs = [pl.BlockSpec((PLE_DIM, D_MODEL), lambda i: (0, 0)),
                pl.BlockSpec((D_MODEL, D_MODEL), lambda i: (0, 0))]
    wb_shapes = [jax.ShapeDtypeStruct(wpe.shape, BF16), jax.ShapeDtypeStruct(wg.shape, BF16)]
    return pl.pallas_call(
        _pe_ln2_kernel,
        grid=(m // tm,),
        in_specs=[row, row, pl.BlockSpec((tm, PLE_DIM), lambda i: (i, 0)), row,
                  pl.BlockSpec((PLE_DIM, D_MODEL), lambda i: (0, 0), pipeline_mode=pl.Buffered(1)),
                  pl.BlockSpec((D_MODEL, D_MODEL), lambda i: (0, 0), pipeline_mode=pl.Buffered(1)),
                  vec, vec],
        out_specs=[row] + wb_specs if cast_w else row,
        out_shape=[oshape] + wb_shapes if cast_w else oshape,
        compiler_params=_params("arbitrary"),
        name="pe_ln2_cast" if cast_w else "pe_ln2",
    )(x1, x1b, p, ff, wpe, wg, g, b)


def _rope_tables(pos):
    inv = ROPE_BASE ** (-jnp.arange(HALF, dtype=F32) * 2.0 / HEAD_DIM)
    ang = pos[:, None] * inv[None, :]
    cos, sin = jnp.cos(ang), jnp.sin(ang)
    one, zero = jnp.ones_like(cos), jnp.zeros_like(cos)
    return (jnp.stack([cos, cos * QK_SCALE, one, one * QK_SCALE]),
            jnp.stack([sin, sin * QK_SCALE, zero, zero]))


def _decay_tables():
    lg = jnp.log(1.0 - 2.0 ** (-5.0 - jnp.arange(HEADS, dtype=F32)))
    t = jnp.arange(CHUNK, dtype=F32)
    causal = t[:, None] >= t[None, :]
    intra = jnp.exp(jnp.where(causal, (t[:, None] - t[None, :]) * lg[:, None, None], -jnp.inf))
    q_decay = jnp.exp(lg[:, None] * (t + 1.0))
    k_decay = jnp.exp(lg[:, None] * (CHUNK - 1.0 - t))
    return intra, jnp.broadcast_to(q_decay[:, :, None], (HEADS, CHUNK, HEAD_DIM)), k_decay


def _tail(x, mixed, p, w, *, tm, tm_ffn, tf):
    casts = {}
    res = _outproj(mixed, w["out"], x, w["ln1_g"], w["ln1_b"], tm=tm)
    x1, x1b = res[0], res[1]
    if len(res) > 2:
        casts["out"] = res[2]
    res = _ffn(x1b, w["ff1"], w["ff2"], tm=tm_ffn, tf=tf)
    if isinstance(res, (list, tuple)):
        ff, casts["ff1"], casts["ff2"] = res
    else:
        ff = res
    res = _pe_ln2(x1, x1b, p, ff, w["pe"], w["pe_gate"], w["ln2_g"], w["ln2_b"], tm=tm)
    if isinstance(res, (list, tuple)):
        y, casts["pe"], casts["pe_gate"] = res
    else:
        y = res
    return y, casts


def kernel(x_prompt, x_sample, state_ret, state_mlstm_C, state_mlstm_n, state_mlstm_m, p_prompt, p_sample, w_in, b_gate, ret_gn_w, mlstm_gn_w, w_out, ln1_g, ln1_b, w_ff1, w_ff2, w_pe, w_pe_gate, ln2_g, ln2_b):
    batch, seq, _ = x_prompt.shape
    nb = x_sample.shape[0]

    w_in_t = w_in[0].T
    w8 = jnp.pad(w_in[0][:, 8 * WIDTH:], ((0, 0), (0, LANES - 2 * HEADS))).astype(BF16)
    b8 = jnp.pad(b_gate[0].astype(F32), (0, LANES - 2 * HEADS)).reshape(1, LANES)
    w = {
        "out": w_out[0], "ff1": w_ff1[0], "ff2": w_ff2[0], "pe": w_pe[0], "pe_gate": w_pe_gate[0],
        "ln1_g": ln1_g[0].reshape(1, D_MODEL), "ln1_b": ln1_b[0].reshape(1, D_MODEL),
        "ln2_g": ln2_g[0].reshape(1, D_MODEL), "ln2_b": ln2_b[0].reshape(1, D_MODEL),
    }
    rgn = ret_gn_w[0].reshape(1, WIDTH)
    mgn = mlstm_gn_w[0].reshape(1, WIDTH)
    intra, qdec, kdec = _decay_tables()

    xs = x_sample.reshape(nb, D_MODEL)
    xsb, gs_tok, _ = _mgate(xs, w8, b8, tm=nb)
    ctab, stab = _rope_tables(jnp.full((nb,), PAST_LEN, dtype=F32))
    proj_s, w_in_b = _inproj(xsb, w_in_t, ctab, stab, tm=nb)
    proj_s = proj_s.astype(F32)
    m0 = jnp.pad(state_mlstm_m[0], ((0, 0), (0, LANES - HEADS)))
    mt, dw, iw, emt = _sample_gates(gs_tok, m0)
    scal = jnp.concatenate([mt[:, :HEADS], dw[:, :HEADS], iw[:, :HEADS], emt[:, :HEADS]], axis=1)
    kk = jnp.concatenate([proj_s[:, WIDTH:2 * WIDTH], proj_s[:, 5 * WIDTH:6 * WIDTH],
                          jnp.zeros((nb, 2 * WIDTH), F32)], axis=1)
    cols = jnp.transpose(kk.reshape(nb, 4 * HEADS, HEAD_DIM), (0, 2, 1))
    mixed_s, s_s, c_s, n_s = _sample_mixer(
        scal, cols, proj_s.reshape(nb, 1, 8 * WIDTH),
        state_ret, state_mlstm_C, state_mlstm_n, rgn, mgn)
    m_s = mt[:, :HEADS].reshape(1, nb, HEADS)
    y_s, w_bf16 = _tail(xs, mixed_s.reshape(nb, 2 * WIDTH), p_sample[0].reshape(nb, PLE_DIM), w,
                        tm=nb, tm_ffn=nb, tf=512)

    w = {**w, **w_bf16}
    xp = x_prompt.reshape(batch * seq, D_MODEL)
    xpb, g_tok, g_t = _mgate(xp, w8, b8, tm=1024)
    ctab, stab = _rope_tables(jnp.arange(seq, dtype=F32))
    proj_p = _inproj(xpb, w_in_b, ctab, stab, tm=2048)
    mixed_p, s_p, c_p, n_p, m_rows = _prompt_mixer(proj_p, g_tok, g_t, intra, qdec, kdec, rgn, mgn,
                                                   batch=batch, seq=seq)
    m_p = m_rows[:, :HEADS, 0].reshape(1, batch, HEADS)
    y_p, _ = _tail(xp, mixed_p, p_prompt[0].reshape(batch * seq, PLE_DIM), w,
                   tm=512, tm_ffn=1024, tf=1024)

    return (y_p.reshape(batch, seq, D_MODEL), y_s.reshape(nb, 1, D_MODEL),
            s_p, c_p, n_p, m_p, s_s, c_s, n_s, m_s)
```

```python
import jax
import jax.numpy as jnp
import numpy as np
from jax import lax
from jax.experimental import pallas as pl
from jax.experimental.pallas import tpu as pltpu

F32 = jnp.float32
BF16 = jnp.bfloat16

D_MODEL = 2048
HEADS = 4
HEAD_DIM = 256
HALF = HEAD_DIM // 2
WIDTH = HEADS * HEAD_DIM
D_FF = 4 * D_MODEL
PLE_DIM = 256
CHUNK = 128
PAST_LEN = 16384
ROPE_BASE = 10000.0
LN_EPS = 1e-5
GN_EPS = 1e-6
DEPTH = 1
ALPHA = (2 * DEPTH) ** 0.25
QK_SCALE = HEAD_DIM ** -0.5
LANES = 128
MXU_ROWS = 256
VMEM_LIMIT = 56 * 1024 * 1024
SAMPLE_TOKENS_PER_STEP = 4
MIXER_CHUNKS_PER_STEP = 2

RET_G = tuple(1.0 - 2.0 ** (-5.0 - h) for h in range(HEADS))
RET_LOG_G = tuple(float(np.log(np.float32(g))) for g in RET_G)
RET_STATE_DECAY = tuple(float(np.exp(np.float32(lg) * np.float32(CHUNK))) for lg in RET_LOG_G)

_NT = (((1,), (1,)), ((), ()))
_TN = (((0,), (0,)), ((), ()))


def _params(*sem):
    return pltpu.CompilerParams(dimension_semantics=sem, vmem_limit_bytes=VMEM_LIMIT)


def _sigmoid(x):
    return 0.5 * jnp.tanh(0.5 * x) + 0.5


def _log_sigmoid(x):
    return jnp.minimum(x, 0.0) - jnp.log(1.0 + jnp.exp(-jnp.abs(x)))


def _layer_norm(y, g, b):
    mu = jnp.mean(y, axis=-1, keepdims=True)
    yc = y - mu
    var = jnp.mean(jnp.square(yc), axis=-1, keepdims=True)
    return yc * lax.rsqrt(var + LN_EPS) * g + b


def _row_splits(tm):
    if tm < 2 * MXU_ROWS:
        return [slice(0, tm)]
    return [slice(0, tm // 2), slice(tm // 2, tm)]


def _head_norm(o):
    mu = jnp.mean(o, axis=-1, keepdims=True)
    oc = o - mu
    var = jnp.mean(jnp.square(oc), axis=-1, keepdims=True)
    return oc * lax.rsqrt(var + GN_EPS)


RG_GROUP, MO_GROUP = 3, 7


def _inproj_kernel(x_ref, wt_ref, c_ref, s_ref, o_ref, *wb_out):
    j = pl.program_id(0)
    if wb_out:
        wb_ref, = wb_out
        wb_ref[...] = wt_ref[...].astype(BF16)
    else:
        wb_ref = wt_ref

    is_gate = (j == RG_GROUP) | (j == MO_GROUP)
    pieces = [(h, rows) for h in range(HEADS) for rows in _row_splits(x_ref.shape[0])]

    def piece_acc(h, rows):
        return lax.dot_general(x_ref[rows, :], wb_ref[h * HEAD_DIM:(h + 1) * HEAD_DIM, :], _NT,
                               preferred_element_type=F32)

    @pl.when(is_gate)
    def _():
        for h, rows in pieces:
            acc = piece_acc(h, rows)
            sig = _sigmoid(acc)
            o_ref[rows, h * HEAD_DIM:(h + 1) * HEAD_DIM] = jnp.where(
                j == RG_GROUP, acc * sig, sig).astype(o_ref.dtype)

    @pl.when(jnp.logical_not(is_gate))
    def _():
        for h, rows in pieces:
            lo = h * HEAD_DIM
            c = c_ref[0, rows, :]
            s = s_ref[0, rows, :]
            acc = piece_acc(h, rows)
            x1 = acc[:, :HALF]
            x2 = acc[:, HALF:]
            o_ref[rows, lo:lo + HALF] = (x1 * c - x2 * s).astype(o_ref.dtype)
            o_ref[rows, lo + HALF:lo + HEAD_DIM] = (x1 * s + x2 * c).astype(o_ref.dtype)


def _inproj(xb, wt, ctab, stab, *, tm):
    m = xb.shape[0]
    n_pos = ctab.shape[1] // tm
    cast_w = wt.dtype != BF16
    assert not cast_w or m == tm
    wspec = pl.BlockSpec((WIDTH, D_MODEL), lambda j, i: (j, 0))
    ospec = pl.BlockSpec((tm, WIDTH), lambda j, i: (i, j))
    oshape = jax.ShapeDtypeStruct((m, 8 * WIDTH), BF16)

    def tab_map(j, i):
        kind = jnp.where(j == 0, 0, jnp.where(j == 1, 1, jnp.where(j == 5, 3, 2)))
        return (kind, i % n_pos, 0)

    return pl.pallas_call(
        _inproj_kernel,
        grid=(8, m // tm),
        in_specs=[
            pl.BlockSpec((tm, D_MODEL), lambda j, i: (i, 0)),
            wspec,
            pl.BlockSpec((1, tm, HALF), tab_map),
            pl.BlockSpec((1, tm, HALF), tab_map),
        ],
        out_specs=[ospec, wspec] if cast_w else ospec,
        out_shape=[oshape, jax.ShapeDtypeStruct((8 * WIDTH, D_MODEL), BF16)] if cast_w else oshape,
        compiler_params=_params("arbitrary", "arbitrary"),
        name="inproj_cast" if cast_w else "inproj",
    )(xb, wt, ctab, stab)


def _mgate_kernel(x_ref, w_ref, b_ref, xb_ref, g_ref, gt_ref):
    xb = x_ref[...].astype(BF16)
    xb_ref[...] = xb
    g = jnp.dot(xb, w_ref[...], preferred_element_type=F32) + b_ref[...]
    lane = lax.broadcasted_iota(jnp.int32, g.shape, 1)
    g = jnp.where((lane >= HEADS) & (lane < 2 * HEADS), _log_sigmoid(g), g)
    r = lax.broadcasted_iota(jnp.int32, (CHUNK, CHUNK), 0)
    s = lax.broadcasted_iota(jnp.int32, (CHUNK, CHUNK), 1)
    tril = (r >= s).astype(BF16)
    g1 = g.astype(BF16)
    res = g - g1.astype(F32)
    g2 = res.astype(BF16)
    g3 = (res - g2.astype(F32)).astype(BF16)
    parts = []
    for c in range(g.shape[0] // CHUNK):
        sl = slice(c * CHUNK, (c + 1) * CHUNK)
        parts.append(jnp.dot(tril, g1[sl], preferred_element_type=F32)
                     + jnp.dot(tril, g2[sl], preferred_element_type=F32)
                     + jnp.dot(tril, g3[sl], preferred_element_type=F32))
    cs = parts[0] if len(parts) == 1 else jnp.concatenate(parts, axis=0)
    b = pltpu.roll(cs, HEADS, 1)
    u = pltpu.roll(g, 3 * HEADS, 1) - pltpu.roll(cs, 2 * HEADS, 1)
    g = jnp.where((lane >= 2 * HEADS) & (lane < 3 * HEADS), b, g)
    g = jnp.where((lane >= 3 * HEADS) & (lane < 4 * HEADS), u, g)
    g_ref[...] = g
    gt_ref[...] = g.T[0:4 * HEADS, :]


def _mgate(x, w8, b8, *, tm):
    m = x.shape[0]
    return pl.pallas_call(
        _mgate_kernel,
        grid=(m // tm,),
        in_specs=[
            pl.BlockSpec((tm, D_MODEL), lambda i: (i, 0)),
            pl.BlockSpec((D_MODEL, LANES), lambda i: (0, 0)),
            pl.BlockSpec((1, LANES), lambda i: (0, 0)),
        ],
        out_specs=[
            pl.BlockSpec((tm, D_MODEL), lambda i: (i, 0)),
            pl.BlockSpec((tm, LANES), lambda i: (i, 0)),
            pl.BlockSpec((4 * HEADS, tm), lambda i: (0, i)),
        ],
        out_shape=[
            jax.ShapeDtypeStruct((m, D_MODEL), BF16),
            jax.ShapeDtypeStruct((m, LANES), F32),
            jax.ShapeDtypeStruct((4 * HEADS, m), F32),
        ],
        compiler_params=_params("arbitrary"),
        name="mlstm_gates",
    )(x, w8, b8)


def _mixer_chunk(rows, causal, qkv_ref, g_ref, gt_ref, intra_ref, dec_ref, kdec_ref, rgn_ref, mgn_ref,
                 mix_ref, s_ref, c_ref, n_ref, m_scr):
    def head(group, h):
        lo = group * WIDTH + h * HEAD_DIM
        return qkv_ref[rows, lo:lo + HEAD_DIM]

    gw = []
    for h in range(HEADS):
        b_col = g_ref[rows, 2 * HEADS + h:2 * HEADS + h + 1]
        u_row = gt_ref[3 * HEADS + h:3 * HEADS + h + 1, rows]
        m_prev = m_scr[h:h + 1, 0:1]
        dlog = jnp.where(causal, b_col + u_row, -jnp.inf)
        inter = b_col + m_prev
        mt = jnp.maximum(inter, jnp.max(dlog, axis=1, keepdims=True))
        m_new = mt[CHUNK - 1:CHUNK, :]
        b_last = b_col[CHUNK - 1:CHUNK, :]
        gw.append(dict(
            dw=jnp.exp(dlog - mt), iw=jnp.exp(inter - mt), emt=jnp.exp(-mt),
            sw=jnp.exp(b_last - m_new + u_row), sd=jnp.exp(b_last + m_prev - m_new)))
        m_scr[h:h + 1, :] = jnp.broadcast_to(m_new, (1, LANES))

    st = []
    for h in range(HEADS):
        for kind, (qg, state_ref) in enumerate(((0, s_ref), (MO_GROUP - 3, c_ref))):
            q = head(qg, h)
            k = head(qg + 1, h)
            old = state_ref[0, 0, h]
            kt = k.T
            st.append(dict(
                h=h, kind=kind, q=q, k=k, kt=kt, v=head(qg + 2, h), old=old,
                sc=jnp.dot(q, kt, preferred_element_type=F32),
                inter=jnp.dot(q, old.astype(BF16), preferred_element_type=F32)))

    for e in st:
        h = e["h"]
        if e["kind"] == 0:
            kd = (e["kt"].astype(F32) * kdec_ref[h:h + 1, :]).astype(BF16)
            s_ref[0, 0, h] = (e["old"] * RET_STATE_DECAY[h]
                              + jnp.dot(kd, e["v"], preferred_element_type=F32))
        else:
            w = gw[h]
            ksw = (e["kt"].astype(F32) * w["sw"]).astype(BF16)
            c_ref[0, 0, h] = w["sd"] * e["old"] + jnp.dot(ksw, e["v"], preferred_element_type=F32)
            n_old = n_ref[0, 0, h:h + 1, :]
            e["qn"] = jnp.sum(e["q"].astype(F32) * n_old, axis=1, keepdims=True)
            sw8 = jnp.broadcast_to(w["sw"], (8, CHUNK)).astype(BF16)
            n_ref[0, 0, h:h + 1, :] = (w["sd"] * n_old
                                       + jnp.dot(sw8, e["k"], preferred_element_type=F32)[0:1, :])

    for e in st:
        h = e["h"]
        e["sc"] = e["sc"] * (intra_ref[h] if e["kind"] == 0 else gw[h]["dw"])
        e["pv"] = jnp.dot(e["sc"].astype(BF16), e["v"], preferred_element_type=F32)

    def tail():
        dens = [jnp.sum(e["sc"], axis=1, keepdims=True) if e["kind"] else None for e in st]
        outs = []
        for e, den in zip(st, dens):
            h = e["h"]
            if e["kind"] == 0:
                outs.append(e["pv"] + e["inter"] * dec_ref[h])
            else:
                w = gw[h]
                num = e["pv"] + w["iw"] * e["inter"]
                den = den + w["iw"] * e["qn"]
                outs.append(num * (1.0 / jnp.maximum(jnp.abs(den), w["emt"])))
        mus = [jnp.mean(o, axis=-1, keepdims=True) for o in outs]
        cen = [o - mu for o, mu in zip(outs, mus)]
        var = [jnp.mean(jnp.square(oc), axis=-1, keepdims=True) for oc in cen]
        for e, oc, v in zip(st, cen, var):
            h = e["h"]
            lo = h * HEAD_DIM
            y = oc * lax.rsqrt(v + GN_EPS)
            if e["kind"] == 0:
                y = y * rgn_ref[:, lo:lo + HEAD_DIM] * head(RG_GROUP, h).astype(F32)
                mix_ref[rows, lo:lo + HEAD_DIM] = y.astype(mix_ref.dtype)
            else:
                y = y * mgn_ref[:, lo:lo + HEAD_DIM] * head(MO_GROUP, h).astype(F32)
                mix_ref[rows, WIDTH + lo:WIDTH + lo + HEAD_DIM] = y.astype(mix_ref.dtype)

    return tail


def _prompt_mixer_kernel(qkv_ref, g_ref, gt_ref, intra_ref, dec_ref, kdec_ref, rgn_ref, mgn_ref,
                         mix_ref, s_ref, c_ref, n_ref, m_ref, m_scr):
    c = pl.program_id(1)

    @pl.when(c == 0)
    def _():
        s_ref[...] = jnp.zeros_like(s_ref)
        c_ref[...] = jnp.zeros_like(c_ref)
        n_ref[...] = jnp.zeros_like(n_ref)
        m_scr[...] = jnp.zeros_like(m_scr)

    ti = lax.broadcasted_iota(jnp.int32, (CHUNK, CHUNK), 0)
    si = lax.broadcasted_iota(jnp.int32, (CHUNK, CHUNK), 1)
    causal = ti >= si

    pending = None
    for ci in range(MIXER_CHUNKS_PER_STEP):
        tail = _mixer_chunk(slice(ci * CHUNK, (ci + 1) * CHUNK), causal, qkv_ref, g_ref, gt_ref,
                            intra_ref, dec_ref, kdec_ref, rgn_ref, mgn_ref, mix_ref, s_ref, c_ref,
                            n_ref, m_scr)
        if pending is not None:
            pending()
        pending = tail
    pending()

    @pl.when(c == pl.num_programs(1) - 1)
    def _():
        m_ref[0] = m_scr[...]


def _prompt_mixer(qkv, g_tok, g_t, intra, qdec, kdec, rgn, mgn, *, batch, seq):
    rows = MIXER_CHUNKS_PER_STEP * CHUNK
    nc = seq // rows
    row = lambda b, c: (b * nc + c, 0)
    state = pl.BlockSpec((1, 1, HEADS, HEAD_DIM, HEAD_DIM), lambda b, c: (0, b, 0, 0, 0))
    return pl.pallas_call(
        _prompt_mixer_kernel,
        grid=(batch, nc),
        in_specs=[
            pl.BlockSpec((rows, 8 * WIDTH), row),
            pl.BlockSpec((rows, LANES), row),
            pl.BlockSpec((4 * HEADS, rows), lambda b, c: (0, b * nc + c)),
            pl.BlockSpec((HEADS, CHUNK, CHUNK), lambda b, c: (0, 0, 0)),
            pl.BlockSpec((HEADS, CHUNK, HEAD_DIM), lambda b, c: (0, 0, 0)),
            pl.BlockSpec((HEADS, CHUNK), lambda b, c: (0, 0)),
            pl.BlockSpec((1, WIDTH), lambda b, c: (0, 0)),
            pl.BlockSpec((1, WIDTH), lambda b, c: (0, 0)),
        ],
        out_specs=[
            pl.BlockSpec((rows, 2 * WIDTH), row),
            state,
            state,
            pl.BlockSpec((1, 1, HEADS, HEAD_DIM), lambda b, c: (0, b, 0, 0)),
            pl.BlockSpec((1, 8, LANES), lambda b, c: (b, 0, 0)),
        ],
        out_shape=[
            jax.ShapeDtypeStruct((batch * seq, 2 * WIDTH), BF16),
            jax.ShapeDtypeStruct((1, batch, HEADS, HEAD_DIM, HEAD_DIM), F32),
            jax.ShapeDtypeStruct((1, batch, HEADS, HEAD_DIM, HEAD_DIM), F32),
            jax.ShapeDtypeStruct((1, batch, HEADS, HEAD_DIM), F32),
            jax.ShapeDtypeStruct((batch, 8, LANES), F32),
        ],
        scratch_shapes=[pltpu.VMEM((8, LANES), F32)],
        compiler_params=_params("arbitrary", "arbitrary"),
        name="prompt_mixer",
    )(qkv, g_tok, g_t, intra, qdec, kdec, rgn, mgn)


def _sample_gate_kernel(g_ref, m0_ref, mt_ref, dw_ref, iw_ref, emt_ref):
    ig = g_ref[...]
    lf = pltpu.roll(ig, LANES - HEADS, 1)
    inter = lf + m0_ref[...]
    mt = jnp.maximum(inter, ig)
    mt_ref[...] = mt
    dw_ref[...] = jnp.exp(ig - mt)
    iw_ref[...] = jnp.exp(inter - mt)
    emt_ref[...] = jnp.exp(-mt)


def _sample_gates(g_tok, m0_pad):
    m = g_tok.shape[0]
    spec = pl.BlockSpec((m, LANES), lambda: (0, 0))
    return pl.pallas_call(
        _sample_gate_kernel,
        in_specs=[spec, spec],
        out_specs=[spec] * 4,
        out_shape=[jax.ShapeDtypeStruct((m, LANES), F32)] * 4,
        name="sample_gates",
    )(g_tok, m0_pad)


def _sample_mixer_kernel(scal_ref, rows_ref, s0_ref, c0_ref, n0_ref,
                         rgn_ref, mgn_ref, mix_ref, s_ref, c_ref, n_ref):
    first = lax.broadcasted_iota(jnp.int32, (4 * HEADS, HEAD_DIM), 0) == 0
    for t in range(SAMPLE_TOKENS_PER_STEP):
        b = pl.program_id(0) * SAMPLE_TOKENS_PER_STEP + t

        def row(group, h):
            lo = group * WIDTH + h * HEAD_DIM
            return rows_ref[t, :, lo:lo + HEAD_DIM]

        def outer(k, v):
            kp = jnp.where(first, jnp.broadcast_to(k, first.shape), 0.0).astype(BF16)
            vp = jnp.where(first, jnp.broadcast_to(v, first.shape), 0.0).astype(BF16)
            return lax.dot_general(kp, vp, _TN, preferred_element_type=F32)

        def apply(q_row, state):
            q8 = jnp.broadcast_to(q_row, (8, HEAD_DIM)).astype(BF16)
            return jnp.dot(q8, state.astype(BF16), preferred_element_type=F32)[0:1, :]

        dw = [scal_ref[b, HEADS + h] for h in range(HEADS)]
        iw = [scal_ref[b, 2 * HEADS + h] for h in range(HEADS)]
        emt = [scal_ref[b, 3 * HEADS + h] for h in range(HEADS)]
        kv_s = [outer(row(1, h), row(2, h)) for h in range(HEADS)]
        kv_c = [outer(row(5, h), dw[h] * row(6, h)) for h in range(HEADS)]
        s_new, c_new = [], []
        for h in range(HEADS):
            s_new.append(RET_G[h] * s0_ref[0, t, h] + kv_s[h])
            s_ref[0, t, h] = s_new[h]
            c_new.append(iw[h] * c0_ref[0, t, h] + kv_c[h])
            c_ref[0, t, h] = c_new[h]
        o = [apply(row(0, h), s_new[h]) for h in range(HEADS)]
        num = [apply(row(4, h), c_new[h]) for h in range(HEADS)]
        for h in range(HEADS):
            lo = h * HEAD_DIM
            mix_ref[t, :, lo:lo + HEAD_DIM] = (_head_norm(o[h]) * rgn_ref[:, lo:lo + HEAD_DIM]
                                               * row(RG_GROUP, h))
            n_new = iw[h] * n0_ref[0, t, h:h + 1, :] + dw[h] * row(5, h)
            n_ref[0, t, h:h + 1, :] = n_new
            den = jnp.sum(row(4, h) * n_new, axis=1, keepdims=True)
            hid = num[h] * (1.0 / jnp.maximum(jnp.abs(den), emt[h]))
            mix_ref[t, :, WIDTH + lo:WIDTH + lo + HEAD_DIM] = (
                _head_norm(hid) * mgn_ref[:, lo:lo + HEAD_DIM] * row(MO_GROUP, h))


def _sample_mixer(scal, rows, s0, c0, n0, rgn, mgn):
    nb = rows.shape[0]
    tb = SAMPLE_TOKENS_PER_STEP
    state = pl.BlockSpec((1, tb, HEADS, HEAD_DIM, HEAD_DIM), lambda b: (0, b, 0, 0, 0))
    nspec = pl.BlockSpec((1, tb, HEADS, HEAD_DIM), lambda b: (0, b, 0, 0))
    return pl.pallas_call(
        _sample_mixer_kernel,
        grid=(nb // tb,),
        in_specs=[
            pl.BlockSpec(memory_space=pltpu.SMEM),
            pl.BlockSpec((tb, 1, 8 * WIDTH), lambda b: (b, 0, 0)),
            state, state, nspec,
            pl.BlockSpec((1, WIDTH), lambda b: (0, 0)),
            pl.BlockSpec((1, WIDTH), lambda b: (0, 0)),
        ],
        out_specs=[pl.BlockSpec((tb, 1, 2 * WIDTH), lambda b: (b, 0, 0)), state, state, nspec],
        out_shape=[
            jax.ShapeDtypeStruct((nb, 1, 2 * WIDTH), F32),
            jax.ShapeDtypeStruct(s0.shape, F32),
            jax.ShapeDtypeStruct(c0.shape, F32),
            jax.ShapeDtypeStruct(n0.shape, F32),
        ],
        compiler_params=_params("arbitrary"),
        name="sample_mixer",
    )(scal, rows, s0, c0, n0, rgn, mgn)


def _outproj_kernel(a_ref, w_ref, x_ref, g_ref, b_ref, x1_ref, x1b_ref, *wb_out):
    if wb_out:
        wb_out[0][...] = w_ref[...].astype(BF16)
        w_ref = wb_out[0]
    tm = a_ref.shape[0]
    for rows in _row_splits(tm):
        mix = jnp.dot(a_ref[rows, :].astype(BF16), w_ref[...], preferred_element_type=F32)
        x1 = _layer_norm(ALPHA * x_ref[rows, :] + mix, g_ref[...], b_ref[...])
        x1_ref[rows, :] = x1
        x1b_ref[rows, :] = x1.astype(BF16)


def _outproj(a, w, x, g, b, *, tm):
    m = a.shape[0]
    cast_w = w.dtype != BF16
    assert not cast_w or m == tm
    row = pl.BlockSpec((tm, D_MODEL), lambda i: (i, 0))
    vec = pl.BlockSpec((1, D_MODEL), lambda i: (0, 0))
    out_specs = [row, row]
    out_shape = [jax.ShapeDtypeStruct((m, D_MODEL), F32), jax.ShapeDtypeStruct((m, D_MODEL), BF16)]
    if cast_w:
        out_specs.append(pl.BlockSpec((D_MODEL, D_MODEL), lambda i: (0, 0)))
        out_shape.append(jax.ShapeDtypeStruct((D_MODEL, D_MODEL), BF16))
    return pl.pallas_call(
        _outproj_kernel,
        grid=(m // tm,),
        in_specs=[row, pl.BlockSpec((D_MODEL, D_MODEL), lambda i: (0, 0), pipeline_mode=pl.Buffered(1)),
                  row, vec, vec],
        out_specs=out_specs,
        out_shape=out_shape,
        compiler_params=_params("arbitrary"),
        name="outproj_ln1_cast" if cast_w else "outproj_ln1",
    )(a, w, x, g, b)


def _ffn_kernel(x1b_ref, w1_ref, w2_ref, o_ref, *wb_out):
    @pl.when(pl.program_id(1) == 0)
    def _():
        o_ref[...] = jnp.zeros_like(o_ref)

    w1 = w1_ref[...]
    w2 = w2_ref[...]
    if wb_out:
        w1 = w1.astype(BF16)
        w2 = w2.astype(BF16)
        wb_out[0][...] = w1
        wb_out[1][...] = w2
    hid = jnp.dot(x1b_ref[...], w1, preferred_element_type=F32)
    hid = jnp.square(jnp.maximum(hid, 0.0)).astype(BF16)
    o_ref[...] += jnp.dot(hid, w2, preferred_element_type=F32)


def _ffn(x1b, w1, w2, *, tm, tf):
    m = x1b.shape[0]
    cast_w = w1.dtype != BF16
    assert not cast_w or m == tm
    row = pl.BlockSpec((tm, D_MODEL), lambda i, f: (i, 0))
    w1spec = pl.BlockSpec((D_MODEL, tf), lambda i, f: (0, f))
    w2spec = pl.BlockSpec((tf, D_MODEL), lambda i, f: (f, 0))
    oshape = jax.ShapeDtypeStruct((m, D_MODEL), F32)
    return pl.pallas_call(
        _ffn_kernel,
        grid=(m // tm, D_FF // tf),
        in_specs=[row, w1spec, w2spec],
        out_specs=[row, w1spec, w2spec] if cast_w else row,
        out_shape=[oshape, jax.ShapeDtypeStruct(w1.shape, BF16),
                   jax.ShapeDtypeStruct(w2.shape, BF16)] if cast_w else oshape,
        compiler_params=_params("arbitrary", "arbitrary"),
        name="ffn_cast" if cast_w else "ffn",
    )(x1b, w1, w2)


def _pe_ln2_kernel(x1_ref, x1b_ref, p_ref, ff_ref, wpe_ref, wg_ref, g_ref, b_ref, o_ref, *wb_out):
    if wb_out:
        wb_out[0][...] = wpe_ref[...].astype(BF16)
        wb_out[1][...] = wg_ref[...].astype(BF16)
        wpe_ref, wg_ref = wb_out
    for rows in _row_splits(x1_ref.shape[0]):
        gate = _sigmoid(jnp.dot(x1b_ref[rows, :], wg_ref[...], preferred_element_type=F32))
        pe = jnp.dot(p_ref[rows, :].astype(BF16), wpe_ref[...], preferred_element_type=F32)
        y = ALPHA * x1_ref[rows, :] + ff_ref[rows, :] + pe * gate
        o_ref[rows, :] = _layer_norm(y, g_ref[...], b_ref[...])


def _pe_ln2(x1, x1b, p, ff, wpe, wg, g, b, *, tm):
    m = x1.shape[0]
    cast_w = wg.dtype != BF16
    assert not cast_w or m == tm
    row = pl.BlockSpec((tm, D_MODEL), lambda i: (i, 0))
    vec = pl.BlockSpec((1, D_MODEL), lambda i: (0, 0))
    oshape = jax.ShapeDtypeStruct((m, D_MODEL), F32)
    wb_specs = [pl.BlockSpec((PLE_DIM, D_MODEL), lambda i: (0, 0)),
                pl.BlockSpec((D_MODEL, D_MODEL), lambda i: (0, 0))]
    wb_shapes = [jax.ShapeDtypeStruct(wpe.shape, BF16), jax.ShapeDtypeStruct(wg.shape, BF16)]
    return pl.pallas_call(
        _pe_ln2_kernel,
        grid=(m // tm,),
        in_specs=[row, row, pl.BlockSpec((tm, PLE_DIM), lambda i: (i, 0)), row,
                  pl.BlockSpec((PLE_DIM, D_MODEL), lambda i: (0, 0), pipeline_mode=pl.Buffered(1)),
                  pl.BlockSpec((D_MODEL, D_MODEL), lambda i: (0, 0), pipeline_mode=pl.Buffered(1)),
                  vec, vec],
        out_specs=[row] + wb_specs if cast_w else row,
        out_shape=[oshape] + wb_shapes if cast_w else oshape,
        compiler_params=_params("arbitrary"),
        name="pe_ln2_cast" if cast_w else "pe_ln2",
    )(x1, x1b, p, ff, wpe, wg, g, b)


def _rope_tables(pos):
    inv = ROPE_BASE ** (-jnp.arange(HALF, dtype=F32) * 2.0 / HEAD_DIM)
    ang = pos[:, None] * inv[None, :]
    cos, sin = jnp.cos(ang), jnp.sin(ang)
    one, zero = jnp.ones_like(cos), jnp.zeros_like(cos)
    return (jnp.stack([cos, cos * QK_SCALE, one, one * QK_SCALE]),
            jnp.stack([sin, sin * QK_SCALE, zero, zero]))


def _decay_tables():
    lg = jnp.log(1.0 - 2.0 ** (-5.0 - jnp.arange(HEADS, dtype=F32)))
    t = jnp.arange(CHUNK, dtype=F32)
    causal = t[:, None] >= t[None, :]
    intra = jnp.exp(jnp.where(causal, (t[:, None] - t[None, :]) * lg[:, None, None], -jnp.inf))
    q_decay = jnp.exp(lg[:, None] * (t + 1.0))
    k_decay = jnp.exp(lg[:, None] * (CHUNK - 1.0 - t))
    return intra, jnp.broadcast_to(q_decay[:, :, None], (HEADS, CHUNK, HEAD_DIM)), k_decay


def _tail(x, mixed, p, w, *, tm, tm_ffn, tf):
    casts = {}
    res = _outproj(mixed, w["out"], x, w["ln1_g"], w["ln1_b"], tm=tm)
    x1, x1b = res[0], res[1]
    if len(res) > 2:
        casts["out"] = res[2]
    res = _ffn(x1b, w["ff1"], w["ff2"], tm=tm_ffn, tf=tf)
    if isinstance(res, (list, tuple)):
        ff, casts["ff1"], casts["ff2"] = res
    else:
        ff = res
    res = _pe_ln2(x1, x1b, p, ff, w["pe"], w["pe_gate"], w["ln2_g"], w["ln2_b"], tm=tm)
    if isinstance(res, (list, tuple)):
        y, casts["pe"], casts["pe_gate"] = res
    else:
        y = res
    return y, casts


def kernel(x_prompt, x_sample, state_ret, state_mlstm_C, state_mlstm_n, state_mlstm_m, p_prompt, p_sample, w_in, b_gate, ret_gn_w, mlstm_gn_w, w_out, ln1_g, ln1_b, w_ff1, w_ff2, w_pe, w_pe_gate, ln2_g, ln2_b):
    batch, seq, _ = x_prompt.shape
    nb = x_sample.shape[0]

    w_in_t = w_in[0].T
    w8 = jnp.pad(w_in[0][:, 8 * WIDTH:], ((0, 0), (0, LANES - 2 * HEADS))).astype(BF16)
    b8 = jnp.pad(b_gate[0].astype(F32), (0, LANES - 2 * HEADS)).reshape(1, LANES)
    w = {
        "out": w_out[0], "ff1": w_ff1[0], "ff2": w_ff2[0], "pe": w_pe[0], "pe_gate": w_pe_gate[0],
        "ln1_g": ln1_g[0].reshape(1, D_MODEL), "ln1_b": ln1_b[0].reshape(1, D_MODEL),
        "ln2_g": ln2_g[0].reshape(1, D_MODEL), "ln2_b": ln2_b[0].reshape(1, D_MODEL),
    }
    rgn = ret_gn_w[0].reshape(1, WIDTH)
    mgn = mlstm_gn_w[0].reshape(1, WIDTH)
    intra, qdec, kdec = _decay_tables()

    xs = x_sample.reshape(nb, D_MODEL)
    xsb, gs_tok, _ = _mgate(xs, w8, b8, tm=nb)
    ctab, stab = _rope_tables(jnp.full((nb,), PAST_LEN, dtype=F32))
    proj_s, w_in_b = _inproj(xsb, w_in_t, ctab, stab, tm=nb)
    proj_s = proj_s.astype(F32)
    m0 = jnp.pad(state_mlstm_m[0], ((0, 0), (0, LANES - HEADS)))
    mt, dw, iw, emt = _sample_gates(gs_tok, m0)
    scal = jnp.concatenate([mt[:, :HEADS], dw[:, :HEADS], iw[:, :HEADS], emt[:, :HEADS]], axis=1)
    mixed_s, s_s, c_s, n_s = _sample_mixer(
        scal, proj_s.reshape(nb, 1, 8 * WIDTH),
        state_ret, state_mlstm_C, state_mlstm_n, rgn, mgn)
    m_s = mt[:, :HEADS].reshape(1, nb, HEADS)
    y_s, w_bf16 = _tail(xs, mixed_s.reshape(nb, 2 * WIDTH), p_sample[0].reshape(nb, PLE_DIM), w,
                        tm=nb, tm_ffn=nb, tf=512)

    w = {**w, **w_bf16}
    xp = x_prompt.reshape(batch * seq, D_MODEL)
    xpb, g_tok, g_t = _mgate(xp, w8, b8, tm=1024)
    ctab, stab = _rope_tables(jnp.arange(seq, dtype=F32))
    proj_p = _inproj(xpb, w_in_b, ctab, stab, tm=2048)
    mixed_p, s_p, c_p, n_p, m_rows = _prompt_mixer(proj_p, g_tok, g_t, intra, qdec, kdec, rgn, mgn,
                                                   batch=batch, seq=seq)
    m_p = m_rows[:, :HEADS, 0].reshape(1, batch, HEADS)
    y_p, _ = _tail(xp, mixed_p, p_prompt[0].reshape(batch * seq, PLE_DIM), w,
                   tm=512, tm_ffn=1024, tf=1024)

    return (y_p.reshape(batch, seq, D_MODEL), y_s.reshape(nb, 1, D_MODEL),
            s_p, c_p, n_p, m_p, s_s, c_s, n_s, m_s)
```

```python
import jax
import jax.numpy as jnp
import numpy as np
from jax import lax
from jax.experimental import pallas as pl
from jax.experimental.pallas import tpu as pltpu

F32 = jnp.float32
BF16 = jnp.bfloat16

D_MODEL = 2048
HEADS = 4
HEAD_DIM = 256
HALF = HEAD_DIM // 2
WIDTH = HEADS * HEAD_DIM
D_FF = 4 * D_MODEL
PLE_DIM = 256
CHUNK = 128
PAST_LEN = 16384
ROPE_BASE = 10000.0
LN_EPS = 1e-5
GN_EPS = 1e-6
DEPTH = 1
ALPHA = (2 * DEPTH) ** 0.25
QK_SCALE = HEAD_DIM ** -0.5
LANES = 128
SUBLANES = 8
MXU_ROWS = 256
VMEM_LIMIT = 56 * 1024 * 1024
SAMPLE_TOKENS_PER_STEP = 4
MIXER_CHUNKS_PER_STEP = 2
GATE_TM = 1024
INPROJ_TM = 2048
TAIL_TM = 512
FFN_TM = 1024
FFN_TF = 1024
CAST_FFN_TF = 512

RET_G = tuple(1.0 - 2.0 ** (-5.0 - h) for h in range(HEADS))
RET_LOG_G = tuple(float(np.log(np.float32(g))) for g in RET_G)
RET_STATE_DECAY = tuple(float(np.exp(np.float32(lg) * np.float32(CHUNK))) for lg in RET_LOG_G)

_NT = (((1,), (1,)), ((), ()))
_TN = (((0,), (0,)), ((), ()))


def _params(*sem):
    return pltpu.CompilerParams(dimension_semantics=sem, vmem_limit_bytes=VMEM_LIMIT)


def _sigmoid(x):
    return 0.5 * jnp.tanh(0.5 * x) + 0.5


def _log_sigmoid(x):
    return jnp.minimum(x, 0.0) - jnp.log(1.0 + jnp.exp(-jnp.abs(x)))


def _layer_norm(y, g, b):
    mu = jnp.mean(y, axis=-1, keepdims=True)
    yc = y - mu
    var = jnp.mean(jnp.square(yc), axis=-1, keepdims=True)
    return yc * lax.rsqrt(var + LN_EPS) * g + b


def _row_splits(tm):
    if tm < 2 * MXU_ROWS:
        return [slice(0, tm)]
    return [slice(0, tm // 2), slice(tm // 2, tm)]


def _head_norm(o):
    mu = jnp.mean(o, axis=-1, keepdims=True)
    oc = o - mu
    var = jnp.mean(jnp.square(oc), axis=-1, keepdims=True)
    return oc * lax.rsqrt(var + GN_EPS)


RG_GROUP, MO_GROUP = 3, 7


def _inproj_kernel(x_ref, wt_ref, c_ref, s_ref, o_ref, *wb_out):
    j = pl.program_id(0)
    if wb_out:
        wb_ref, = wb_out
        wb_ref[...] = wt_ref[...].astype(BF16)
    else:
        wb_ref = wt_ref

    is_gate = (j == RG_GROUP) | (j == MO_GROUP)
    pieces = [(h, rows) for h in range(HEADS) for rows in _row_splits(x_ref.shape[0])]

    def piece_acc(h, rows):
        return lax.dot_general(x_ref[rows, :], wb_ref[h * HEAD_DIM:(h + 1) * HEAD_DIM, :], _NT,
                               preferred_element_type=F32)

    @pl.when(is_gate)
    def _():
        for h, rows in pieces:
            acc = piece_acc(h, rows)
            sig = _sigmoid(acc)
            o_ref[rows, h * HEAD_DIM:(h + 1) * HEAD_DIM] = jnp.where(
                j == RG_GROUP, acc * sig, sig).astype(o_ref.dtype)

    @pl.when(jnp.logical_not(is_gate))
    def _():
        rotary = j <= 1
        scale = jnp.where((j == 1) | (j == MO_GROUP - 2), QK_SCALE, 1.0).astype(F32)
        for h, rows in pieces:
            lo = h * HEAD_DIM
            c = jnp.where(rotary, c_ref[rows, :], 1.0) * scale
            s = jnp.where(rotary, s_ref[rows, :], 0.0) * scale
            acc = piece_acc(h, rows)
            x1 = acc[:, :HALF]
            x2 = acc[:, HALF:]
            o_ref[rows, lo:lo + HALF] = (x1 * c - x2 * s).astype(o_ref.dtype)
            o_ref[rows, lo + HALF:lo + HEAD_DIM] = (x1 * s + x2 * c).astype(o_ref.dtype)


def _inproj(xb, wt, cos, sin, *, tm):
    m = xb.shape[0]
    n_pos = cos.shape[0] // tm
    cast_w = wt.dtype != BF16
    assert not cast_w or m == tm
    wspec = pl.BlockSpec((WIDTH, D_MODEL), lambda j, i: (j, 0))
    ospec = pl.BlockSpec((tm, WIDTH), lambda j, i: (i, j))
    oshape = jax.ShapeDtypeStruct((m, 8 * WIDTH), BF16)
    tspec = pl.BlockSpec((tm, HALF), lambda j, i: (i % n_pos, 0))

    return pl.pallas_call(
        _inproj_kernel,
        grid=(8, m // tm),
        in_specs=[pl.BlockSpec((tm, D_MODEL), lambda j, i: (i, 0)), wspec, tspec, tspec],
        out_specs=[ospec, wspec] if cast_w else ospec,
        out_shape=[oshape, jax.ShapeDtypeStruct((8 * WIDTH, D_MODEL), BF16)] if cast_w else oshape,
        compiler_params=_params("arbitrary", "arbitrary"),
        name="inproj_cast" if cast_w else "inproj",
    )(xb, wt, cos, sin)


def _mgate_kernel(x_ref, w_ref, b_ref, xb_ref, g_ref, gt_ref):
    xb = x_ref[...].astype(BF16)
    xb_ref[...] = xb
    g = jnp.dot(xb, w_ref[...], preferred_element_type=F32) + b_ref[...]
    lane = lax.broadcasted_iota(jnp.int32, g.shape, 1)
    g = jnp.where((lane >= HEADS) & (lane < 2 * HEADS), _log_sigmoid(g), g)
    r = lax.broadcasted_iota(jnp.int32, (CHUNK, CHUNK), 0)
    s = lax.broadcasted_iota(jnp.int32, (CHUNK, CHUNK), 1)
    tril = (r >= s).astype(BF16)
    g1 = g.astype(BF16)
    res = g - g1.astype(F32)
    g2 = res.astype(BF16)
    g3 = (res - g2.astype(F32)).astype(BF16)
    parts = []
    for c in range(g.shape[0] // CHUNK):
        sl = slice(c * CHUNK, (c + 1) * CHUNK)
        parts.append(jnp.dot(tril, g1[sl], preferred_element_type=F32)
                     + jnp.dot(tril, g2[sl], preferred_element_type=F32)
                     + jnp.dot(tril, g3[sl], preferred_element_type=F32))
    cs = parts[0] if len(parts) == 1 else jnp.concatenate(parts, axis=0)
    b = pltpu.roll(cs, HEADS, 1)
    u = pltpu.roll(g, 3 * HEADS, 1) - pltpu.roll(cs, 2 * HEADS, 1)
    g = jnp.where((lane >= 2 * HEADS) & (lane < 3 * HEADS), b, g)
    g = jnp.where((lane >= 3 * HEADS) & (lane < 4 * HEADS), u, g)
    g_ref[...] = g
    gt_ref[...] = g.T[0:4 * HEADS, :]


def _mgate(x, w8, b8, *, tm):
    m = x.shape[0]
    return pl.pallas_call(
        _mgate_kernel,
        grid=(m // tm,),
        in_specs=[
            pl.BlockSpec((tm, D_MODEL), lambda i: (i, 0)),
            pl.BlockSpec((D_MODEL, LANES), lambda i: (0, 0)),
            pl.BlockSpec((1, LANES), lambda i: (0, 0)),
        ],
        out_specs=[
            pl.BlockSpec((tm, D_MODEL), lambda i: (i, 0)),
            pl.BlockSpec((tm, LANES), lambda i: (i, 0)),
            pl.BlockSpec((4 * HEADS, tm), lambda i: (0, i)),
        ],
        out_shape=[
            jax.ShapeDtypeStruct((m, D_MODEL), BF16),
            jax.ShapeDtypeStruct((m, LANES), F32),
            jax.ShapeDtypeStruct((4 * HEADS, m), F32),
        ],
        compiler_params=_params("arbitrary"),
        name="mlstm_gates",
    )(x, w8, b8)


def _mixer_chunk(rows, causal, qkv_ref, g_ref, gt_ref, intra_ref, dec_ref, kdec_ref, rgn_ref, mgn_ref,
                 mix_ref, s_ref, c_ref, n_ref, m_scr):
    def head(group, h):
        lo = group * WIDTH + h * HEAD_DIM
        return qkv_ref[rows, lo:lo + HEAD_DIM]

    gw = []
    for h in range(HEADS):
        b_col = g_ref[rows, 2 * HEADS + h:2 * HEADS + h + 1]
        u_row = gt_ref[3 * HEADS + h:3 * HEADS + h + 1, rows]
        m_prev = m_scr[h:h + 1, 0:1]
        dlog = jnp.where(causal, b_col + u_row, -jnp.inf)
        inter = b_col + m_prev
        mt = jnp.maximum(inter, jnp.max(dlog, axis=1, keepdims=True))
        m_new = mt[CHUNK - 1:CHUNK, :]
        b_last = b_col[CHUNK - 1:CHUNK, :]
        gw.append(dict(
            dw=jnp.exp(dlog - mt), iw=jnp.exp(inter - mt), emt=jnp.exp(-mt),
            sw=jnp.exp(b_last - m_new + u_row), sd=jnp.exp(b_last + m_prev - m_new)))
        m_scr[h:h + 1, :] = jnp.broadcast_to(m_new, (1, LANES))

    st = []
    for h in range(HEADS):
        for kind, (qg, state_ref) in enumerate(((0, s_ref), (MO_GROUP - 3, c_ref))):
            q = head(qg, h)
            k = head(qg + 1, h)
            old = state_ref[0, 0, h]
            kt = k.T
            st.append(dict(
                h=h, kind=kind, q=q, k=k, kt=kt, v=head(qg + 2, h), old=old,
                sc=jnp.dot(q, kt, preferred_element_type=F32),
                inter=jnp.dot(q, old.astype(BF16), preferred_element_type=F32)))

    for e in st:
        h = e["h"]
        if e["kind"] == 0:
            kd = (e["kt"].astype(F32) * kdec_ref[h:h + 1, :]).astype(BF16)
            s_ref[0, 0, h] = (e["old"] * RET_STATE_DECAY[h]
                              + jnp.dot(kd, e["v"], preferred_element_type=F32))
        else:
            w = gw[h]
            ksw = (e["kt"].astype(F32) * w["sw"]).astype(BF16)
            c_ref[0, 0, h] = w["sd"] * e["old"] + jnp.dot(ksw, e["v"], preferred_element_type=F32)
            n_old = n_ref[0, 0, h:h + 1, :]
            e["qn"] = jnp.sum(e["q"].astype(F32) * n_old, axis=1, keepdims=True)
            sw8 = jnp.broadcast_to(w["sw"], (SUBLANES, CHUNK)).astype(BF16)
            n_ref[0, 0, h:h + 1, :] = (w["sd"] * n_old
                                       + jnp.dot(sw8, e["k"], preferred_element_type=F32)[0:1, :])

    for e in st:
        h = e["h"]
        e["sc"] = e["sc"] * (intra_ref[h] if e["kind"] == 0 else gw[h]["dw"])
        e["pv"] = jnp.dot(e["sc"].astype(BF16), e["v"], preferred_element_type=F32)

    for e in st:
        e["rows"] = rows
        if e["kind"]:
            e["iw"] = gw[e["h"]]["iw"]
            e["emt"] = gw[e["h"]]["emt"]
    return st


def _mixer_tail(streams, qkv_ref, dec_ref, rgn_ref, mgn_ref, mix_ref):
    dens = [jnp.sum(e["sc"], axis=1, keepdims=True) if e["kind"] else None for e in streams]
    outs = []
    for e, den in zip(streams, dens):
        if e["kind"] == 0:
            outs.append(e["pv"] + e["inter"] * dec_ref[e["h"]])
        else:
            num = e["pv"] + e["iw"] * e["inter"]
            den = den + e["iw"] * e["qn"]
            outs.append(num * (1.0 / jnp.maximum(jnp.abs(den), e["emt"])))
    mus = [jnp.mean(o, axis=-1, keepdims=True) for o in outs]
    cen = [o - mu for o, mu in zip(outs, mus)]
    var = [jnp.mean(jnp.square(oc), axis=-1, keepdims=True) for oc in cen]
    for e, oc, v in zip(streams, cen, var):
        rows = e["rows"]
        lo = e["h"] * HEAD_DIM
        gate_lo = (MO_GROUP if e["kind"] else RG_GROUP) * WIDTH + lo
        gain_ref = mgn_ref if e["kind"] else rgn_ref
        out_lo = e["kind"] * WIDTH + lo
        y = (oc * lax.rsqrt(v + GN_EPS) * gain_ref[:, lo:lo + HEAD_DIM]
             * qkv_ref[rows, gate_lo:gate_lo + HEAD_DIM].astype(F32))
        mix_ref[rows, out_lo:out_lo + HEAD_DIM] = y.astype(mix_ref.dtype)


def _prompt_mixer_kernel(qkv_ref, g_ref, gt_ref, intra_ref, dec_ref, kdec_ref, rgn_ref, mgn_ref,
                         mix_ref, s_ref, c_ref, n_ref, m_ref, m_scr):
    c = pl.program_id(1)

    @pl.when(c == 0)
    def _():
        s_ref[...] = jnp.zeros_like(s_ref)
        c_ref[...] = jnp.zeros_like(c_ref)
        n_ref[...] = jnp.zeros_like(n_ref)
        m_scr[...] = jnp.zeros_like(m_scr)

    ti = lax.broadcasted_iota(jnp.int32, (CHUNK, CHUNK), 0)
    si = lax.broadcasted_iota(jnp.int32, (CHUNK, CHUNK), 1)
    causal = ti >= si

    streams = []
    for ci in range(MIXER_CHUNKS_PER_STEP):
        streams += _mixer_chunk(slice(ci * CHUNK, (ci + 1) * CHUNK), causal, qkv_ref, g_ref, gt_ref,
                                intra_ref, dec_ref, kdec_ref, rgn_ref, mgn_ref, mix_ref, s_ref, c_ref,
                                n_ref, m_scr)
    _mixer_tail(streams, qkv_ref, dec_ref, rgn_ref, mgn_ref, mix_ref)

    @pl.when(c == pl.num_programs(1) - 1)
    def _():
        m_ref[0] = m_scr[...]


def _prompt_mixer(qkv, g_tok, g_t, intra, qdec, kdec, rgn, mgn, *, batch, seq):
    rows = MIXER_CHUNKS_PER_STEP * CHUNK
    nc = seq // rows
    row = lambda b, c: (b * nc + c, 0)
    state = pl.BlockSpec((1, 1, HEADS, HEAD_DIM, HEAD_DIM), lambda b, c: (0, b, 0, 0, 0))
    return pl.pallas_call(
        _prompt_mixer_kernel,
        grid=(batch, nc),
        in_specs=[
            pl.BlockSpec((rows, 8 * WIDTH), row),
            pl.BlockSpec((rows, LANES), row),
            pl.BlockSpec((4 * HEADS, rows), lambda b, c: (0, b * nc + c)),
            pl.BlockSpec((HEADS, CHUNK, CHUNK), lambda b, c: (0, 0, 0)),
            pl.BlockSpec((HEADS, CHUNK, HEAD_DIM), lambda b, c: (0, 0, 0)),
            pl.BlockSpec((HEADS, CHUNK), lambda b, c: (0, 0)),
            pl.BlockSpec((1, WIDTH), lambda b, c: (0, 0)),
            pl.BlockSpec((1, WIDTH), lambda b, c: (0, 0)),
        ],
        out_specs=[
            pl.BlockSpec((rows, 2 * WIDTH), row),
            state,
            state,
            pl.BlockSpec((1, 1, HEADS, HEAD_DIM), lambda b, c: (0, b, 0, 0)),
            pl.BlockSpec((1, SUBLANES, LANES), lambda b, c: (b, 0, 0)),
        ],
        out_shape=[
            jax.ShapeDtypeStruct((batch * seq, 2 * WIDTH), BF16),
            jax.ShapeDtypeStruct((1, batch, HEADS, HEAD_DIM, HEAD_DIM), F32),
            jax.ShapeDtypeStruct((1, batch, HEADS, HEAD_DIM, HEAD_DIM), F32),
            jax.ShapeDtypeStruct((1, batch, HEADS, HEAD_DIM), F32),
            jax.ShapeDtypeStruct((batch, SUBLANES, LANES), F32),
        ],
        scratch_shapes=[pltpu.VMEM((SUBLANES, LANES), F32)],
        compiler_params=_params("arbitrary", "arbitrary"),
        name="prompt_mixer",
    )(qkv, g_tok, g_t, intra, qdec, kdec, rgn, mgn)


def _sample_gate_kernel(g_ref, m0_ref, mt_ref, dw_ref, iw_ref, emt_ref):
    ig = g_ref[...]
    lf = pltpu.roll(ig, LANES - HEADS, 1)
    inter = lf + m0_ref[...]
    mt = jnp.maximum(inter, ig)
    mt_ref[...] = mt
    dw_ref[...] = jnp.exp(ig - mt)
    iw_ref[...] = jnp.exp(inter - mt)
    emt_ref[...] = jnp.exp(-mt)


def _sample_gates(g_tok, m0_pad):
    m = g_tok.shape[0]
    spec = pl.BlockSpec((m, LANES), lambda: (0, 0))
    return pl.pallas_call(
        _sample_gate_kernel,
        in_specs=[spec, spec],
        out_specs=[spec] * 4,
        out_shape=[jax.ShapeDtypeStruct((m, LANES), F32)] * 4,
        name="sample_gates",
    )(g_tok, m0_pad)


def _sample_mixer_kernel(scal_ref, rows_ref, s0_ref, c0_ref, n0_ref,
                         rgn_ref, mgn_ref, mix_ref, s_ref, c_ref, n_ref):
    first = lax.broadcasted_iota(jnp.int32, (4 * HEADS, HEAD_DIM), 0) == 0
    for t in range(SAMPLE_TOKENS_PER_STEP):
        b = pl.program_id(0) * SAMPLE_TOKENS_PER_STEP + t

        def row(group, h):
            lo = group * WIDTH + h * HEAD_DIM
            return rows_ref[pl.ds(b, 1), lo:lo + HEAD_DIM]

        def outer(k, v):
            kp = jnp.where(first, jnp.broadcast_to(k, first.shape), 0.0).astype(BF16)
            vp = jnp.where(first, jnp.broadcast_to(v, first.shape), 0.0).astype(BF16)
            return lax.dot_general(kp, vp, _TN, preferred_element_type=F32)

        def apply(q_row, state):
            q8 = jnp.broadcast_to(q_row, (SUBLANES, HEAD_DIM)).astype(BF16)
            return jnp.dot(q8, state.astype(BF16), preferred_element_type=F32)[0:1, :]

        dw = [scal_ref[b, HEADS + h] for h in range(HEADS)]
        iw = [scal_ref[b, 2 * HEADS + h] for h in range(HEADS)]
        emt = [scal_ref[b, 3 * HEADS + h] for h in range(HEADS)]
        kv_s = [outer(row(1, h), row(2, h)) for h in range(HEADS)]
        kv_c = [outer(row(5, h), dw[h] * row(6, h)) for h in range(HEADS)]
        s_new, c_new = [], []
        for h in range(HEADS):
            s_new.append(RET_G[h] * s0_ref[0, t, h] + kv_s[h])
            s_ref[0, t, h] = s_new[h]
            c_new.append(iw[h] * c0_ref[0, t, h] + kv_c[h])
            c_ref[0, t, h] = c_new[h]
        o = [apply(row(0, h), s_new[h]) for h in range(HEADS)]
        num = [apply(row(4, h), c_new[h]) for h in range(HEADS)]
        for h in range(HEADS):
            lo = h * HEAD_DIM
            mix_ref[pl.ds(b, 1), lo:lo + HEAD_DIM] = (_head_norm(o[h]) * rgn_ref[:, lo:lo + HEAD_DIM]
                                               * row(RG_GROUP, h))
            n_new = iw[h] * n0_ref[0, t, h:h + 1, :] + dw[h] * row(5, h)
            n_ref[0, t, h:h + 1, :] = n_new
            den = jnp.sum(row(4, h) * n_new, axis=1, keepdims=True)
            hid = num[h] * (1.0 / jnp.maximum(jnp.abs(den), emt[h]))
            mix_ref[pl.ds(b, 1), WIDTH + lo:WIDTH + lo + HEAD_DIM] = (
                _head_norm(hid) * mgn_ref[:, lo:lo + HEAD_DIM] * row(MO_GROUP, h))


def _sample_mixer(scal, rows, s0, c0, n0, rgn, mgn):
    nb = rows.shape[0]
    tb = SAMPLE_TOKENS_PER_STEP
    state = pl.BlockSpec((1, tb, HEADS, HEAD_DIM, HEAD_DIM), lambda b: (0, b, 0, 0, 0))
    nspec = pl.BlockSpec((1, tb, HEADS, HEAD_DIM), lambda b: (0, b, 0, 0))
    return pl.pallas_call(
        _sample_mixer_kernel,
        grid=(nb // tb,),
        in_specs=[
            pl.BlockSpec(memory_space=pltpu.SMEM),
            pl.BlockSpec((nb, 8 * WIDTH), lambda b: (0, 0)),
            state, state, nspec,
            pl.BlockSpec((1, WIDTH), lambda b: (0, 0)),
            pl.BlockSpec((1, WIDTH), lambda b: (0, 0)),
        ],
        out_specs=[pl.BlockSpec((nb, 2 * WIDTH), lambda b: (0, 0)), state, state, nspec],
        out_shape=[
            jax.ShapeDtypeStruct((nb, 2 * WIDTH), F32),
            jax.ShapeDtypeStruct(s0.shape, F32),
            jax.ShapeDtypeStruct(c0.shape, F32),
            jax.ShapeDtypeStruct(n0.shape, F32),
        ],
        compiler_params=_params("arbitrary"),
        name="sample_mixer",
    )(scal, rows, s0, c0, n0, rgn, mgn)


def _outproj_kernel(a_ref, w_ref, x_ref, g_ref, b_ref, x1_ref, x1b_ref, *wb_out):
    if wb_out:
        wb_out[0][...] = w_ref[...].astype(BF16)
        w_ref = wb_out[0]
    tm = a_ref.shape[0]
    for rows in _row_splits(tm):
        mix = jnp.dot(a_ref[rows, :].astype(BF16), w_ref[...], preferred_element_type=F32)
        x1 = _layer_norm(ALPHA * x_ref[rows, :] + mix, g_ref[...], b_ref[...])
        x1_ref[rows, :] = x1
        x1b_ref[rows, :] = x1.astype(BF16)


def _outproj(a, w, x, g, b, *, tm):
    m = a.shape[0]
    cast_w = w.dtype != BF16
    assert not cast_w or m == tm
    row = pl.BlockSpec((tm, D_MODEL), lambda i: (i, 0))
    vec = pl.BlockSpec((1, D_MODEL), lambda i: (0, 0))
    out_specs = [row, row]
    out_shape = [jax.ShapeDtypeStruct((m, D_MODEL), F32), jax.ShapeDtypeStruct((m, D_MODEL), BF16)]
    if cast_w:
        out_specs.append(pl.BlockSpec((D_MODEL, D_MODEL), lambda i: (0, 0)))
        out_shape.append(jax.ShapeDtypeStruct((D_MODEL, D_MODEL), BF16))
    return pl.pallas_call(
        _outproj_kernel,
        grid=(m // tm,),
        in_specs=[row, pl.BlockSpec((D_MODEL, D_MODEL), lambda i: (0, 0), pipeline_mode=pl.Buffered(1)),
                  row, vec, vec],
        out_specs=out_specs,
        out_shape=out_shape,
        compiler_params=_params("arbitrary"),
        name="outproj_ln1_cast" if cast_w else "outproj_ln1",
    )(a, w, x, g, b)


def _ffn_kernel(x1b_ref, w1_ref, w2_ref, o_ref, *wb_out):
    @pl.when(pl.program_id(1) == 0)
    def _():
        o_ref[...] = jnp.zeros_like(o_ref)

    w1 = w1_ref[...]
    w2 = w2_ref[...]
    if wb_out:
        w1 = w1.astype(BF16)
        w2 = w2.astype(BF16)
        wb_out[0][...] = w1
        wb_out[1][...] = w2
    hid = jnp.dot(x1b_ref[...], w1, preferred_element_type=F32)
    hid = jnp.square(jnp.maximum(hid, 0.0)).astype(BF16)
    o_ref[...] += jnp.dot(hid, w2, preferred_element_type=F32)


def _ffn(x1b, w1, w2, *, tm, tf):
    m = x1b.shape[0]
    cast_w = w1.dtype != BF16
    assert not cast_w or m == tm
    row = pl.BlockSpec((tm, D_MODEL), lambda i, f: (i, 0))
    w1spec = pl.BlockSpec((D_MODEL, tf), lambda i, f: (0, f))
    w2spec = pl.BlockSpec((tf, D_MODEL), lambda i, f: (f, 0))
    oshape = jax.ShapeDtypeStruct((m, D_MODEL), F32)
    return pl.pallas_call(
        _ffn_kernel,
        grid=(m // tm, D_FF // tf),
        in_specs=[row, w1spec, w2spec],
        out_specs=[row, w1spec, w2spec] if cast_w else row,
        out_shape=[oshape, jax.ShapeDtypeStruct(w1.shape, BF16),
                   jax.ShapeDtypeStruct(w2.shape, BF16)] if cast_w else oshape,
        compiler_params=_params("arbitrary", "arbitrary"),
        name="ffn_cast" if cast_w else "ffn",
    )(x1b, w1, w2)


def _pe_ln2_kernel(x1_ref, x1b_ref, p_ref, ff_ref, wpe_ref, wg_ref, g_ref, b_ref, o_ref, *wb_out):
    if wb_out:
        wb_out[0][...] = wpe_ref[...].astype(BF16)
        wb_out[1][...] = wg_ref[...].astype(BF16)
        wpe_ref, wg_ref = wb_out
    for rows in _row_splits(x1_ref.shape[0]):
        gate = _sigmoid(jnp.dot(x1b_ref[rows, :], wg_ref[...], preferred_element_type=F32))
        pe = jnp.dot(p_ref[rows, :].astype(BF16), wpe_ref[...], preferred_element_type=F32)
        y = ALPHA * x1_ref[rows, :] + ff_ref[rows, :] + pe * gate
        o_ref[rows, :] = _layer_norm(y, g_ref[...], b_ref[...])


def _pe_ln2(x1, x1b, p, ff, wpe, wg, g, b, *, tm):
    m = x1.shape[0]
    cast_w = wg.dtype != BF16
    assert not cast_w or m == tm
    row = pl.BlockSpec((tm, D_MODEL), lambda i: (i, 0))
    vec = pl.BlockSpec((1, D_MODEL), lambda i: (0, 0))
    oshape = jax.ShapeDtypeStruct((m, D_MODEL), F32)
    wb_specs = [pl.BlockSpec((PLE_DIM, D_MODEL), lambda i: (0, 0)),
                pl.BlockSpec((D_MODEL, D_MODEL), lambda i: (0, 0))]
    wb_shapes = [jax.ShapeDtypeStruct(wpe.shape, BF16), jax.ShapeDtypeStruct(wg.shape, BF16)]
    return pl.pallas_call(
        _pe_ln2_kernel,
        grid=(m // tm,),
        in_specs=[row, row, pl.BlockSpec((tm, PLE_DIM), lambda i: (i, 0)), row,
                  pl.BlockSpec((PLE_DIM, D_MODEL), lambda i: (0, 0), pipeline_mode=pl.Buffered(1)),
                  pl.BlockSpec((D_MODEL, D_MODEL), lambda i: (0, 0), pipeline_mode=pl.Buffered(1)),
                  vec, vec],
        out_specs=[row] + wb_specs if cast_w else row,
        out_shape=[oshape] + wb_shapes if cast_w else oshape,
        compiler_params=_params("arbitrary"),
        name="pe_ln2_cast" if cast_w else "pe_ln2",
    )(x1, x1b, p, ff, wpe, wg, g, b)


def _rope_tables(pos):
    inv = ROPE_BASE ** (-jnp.arange(HALF, dtype=F32) * 2.0 / HEAD_DIM)
    ang = pos[:, None] * inv[None, :]
    return jnp.cos(ang), jnp.sin(ang)


def _decay_tables():
    lg = jnp.log(1.0 - 2.0 ** (-5.0 - jnp.arange(HEADS, dtype=F32)))
    t = jnp.arange(CHUNK, dtype=F32)
    causal = t[:, None] >= t[None, :]
    intra = jnp.exp(jnp.where(causal, (t[:, None] - t[None, :]) * lg[:, None, None], -jnp.inf))
    q_decay = jnp.exp(lg[:, None] * (t + 1.0))
    k_decay = jnp.exp(lg[:, None] * (CHUNK - 1.0 - t))
    return intra, jnp.broadcast_to(q_decay[:, :, None], (HEADS, CHUNK, HEAD_DIM)), k_decay


def _tail(x, mixed, p, w, *, tm, tm_ffn, tf):
    casts = {}
    res = _outproj(mixed, w["out"], x, w["ln1_g"], w["ln1_b"], tm=tm)
    x1, x1b = res[0], res[1]
    if len(res) > 2:
        casts["out"] = res[2]
    res = _ffn(x1b, w["ff1"], w["ff2"], tm=tm_ffn, tf=tf)
    if isinstance(res, (list, tuple)):
        ff, casts["ff1"], casts["ff2"] = res
    else:
        ff = res
    res = _pe_ln2(x1, x1b, p, ff, w["pe"], w["pe_gate"], w["ln2_g"], w["ln2_b"], tm=tm)
    if isinstance(res, (list, tuple)):
        y, casts["pe"], casts["pe_gate"] = res
    else:
        y = res
    return y, casts


def kernel(x_prompt, x_sample, state_ret, state_mlstm_C, state_mlstm_n, state_mlstm_m, p_prompt, p_sample, w_in, b_gate, ret_gn_w, mlstm_gn_w, w_out, ln1_g, ln1_b, w_ff1, w_ff2, w_pe, w_pe_gate, ln2_g, ln2_b):
    batch, seq, _ = x_prompt.shape
    nb = x_sample.shape[0]

    w_in_t = w_in[0].T
    w8 = jnp.pad(w_in[0][:, 8 * WIDTH:], ((0, 0), (0, LANES - 2 * HEADS))).astype(BF16)
    b8 = jnp.pad(b_gate[0].astype(F32), (0, LANES - 2 * HEADS)).reshape(1, LANES)
    w = {
        "out": w_out[0], "ff1": w_ff1[0], "ff2": w_ff2[0], "pe": w_pe[0], "pe_gate": w_pe_gate[0],
        "ln1_g": ln1_g[0].reshape(1, D_MODEL), "ln1_b": ln1_b[0].reshape(1, D_MODEL),
        "ln2_g": ln2_g[0].reshape(1, D_MODEL), "ln2_b": ln2_b[0].reshape(1, D_MODEL),
    }
    rgn = ret_gn_w[0].reshape(1, WIDTH)
    mgn = mlstm_gn_w[0].reshape(1, WIDTH)
    intra, qdec, kdec = _decay_tables()

    xs = x_sample.reshape(nb, D_MODEL)
    xsb, gs_tok, _ = _mgate(xs, w8, b8, tm=nb)
    ctab, stab = _rope_tables(jnp.full((nb,), PAST_LEN, dtype=F32))
    proj_s, w_in_b = _inproj(xsb, w_in_t, ctab, stab, tm=nb)
    proj_s = proj_s.astype(F32)
    m0 = jnp.pad(state_mlstm_m[0], ((0, 0), (0, LANES - HEADS)))
    mt, dw, iw, emt = _sample_gates(gs_tok, m0)
    scal = jnp.concatenate([mt[:, :HEADS], dw[:, :HEADS], iw[:, :HEADS], emt[:, :HEADS]], axis=1)
    mixed_s, s_s, c_s, n_s = _sample_mixer(
        scal, proj_s,
        state_ret, state_mlstm_C, state_mlstm_n, rgn, mgn)
    m_s = mt[:, :HEADS].reshape(1, nb, HEADS)
    y_s, w_bf16 = _tail(xs, mixed_s, p_sample[0].reshape(nb, PLE_DIM), w,
                        tm=nb, tm_ffn=nb, tf=CAST_FFN_TF)

    w = {**w, **w_bf16}
    xp = x_prompt.reshape(batch * seq, D_MODEL)
    xpb, g_tok, g_t = _mgate(xp, w8, b8, tm=GATE_TM)
    ctab, stab = _rope_tables(jnp.arange(seq, dtype=F32))
    proj_p = _inproj(xpb, w_in_b, ctab, stab, tm=INPROJ_TM)
    mixed_p, s_p, c_p, n_p, m_rows = _prompt_mixer(proj_p, g_tok, g_t, intra, qdec, kdec, rgn, mgn,
                                                   batch=batch, seq=seq)
    m_p = m_rows[:, :HEADS, 0].reshape(1, batch, HEADS)
    y_p, _ = _tail(xp, mixed_p, p_prompt[0].reshape(batch * seq, PLE_DIM), w,
                   tm=TAIL_TM, tm_ffn=FFN_TM, tf=FFN_TF)

    return (y_p.reshape(batch, seq, D_MODEL), y_s.reshape(nb, 1, D_MODEL),
            s_p, c_p, n_p, m_p, s_s, c_s, n_s, m_s)
```

```python
import jax
import jax.numpy as jnp
import numpy as np
from jax import lax
from jax.experimental import pallas as pl
from jax.experimental.pallas import tpu as pltpu

F32 = jnp.float32
BF16 = jnp.bfloat16

D_MODEL = 2048
HEADS = 4
HEAD_DIM = 256
HALF = HEAD_DIM // 2
WIDTH = HEADS * HEAD_DIM
D_FF = 4 * D_MODEL
PLE_DIM = 256
CHUNK = 128
PAST_LEN = 16384
ROPE_BASE = 10000.0
LN_EPS = 1e-5
GN_EPS = 1e-6
DEPTH = 1
ALPHA = (2 * DEPTH) ** 0.25
QK_SCALE = HEAD_DIM ** -0.5
LANES = 128
SUBLANES = 8
MXU_ROWS = 256
VMEM_LIMIT = 56 * 1024 * 1024
MIXERS_VMEM_LIMIT = 60 * 1024 * 1024
SAMPLE_TOKENS_PER_STEP = 4
MIXER_CHUNKS_PER_STEP = 2
GATE_TM = 1024
INPROJ_TM = 2048
TAIL_TM = 512
FFN_TM = 1024
FFN_TF = 1024
CAST_FFN_TF = 512

RET_G = tuple(1.0 - 2.0 ** (-5.0 - h) for h in range(HEADS))
RET_LOG_G = tuple(float(np.log(np.float32(g))) for g in RET_G)
RET_STATE_DECAY = tuple(float(np.exp(np.float32(lg) * np.float32(CHUNK))) for lg in RET_LOG_G)

_NT = (((1,), (1,)), ((), ()))
_TN = (((0,), (0,)), ((), ()))


def _params(*sem, vmem=None):
    return pltpu.CompilerParams(dimension_semantics=sem, vmem_limit_bytes=vmem or VMEM_LIMIT)


def _sigmoid(x):
    return 0.5 * jnp.tanh(0.5 * x) + 0.5


def _log_sigmoid(x):
    return jnp.minimum(x, 0.0) - jnp.log(1.0 + jnp.exp(-jnp.abs(x)))


def _layer_norm(y, g, b):
    mu = jnp.mean(y, axis=-1, keepdims=True)
    yc = y - mu
    var = jnp.mean(jnp.square(yc), axis=-1, keepdims=True)
    return yc * lax.rsqrt(var + LN_EPS) * g + b


def _row_splits(tm):
    if tm < 2 * MXU_ROWS:
        return [slice(0, tm)]
    return [slice(0, tm // 2), slice(tm // 2, tm)]


def _head_norm(o):
    mu = jnp.mean(o, axis=-1, keepdims=True)
    oc = o - mu
    var = jnp.mean(jnp.square(oc), axis=-1, keepdims=True)
    return oc * lax.rsqrt(var + GN_EPS)


RG_GROUP, MO_GROUP = 3, 7


def _inproj_kernel(x_ref, wt_ref, c_ref, s_ref, o_ref, *wb_out):
    j = pl.program_id(0)
    if wb_out:
        wb_ref, = wb_out
        wb_ref[...] = wt_ref[...].astype(BF16)
    else:
        wb_ref = wt_ref

    is_gate = (j == RG_GROUP) | (j == MO_GROUP)
    pieces = [(h, rows) for h in range(HEADS) for rows in _row_splits(x_ref.shape[0])]

    def piece_acc(h, rows):
        return lax.dot_general(x_ref[rows, :], wb_ref[h * HEAD_DIM:(h + 1) * HEAD_DIM, :], _NT,
                               preferred_element_type=F32)

    @pl.when(is_gate)
    def _():
        for h, rows in pieces:
            acc = piece_acc(h, rows)
            sig = _sigmoid(acc)
            o_ref[rows, h * HEAD_DIM:(h + 1) * HEAD_DIM] = jnp.where(
                j == RG_GROUP, acc * sig, sig).astype(o_ref.dtype)

    @pl.when(jnp.logical_not(is_gate))
    def _():
        rotary = j <= 1
        scale = jnp.where((j == 1) | (j == MO_GROUP - 2), QK_SCALE, 1.0).astype(F32)
        for h, rows in pieces:
            lo = h * HEAD_DIM
            c = jnp.where(rotary, c_ref[rows, :], 1.0) * scale
            s = jnp.where(rotary, s_ref[rows, :], 0.0) * scale
            acc = piece_acc(h, rows)
            x1 = acc[:, :HALF]
            x2 = acc[:, HALF:]
            o_ref[rows, lo:lo + HALF] = (x1 * c - x2 * s).astype(o_ref.dtype)
            o_ref[rows, lo + HALF:lo + HEAD_DIM] = (x1 * s + x2 * c).astype(o_ref.dtype)


def _inproj(xb, wt, cos, sin, *, tm):
    m = xb.shape[0]
    n_pos = cos.shape[0] // tm
    cast_w = wt.dtype != BF16
    assert not cast_w or m == tm
    wspec = pl.BlockSpec((WIDTH, D_MODEL), lambda j, i: (j, 0))
    ospec = pl.BlockSpec((tm, WIDTH), lambda j, i: (i, j))
    oshape = jax.ShapeDtypeStruct((m, 8 * WIDTH), BF16)
    tspec = pl.BlockSpec((tm, HALF), lambda j, i: (i % n_pos, 0))

    return pl.pallas_call(
        _inproj_kernel,
        grid=(8, m // tm),
        in_specs=[pl.BlockSpec((tm, D_MODEL), lambda j, i: (i, 0)), wspec, tspec, tspec],
        out_specs=[ospec, wspec] if cast_w else ospec,
        out_shape=[oshape, jax.ShapeDtypeStruct((8 * WIDTH, D_MODEL), BF16)] if cast_w else oshape,
        compiler_params=_params("arbitrary", "arbitrary"),
        name="inproj_cast" if cast_w else "inproj",
    )(xb, wt, cos, sin)


def _mgate_kernel(x_ref, w_ref, b_ref, xb_ref, g_ref, gt_ref):
    xb = x_ref[...].astype(BF16)
    xb_ref[...] = xb
    g = jnp.dot(xb, w_ref[...], preferred_element_type=F32) + b_ref[...]
    lane = lax.broadcasted_iota(jnp.int32, g.shape, 1)
    g = jnp.where((lane >= HEADS) & (lane < 2 * HEADS), _log_sigmoid(g), g)
    r = lax.broadcasted_iota(jnp.int32, (CHUNK, CHUNK), 0)
    s = lax.broadcasted_iota(jnp.int32, (CHUNK, CHUNK), 1)
    tril = (r >= s).astype(BF16)
    g1 = g.astype(BF16)
    res = g - g1.astype(F32)
    g2 = res.astype(BF16)
    g3 = (res - g2.astype(F32)).astype(BF16)
    parts = []
    for c in range(g.shape[0] // CHUNK):
        sl = slice(c * CHUNK, (c + 1) * CHUNK)
        parts.append(jnp.dot(tril, g1[sl], preferred_element_type=F32)
                     + jnp.dot(tril, g2[sl], preferred_element_type=F32)
                     + jnp.dot(tril, g3[sl], preferred_element_type=F32))
    cs = parts[0] if len(parts) == 1 else jnp.concatenate(parts, axis=0)
    b = pltpu.roll(cs, HEADS, 1)
    u = pltpu.roll(g, 3 * HEADS, 1) - pltpu.roll(cs, 2 * HEADS, 1)
    g = jnp.where((lane >= 2 * HEADS) & (lane < 3 * HEADS), b, g)
    g = jnp.where((lane >= 3 * HEADS) & (lane < 4 * HEADS), u, g)
    g_ref[...] = g
    gt_ref[...] = g.T[0:4 * HEADS, :]


def _mgate(x, w8, b8, *, tm):
    m = x.shape[0]
    return pl.pallas_call(
        _mgate_kernel,
        grid=(m // tm,),
        in_specs=[
            pl.BlockSpec((tm, D_MODEL), lambda i: (i, 0)),
            pl.BlockSpec((D_MODEL, LANES), lambda i: (0, 0)),
            pl.BlockSpec((1, LANES), lambda i: (0, 0)),
        ],
        out_specs=[
            pl.BlockSpec((tm, D_MODEL), lambda i: (i, 0)),
            pl.BlockSpec((tm, LANES), lambda i: (i, 0)),
            pl.BlockSpec((4 * HEADS, tm), lambda i: (0, i)),
        ],
        out_shape=[
            jax.ShapeDtypeStruct((m, D_MODEL), BF16),
            jax.ShapeDtypeStruct((m, LANES), F32),
            jax.ShapeDtypeStruct((4 * HEADS, m), F32),
        ],
        compiler_params=_params("arbitrary"),
        name="mlstm_gates",
    )(x, w8, b8)


def _mixer_chunk(rows, causal, qkv_ref, g_ref, gt_ref, intra_ref, dec_ref, kdec_ref, rgn_ref, mgn_ref,
                 mix_ref, s_ref, c_ref, n_ref, m_scr):
    def head(group, h):
        lo = group * WIDTH + h * HEAD_DIM
        return qkv_ref[rows, lo:lo + HEAD_DIM]

    gw = []
    for h in range(HEADS):
        b_col = g_ref[rows, 2 * HEADS + h:2 * HEADS + h + 1]
        u_row = gt_ref[3 * HEADS + h:3 * HEADS + h + 1, rows]
        m_prev = m_scr[h:h + 1, 0:1]
        dlog = jnp.where(causal, b_col + u_row, -jnp.inf)
        inter = b_col + m_prev
        mt = jnp.maximum(inter, jnp.max(dlog, axis=1, keepdims=True))
        m_new = mt[CHUNK - 1:CHUNK, :]
        b_last = b_col[CHUNK - 1:CHUNK, :]
        gw.append(dict(
            dw=jnp.exp(dlog - mt), iw=jnp.exp(inter - mt), emt=jnp.exp(-mt),
            sw=jnp.exp(b_last - m_new + u_row), sd=jnp.exp(b_last + m_prev - m_new)))
        m_scr[h:h + 1, :] = jnp.broadcast_to(m_new, (1, LANES))

    st = []
    for h in range(HEADS):
        for kind, (qg, state_ref) in enumerate(((0, s_ref), (MO_GROUP - 3, c_ref))):
            q = head(qg, h)
            k = head(qg + 1, h)
            old = state_ref[0, 0, h]
            kt = k.T
            st.append(dict(
                h=h, kind=kind, q=q, k=k, kt=kt, v=head(qg + 2, h), old=old,
                sc=jnp.dot(q, kt, preferred_element_type=F32),
                inter=jnp.dot(q, old.astype(BF16), preferred_element_type=F32)))

    for e in st:
        h = e["h"]
        if e["kind"] == 0:
            kd = (e["kt"].astype(F32) * kdec_ref[h:h + 1, :]).astype(BF16)
            s_ref[0, 0, h] = (e["old"] * RET_STATE_DECAY[h]
                              + jnp.dot(kd, e["v"], preferred_element_type=F32))
        else:
            w = gw[h]
            ksw = (e["kt"].astype(F32) * w["sw"]).astype(BF16)
            c_ref[0, 0, h] = w["sd"] * e["old"] + jnp.dot(ksw, e["v"], preferred_element_type=F32)
            n_old = n_ref[0, 0, h:h + 1, :]
            e["qn"] = jnp.sum(e["q"].astype(F32) * n_old, axis=1, keepdims=True)
            sw8 = jnp.broadcast_to(w["sw"], (SUBLANES, CHUNK)).astype(BF16)
            n_ref[0, 0, h:h + 1, :] = (w["sd"] * n_old
                                       + jnp.dot(sw8, e["k"], preferred_element_type=F32)[0:1, :])

    for e in st:
        h = e["h"]
        e["sc"] = e["sc"] * (intra_ref[h] if e["kind"] == 0 else gw[h]["dw"])
        e["pv"] = jnp.dot(e["sc"].astype(BF16), e["v"], preferred_element_type=F32)

    for e in st:
        e["rows"] = rows
        if e["kind"]:
            e["iw"] = gw[e["h"]]["iw"]
            e["emt"] = gw[e["h"]]["emt"]
    return st


def _mixer_tail(streams, qkv_ref, dec_ref, rgn_ref, mgn_ref, mix_ref):
    dens = [jnp.sum(e["sc"], axis=1, keepdims=True) if e["kind"] else None for e in streams]
    outs = []
    for e, den in zip(streams, dens):
        if e["kind"] == 0:
            outs.append(e["pv"] + e["inter"] * dec_ref[e["h"]])
        else:
            num = e["pv"] + e["iw"] * e["inter"]
            den = den + e["iw"] * e["qn"]
            outs.append(num * (1.0 / jnp.maximum(jnp.abs(den), e["emt"])))
    mus = [jnp.mean(o, axis=-1, keepdims=True) for o in outs]
    cen = [o - mu for o, mu in zip(outs, mus)]
    var = [jnp.mean(jnp.square(oc), axis=-1, keepdims=True) for oc in cen]
    for e, oc, v in zip(streams, cen, var):
        rows = e["rows"]
        lo = e["h"] * HEAD_DIM
        gate_lo = (MO_GROUP if e["kind"] else RG_GROUP) * WIDTH + lo
        gain_ref = mgn_ref if e["kind"] else rgn_ref
        out_lo = e["kind"] * WIDTH + lo
        y = (oc * lax.rsqrt(v + GN_EPS) * gain_ref[:, lo:lo + HEAD_DIM]
             * qkv_ref[rows, gate_lo:gate_lo + HEAD_DIM].astype(F32))
        mix_ref[rows, out_lo:out_lo + HEAD_DIM] = y.astype(mix_ref.dtype)


def _mixers_kernel(scal_ref, rows_ref, s0_ref, c0_ref, n0_ref,
                   qkv_ref, g_ref, gt_ref, intra_ref, dec_ref, kdec_ref, rgn_ref, mgn_ref,
                   smix_ref, ss_ref, sc_ref, sn_ref,
                   mix_ref, s_ref, c_ref, n_ref, m_ref, m_scr):
    c = pl.program_id(1)
    _sample_tokens(pl.program_id(0) * pl.num_programs(1) + c, scal_ref, rows_ref, s0_ref, c0_ref,
                   n0_ref, rgn_ref, mgn_ref, smix_ref, ss_ref, sc_ref, sn_ref)

    @pl.when(c == 0)
    def _():
        s_ref[...] = jnp.zeros_like(s_ref)
        c_ref[...] = jnp.zeros_like(c_ref)
        n_ref[...] = jnp.zeros_like(n_ref)
        m_scr[...] = jnp.zeros_like(m_scr)

    ti = lax.broadcasted_iota(jnp.int32, (CHUNK, CHUNK), 0)
    si = lax.broadcasted_iota(jnp.int32, (CHUNK, CHUNK), 1)
    causal = ti >= si

    streams = []
    for ci in range(MIXER_CHUNKS_PER_STEP):
        streams += _mixer_chunk(slice(ci * CHUNK, (ci + 1) * CHUNK), causal, qkv_ref, g_ref, gt_ref,
                                intra_ref, dec_ref, kdec_ref, rgn_ref, mgn_ref, mix_ref, s_ref, c_ref,
                                n_ref, m_scr)
    _mixer_tail(streams, qkv_ref, dec_ref, rgn_ref, mgn_ref, mix_ref)

    @pl.when(c == pl.num_programs(1) - 1)
    def _():
        m_ref[0] = m_scr[...]


def _mixers(scal, rows_s, s0, c0, n0, qkv, g_tok, g_t, intra, qdec, kdec, rgn, mgn, *, batch, seq):
    rows = MIXER_CHUNKS_PER_STEP * CHUNK
    nc = seq // rows
    nb = rows_s.shape[0]
    tb = SAMPLE_TOKENS_PER_STEP
    assert nb == batch * nc * tb
    row = lambda b, c: (b * nc + c, 0)
    const2 = lambda b, c: (0, 0)
    state = pl.BlockSpec((1, 1, HEADS, HEAD_DIM, HEAD_DIM), lambda b, c: (0, b, 0, 0, 0))
    sstate = pl.BlockSpec((1, tb, HEADS, HEAD_DIM, HEAD_DIM), lambda b, c: (0, b * nc + c, 0, 0, 0))
    snorm = pl.BlockSpec((1, tb, HEADS, HEAD_DIM), lambda b, c: (0, b * nc + c, 0, 0))
    return pl.pallas_call(
        _mixers_kernel,
        grid=(batch, nc),
        in_specs=[
            pl.BlockSpec(memory_space=pltpu.SMEM),
            pl.BlockSpec((nb, 8 * WIDTH), const2, pipeline_mode=pl.Buffered(1)),
            sstate, sstate, snorm,
            pl.BlockSpec((rows, 8 * WIDTH), row),
            pl.BlockSpec((rows, LANES), row),
            pl.BlockSpec((4 * HEADS, rows), lambda b, c: (0, b * nc + c)),
            pl.BlockSpec((HEADS, CHUNK, CHUNK), lambda b, c: (0, 0, 0)),
            pl.BlockSpec((HEADS, CHUNK, HEAD_DIM), lambda b, c: (0, 0, 0)),
            pl.BlockSpec((HEADS, CHUNK), const2),
            pl.BlockSpec((1, WIDTH), const2),
            pl.BlockSpec((1, WIDTH), const2),
        ],
        out_specs=[
            pl.BlockSpec((nb, 2 * WIDTH), const2), sstate, sstate, snorm,
            pl.BlockSpec((rows, 2 * WIDTH), row),
            state,
            state,
            pl.BlockSpec((1, 1, HEADS, HEAD_DIM), lambda b, c: (0, b, 0, 0)),
            pl.BlockSpec((1, SUBLANES, LANES), lambda b, c: (b, 0, 0)),
        ],
        out_shape=[
            jax.ShapeDtypeStruct((nb, 2 * WIDTH), F32),
            jax.ShapeDtypeStruct(s0.shape, F32),
            jax.ShapeDtypeStruct(c0.shape, F32),
            jax.ShapeDtypeStruct(n0.shape, F32),
            jax.ShapeDtypeStruct((batch * seq, 2 * WIDTH), BF16),
            jax.ShapeDtypeStruct((1, batch, HEADS, HEAD_DIM, HEAD_DIM), F32),
            jax.ShapeDtypeStruct((1, batch, HEADS, HEAD_DIM, HEAD_DIM), F32),
            jax.ShapeDtypeStruct((1, batch, HEADS, HEAD_DIM), F32),
            jax.ShapeDtypeStruct((batch, SUBLANES, LANES), F32),
        ],
        scratch_shapes=[pltpu.VMEM((SUBLANES, LANES), F32)],
        compiler_params=_params("arbitrary", "arbitrary", vmem=MIXERS_VMEM_LIMIT),
        name="mixers",
    )(scal, rows_s, s0, c0, n0, qkv, g_tok, g_t, intra, qdec, kdec, rgn, mgn)


def _sample_gate_kernel(g_ref, m0_ref, mt_ref, dw_ref, iw_ref, emt_ref):
    ig = g_ref[...]
    lf = pltpu.roll(ig, LANES - HEADS, 1)
    inter = lf + m0_ref[...]
    mt = jnp.maximum(inter, ig)
    mt_ref[...] = mt
    dw_ref[...] = jnp.exp(ig - mt)
    iw_ref[...] = jnp.exp(inter - mt)
    emt_ref[...] = jnp.exp(-mt)


def _sample_gates(g_tok, m0_pad):
    m = g_tok.shape[0]
    spec = pl.BlockSpec((m, LANES), lambda: (0, 0))
    return pl.pallas_call(
        _sample_gate_kernel,
        in_specs=[spec, spec],
        out_specs=[spec] * 4,
        out_shape=[jax.ShapeDtypeStruct((m, LANES), F32)] * 4,
        name="sample_gates",
    )(g_tok, m0_pad)


def _sample_tokens(step, scal_ref, rows_ref, s0_ref, c0_ref, n0_ref,
                   rgn_ref, mgn_ref, mix_ref, s_ref, c_ref, n_ref):
    first = lax.broadcasted_iota(jnp.int32, (4 * HEADS, HEAD_DIM), 0) == 0
    for t in range(SAMPLE_TOKENS_PER_STEP):
        b = step * SAMPLE_TOKENS_PER_STEP + t

        def row(group, h):
            lo = group * WIDTH + h * HEAD_DIM
            return rows_ref[pl.ds(b, 1), lo:lo + HEAD_DIM]

        def outer(k, v):
            kp = jnp.where(first, jnp.broadcast_to(k, first.shape), 0.0).astype(BF16)
            vp = jnp.where(first, jnp.broadcast_to(v, first.shape), 0.0).astype(BF16)
            return lax.dot_general(kp, vp, _TN, preferred_element_type=F32)

        def apply(q_row, state):
            q8 = jnp.broadcast_to(q_row, (SUBLANES, HEAD_DIM)).astype(BF16)
            return jnp.dot(q8, state.astype(BF16), preferred_element_type=F32)[0:1, :]

        dw = [scal_ref[b, HEADS + h] for h in range(HEADS)]
        iw = [scal_ref[b, 2 * HEADS + h] for h in range(HEADS)]
        emt = [scal_ref[b, 3 * HEADS + h] for h in range(HEADS)]
        kv_s = [outer(row(1, h), row(2, h)) for h in range(HEADS)]
        kv_c = [outer(row(5, h), dw[h] * row(6, h)) for h in range(HEADS)]
        s_new, c_new = [], []
        for h in range(HEADS):
            s_new.append(RET_G[h] * s0_ref[0, t, h] + kv_s[h])
            s_ref[0, t, h] = s_new[h]
            c_new.append(iw[h] * c0_ref[0, t, h] + kv_c[h])
            c_ref[0, t, h] = c_new[h]
        o = [apply(row(0, h), s_new[h]) for h in range(HEADS)]
        num = [apply(row(4, h), c_new[h]) for h in range(HEADS)]
        for h in range(HEADS):
            lo = h * HEAD_DIM
            mix_ref[pl.ds(b, 1), lo:lo + HEAD_DIM] = (_head_norm(o[h]) * rgn_ref[:, lo:lo + HEAD_DIM]
                                               * row(RG_GROUP, h))
            n_new = iw[h] * n0_ref[0, t, h:h + 1, :] + dw[h] * row(5, h)
            n_ref[0, t, h:h + 1, :] = n_new
            den = jnp.sum(row(4, h) * n_new, axis=1, keepdims=True)
            hid = num[h] * (1.0 / jnp.maximum(jnp.abs(den), emt[h]))
            mix_ref[pl.ds(b, 1), WIDTH + lo:WIDTH + lo + HEAD_DIM] = (
                _head_norm(hid) * mgn_ref[:, lo:lo + HEAD_DIM] * row(MO_GROUP, h))


def _outproj_kernel(a_ref, w_ref, x_ref, g_ref, b_ref, x1_ref, x1b_ref, *wb_out):
    if wb_out:
        wb_out[0][...] = w_ref[...].astype(BF16)
        w_ref = wb_out[0]
    tm = a_ref.shape[0]
    for rows in _row_splits(tm):
        mix = jnp.dot(a_ref[rows, :].astype(BF16), w_ref[...], preferred_element_type=F32)
        x1 = _layer_norm(ALPHA * x_ref[rows, :] + mix, g_ref[...], b_ref[...])
        x1_ref[rows, :] = x1
        x1b_ref[rows, :] = x1.astype(BF16)


def _outproj(a, w, x, g, b, *, tm):
    m = a.shape[0]
    cast_w = w.dtype != BF16
    assert not cast_w or m == tm
    row = pl.BlockSpec((tm, D_MODEL), lambda i: (i, 0))
    vec = pl.BlockSpec((1, D_MODEL), lambda i: (0, 0))
    out_specs = [row, row]
    out_shape = [jax.ShapeDtypeStruct((m, D_MODEL), F32), jax.ShapeDtypeStruct((m, D_MODEL), BF16)]
    if cast_w:
        out_specs.append(pl.BlockSpec((D_MODEL, D_MODEL), lambda i: (0, 0)))
        out_shape.append(jax.ShapeDtypeStruct((D_MODEL, D_MODEL), BF16))
    return pl.pallas_call(
        _outproj_kernel,
        grid=(m // tm,),
        in_specs=[row, pl.BlockSpec((D_MODEL, D_MODEL), lambda i: (0, 0), pipeline_mode=pl.Buffered(1)),
                  row, vec, vec],
        out_specs=out_specs,
        out_shape=out_shape,
        compiler_params=_params("arbitrary"),
        name="outproj_ln1_cast" if cast_w else "outproj_ln1",
    )(a, w, x, g, b)


def _ffn_kernel(x1b_ref, w1_ref, w2_ref, o_ref, *wb_out):
    @pl.when(pl.program_id(1) == 0)
    def _():
        o_ref[...] = jnp.zeros_like(o_ref)

    w1 = w1_ref[...]
    w2 = w2_ref[...]
    if wb_out:
        w1 = w1.astype(BF16)
        w2 = w2.astype(BF16)
        wb_out[0][...] = w1
        wb_out[1][...] = w2
    hid = jnp.dot(x1b_ref[...], w1, preferred_element_type=F32)
    hid = jnp.square(jnp.maximum(hid, 0.0)).astype(BF16)
    o_ref[...] += jnp.dot(hid, w2, preferred_element_type=F32)


def _ffn(x1b, w1, w2, *, tm, tf):
    m = x1b.shape[0]
    cast_w = w1.dtype != BF16
    assert not cast_w or m == tm
    row = pl.BlockSpec((tm, D_MODEL), lambda i, f: (i, 0))
    w1spec = pl.BlockSpec((D_MODEL, tf), lambda i, f: (0, f))
    w2spec = pl.BlockSpec((tf, D_MODEL), lambda i, f: (f, 0))
    oshape = jax.ShapeDtypeStruct((m, D_MODEL), F32)
    return pl.pallas_call(
        _ffn_kernel,
        grid=(m // tm, D_FF // tf),
        in_specs=[row, w1spec, w2spec],
        out_specs=[row, w1spec, w2spec] if cast_w else row,
        out_shape=[oshape, jax.ShapeDtypeStruct(w1.shape, BF16),
                   jax.ShapeDtypeStruct(w2.shape, BF16)] if cast_w else oshape,
        compiler_params=_params("arbitrary", "arbitrary"),
        name="ffn_cast" if cast_w else "ffn",
    )(x1b, w1, w2)


def _pe_ln2_kernel(x1_ref, x1b_ref, p_ref, ff_ref, wpe_ref, wg_ref, g_ref, b_ref, o_ref, *wb_out):
    if wb_out:
        wb_out[0][...] = wpe_ref[...].astype(BF16)
        wb_out[1][...] = wg_ref[...].astype(BF16)
        wpe_ref, wg_ref = wb_out
    for rows in _row_splits(x1_ref.shape[0]):
        gate = _sigmoid(jnp.dot(x1b_ref[rows, :], wg_ref[...], preferred_element_type=F32))
        pe = jnp.dot(p_ref[rows, :].astype(BF16), wpe_ref[...], preferred_element_type=F32)
        y = ALPHA * x1_ref[rows, :] + ff_ref[rows, :] + pe * gate
        o_ref[rows, :] = _layer_norm(y, g_ref[...], b_ref[...])


def _pe_ln2(x1, x1b, p, ff, wpe, wg, g, b, *, tm):
    m = x1.shape[0]
    cast_w = wg.dtype != BF16
    assert not cast_w or m == tm
    row = pl.BlockSpec((tm, D_MODEL), lambda i: (i, 0))
    vec = pl.BlockSpec((1, D_MODEL), lambda i: (0, 0))
    oshape = jax.ShapeDtypeStruct((m, D_MODEL), F32)
    wb_specs = [pl.BlockSpec((PLE_DIM, D_MODEL), lambda i: (0, 0)),
                pl.BlockSpec((D_MODEL, D_MODEL), lambda i: (0, 0))]
    wb_shapes = [jax.ShapeDtypeStruct(wpe.shape, BF16), jax.ShapeDtypeStruct(wg.shape, BF16)]
    return pl.pallas_call(
        _pe_ln2_kernel,
        grid=(m // tm,),
        in_specs=[row, row, pl.BlockSpec((tm, PLE_DIM), lambda i: (i, 0)), row,
                  pl.BlockSpec((PLE_DIM, D_MODEL), lambda i: (0, 0), pipeline_mode=pl.Buffered(1)),
                  pl.BlockSpec((D_MODEL, D_MODEL), lambda i: (0, 0), pipeline_mode=pl.Buffered(1)),
                  vec, vec],
        out_specs=[row] + wb_specs if cast_w else row,
        out_shape=[oshape] + wb_shapes if cast_w else oshape,
        compiler_params=_params("arbitrary"),
        name="pe_ln2_cast" if cast_w else "pe_ln2",
    )(x1, x1b, p, ff, wpe, wg, g, b)


def _rope_tables(pos):
    inv = ROPE_BASE ** (-jnp.arange(HALF, dtype=F32) * 2.0 / HEAD_DIM)
    ang = pos[:, None] * inv[None, :]
    return jnp.cos(ang), jnp.sin(ang)


def _decay_tables():
    lg = jnp.log(1.0 - 2.0 ** (-5.0 - jnp.arange(HEADS, dtype=F32)))
    t = jnp.arange(CHUNK, dtype=F32)
    causal = t[:, None] >= t[None, :]
    intra = jnp.exp(jnp.where(causal, (t[:, None] - t[None, :]) * lg[:, None, None], -jnp.inf))
    q_decay = jnp.exp(lg[:, None] * (t + 1.0))
    k_decay = jnp.exp(lg[:, None] * (CHUNK - 1.0 - t))
    return intra, jnp.broadcast_to(q_decay[:, :, None], (HEADS, CHUNK, HEAD_DIM)), k_decay


def _tail(x, mixed, p, w, *, tm, tm_ffn, tf):
    casts = {}
    res = _outproj(mixed, w["out"], x, w["ln1_g"], w["ln1_b"], tm=tm)
    x1, x1b = res[0], res[1]
    if len(res) > 2:
        casts["out"] = res[2]
    res = _ffn(x1b, w["ff1"], w["ff2"], tm=tm_ffn, tf=tf)
    if isinstance(res, (list, tuple)):
        ff, casts["ff1"], casts["ff2"] = res
    else:
        ff = res
    res = _pe_ln2(x1, x1b, p, ff, w["pe"], w["pe_gate"], w["ln2_g"], w["ln2_b"], tm=tm)
    if isinstance(res, (list, tuple)):
        y, casts["pe"], casts["pe_gate"] = res
    else:
        y = res
    return y, casts


def kernel(x_prompt, x_sample, state_ret, state_mlstm_C, state_mlstm_n, state_mlstm_m, p_prompt, p_sample, w_in, b_gate, ret_gn_w, mlstm_gn_w, w_out, ln1_g, ln1_b, w_ff1, w_ff2, w_pe, w_pe_gate, ln2_g, ln2_b):
    batch, seq, _ = x_prompt.shape
    nb = x_sample.shape[0]

    w_in_t = w_in[0].T
    w8 = jnp.pad(w_in[0][:, 8 * WIDTH:], ((0, 0), (0, LANES - 2 * HEADS))).astype(BF16)
    b8 = jnp.pad(b_gate[0].astype(F32), (0, LANES - 2 * HEADS)).reshape(1, LANES)
    w = {
        "out": w_out[0], "ff1": w_ff1[0], "ff2": w_ff2[0], "pe": w_pe[0], "pe_gate": w_pe_gate[0],
        "ln1_g": ln1_g[0].reshape(1, D_MODEL), "ln1_b": ln1_b[0].reshape(1, D_MODEL),
        "ln2_g": ln2_g[0].reshape(1, D_MODEL), "ln2_b": ln2_b[0].reshape(1, D_MODEL),
    }
    rgn = ret_gn_w[0].reshape(1, WIDTH)
    mgn = mlstm_gn_w[0].reshape(1, WIDTH)
    intra, qdec, kdec = _decay_tables()

    xs = x_sample.reshape(nb, D_MODEL)
    xsb, gs_tok, _ = _mgate(xs, w8, b8, tm=nb)
    ctab, stab = _rope_tables(jnp.full((nb,), PAST_LEN, dtype=F32))
    proj_s, w_in_b = _inproj(xsb, w_in_t, ctab, stab, tm=nb)
    proj_s = proj_s.astype(F32)
    m0 = jnp.pad(state_mlstm_m[0], ((0, 0), (0, LANES - HEADS)))
    mt, dw, iw, emt = _sample_gates(gs_tok, m0)
    scal = jnp.concatenate([mt[:, :HEADS], dw[:, :HEADS], iw[:, :HEADS], emt[:, :HEADS]], axis=1)
    m_s = mt[:, :HEADS].reshape(1, nb, HEADS)

    xp = x_prompt.reshape(batch * seq, D_MODEL)
    xpb, g_tok, g_t = _mgate(xp, w8, b8, tm=GATE_TM)
    ctab, stab = _rope_tables(jnp.arange(seq, dtype=F32))
    proj_p = _inproj(xpb, w_in_b, ctab, stab, tm=INPROJ_TM)
    mixed_s, s_s, c_s, n_s, mixed_p, s_p, c_p, n_p, m_rows = _mixers(
        scal, proj_s, state_ret, state_mlstm_C, state_mlstm_n,
        proj_p, g_tok, g_t, intra, qdec, kdec, rgn, mgn, batch=batch, seq=seq)
    m_p = m_rows[:, :HEADS, 0].reshape(1, batch, HEADS)

    y_s, w_bf16 = _tail(xs, mixed_s, p_sample[0].reshape(nb, PLE_DIM), w,
                        tm=nb, tm_ffn=nb, tf=CAST_FFN_TF)
    w = {**w, **w_bf16}
    y_p, _ = _tail(xp, mixed_p, p_prompt[0].reshape(batch * seq, PLE_DIM), w,
                   tm=TAIL_TM, tm_ffn=FFN_TM, tf=FFN_TF)

    return (y_p.reshape(batch, seq, D_MODEL), y_s.reshape(nb, 1, D_MODEL),
            s_p, c_p, n_p, m_p, s_s, c_s, n_s, m_s)
```

```python
import jax
import jax.numpy as jnp
import numpy as np
from jax import lax
from jax.experimental import pallas as pl
from jax.experimental.pallas import tpu as pltpu

F32 = jnp.float32
BF16 = jnp.bfloat16

D_MODEL = 2048
HEADS = 4
HEAD_DIM = 256
HALF = HEAD_DIM // 2
WIDTH = HEADS * HEAD_DIM
D_FF = 4 * D_MODEL
PLE_DIM = 256
CHUNK = 128
PAST_LEN = 16384
ROPE_BASE = 10000.0
LN_EPS = 1e-5
GN_EPS = 1e-6
DEPTH = 1
ALPHA = (2 * DEPTH) ** 0.25
QK_SCALE = HEAD_DIM ** -0.5
LANES = 128
SUBLANES = 8
MXU_ROWS = 256
VMEM_LIMIT = 56 * 1024 * 1024
MIXERS_VMEM_LIMIT = 60 * 1024 * 1024
SAMPLE_TOKENS_PER_STEP = 4
MIXER_CHUNKS_PER_STEP = 2
GATE_TM = 1024
INPROJ_TM = 2048
TAIL_TM = 512
FFN_TM = 1024
FFN_TF = 1024
CAST_FFN_TF = 512

RET_G = tuple(1.0 - 2.0 ** (-5.0 - h) for h in range(HEADS))
RET_LOG_G = tuple(float(np.log(np.float32(g))) for g in RET_G)
RET_STATE_DECAY = tuple(float(np.exp(np.float32(lg) * np.float32(CHUNK))) for lg in RET_LOG_G)

_NT = (((1,), (1,)), ((), ()))
_TN = (((0,), (0,)), ((), ()))


def _params(*sem, vmem=None):
    return pltpu.CompilerParams(dimension_semantics=sem, vmem_limit_bytes=vmem or VMEM_LIMIT)


def _sigmoid(x):
    return 0.5 * jnp.tanh(0.5 * x) + 0.5


def _log_sigmoid(x):
    return jnp.minimum(x, 0.0) - jnp.log(1.0 + jnp.exp(-jnp.abs(x)))


def _layer_norm(y, g, b):
    mu = jnp.mean(y, axis=-1, keepdims=True)
    yc = y - mu
    var = jnp.mean(jnp.square(yc), axis=-1, keepdims=True)
    return yc * lax.rsqrt(var + LN_EPS) * g + b


def _row_splits(tm):
    if tm < 2 * MXU_ROWS:
        return [slice(0, tm)]
    return [slice(0, tm // 2), slice(tm // 2, tm)]


def _head_norm(o):
    mu = jnp.mean(o, axis=-1, keepdims=True)
    oc = o - mu
    var = jnp.mean(jnp.square(oc), axis=-1, keepdims=True)
    return oc * lax.rsqrt(var + GN_EPS)


RG_GROUP, MO_GROUP = 3, 7


def _inproj_kernel(x_ref, wt_ref, c_ref, s_ref, o_ref, *wb_out):
    j = pl.program_id(0)
    if wb_out:
        wb_ref, = wb_out
        wb_ref[...] = wt_ref[...].astype(BF16)
    else:
        wb_ref = wt_ref

    is_gate = (j == RG_GROUP) | (j == MO_GROUP)
    pieces = [(h, rows) for h in range(HEADS) for rows in _row_splits(x_ref.shape[0])]

    def piece_acc(h, rows):
        return lax.dot_general(x_ref[rows, :], wb_ref[h * HEAD_DIM:(h + 1) * HEAD_DIM, :], _NT,
                               preferred_element_type=F32)

    @pl.when(is_gate)
    def _():
        for h, rows in pieces:
            acc = piece_acc(h, rows)
            sig = _sigmoid(acc)
            o_ref[rows, h * HEAD_DIM:(h + 1) * HEAD_DIM] = jnp.where(
                j == RG_GROUP, acc * sig, sig).astype(o_ref.dtype)

    @pl.when(jnp.logical_not(is_gate))
    def _():
        rotary = j <= 1
        scale = jnp.where((j == 1) | (j == MO_GROUP - 2), QK_SCALE, 1.0).astype(F32)
        for h, rows in pieces:
            lo = h * HEAD_DIM
            c = jnp.where(rotary, c_ref[rows, :], 1.0) * scale
            s = jnp.where(rotary, s_ref[rows, :], 0.0) * scale
            acc = piece_acc(h, rows)
            x1 = acc[:, :HALF]
            x2 = acc[:, HALF:]
            o_ref[rows, lo:lo + HALF] = (x1 * c - x2 * s).astype(o_ref.dtype)
            o_ref[rows, lo + HALF:lo + HEAD_DIM] = (x1 * s + x2 * c).astype(o_ref.dtype)


def _inproj(xb, wt, cos, sin, *, tm):
    m = xb.shape[0]
    n_pos = cos.shape[0] // tm
    cast_w = wt.dtype != BF16
    assert not cast_w or m == tm
    wspec = pl.BlockSpec((WIDTH, D_MODEL), lambda j, i: (j, 0))
    ospec = pl.BlockSpec((tm, WIDTH), lambda j, i: (i, j))
    oshape = jax.ShapeDtypeStruct((m, 8 * WIDTH), BF16)
    tspec = pl.BlockSpec((tm, HALF), lambda j, i: (i % n_pos, 0))

    return pl.pallas_call(
        _inproj_kernel,
        grid=(8, m // tm),
        in_specs=[pl.BlockSpec((tm, D_MODEL), lambda j, i: (i, 0)), wspec, tspec, tspec],
        out_specs=[ospec, wspec] if cast_w else ospec,
        out_shape=[oshape, jax.ShapeDtypeStruct((8 * WIDTH, D_MODEL), BF16)] if cast_w else oshape,
        compiler_params=_params("arbitrary", "arbitrary"),
        name="inproj_cast" if cast_w else "inproj",
    )(xb, wt, cos, sin)


def _mgate_kernel(x_ref, wt_ref, b_ref, xb_ref, g_ref, gt_ref):
    xb = x_ref[...].astype(BF16)
    xb_ref[...] = xb
    wt = jnp.concatenate([wt_ref[...], jnp.zeros((LANES - 2 * HEADS, D_MODEL), F32)], axis=0)
    g = lax.dot_general(xb, wt.astype(BF16), _NT, preferred_element_type=F32) + b_ref[...]
    lane = lax.broadcasted_iota(jnp.int32, g.shape, 1)
    g = jnp.where((lane >= HEADS) & (lane < 2 * HEADS), _log_sigmoid(g), g)
    r = lax.broadcasted_iota(jnp.int32, (CHUNK, CHUNK), 0)
    s = lax.broadcasted_iota(jnp.int32, (CHUNK, CHUNK), 1)
    tril = (r >= s).astype(BF16)
    g1 = g.astype(BF16)
    res = g - g1.astype(F32)
    g2 = res.astype(BF16)
    g3 = (res - g2.astype(F32)).astype(BF16)
    parts = []
    for c in range(g.shape[0] // CHUNK):
        sl = slice(c * CHUNK, (c + 1) * CHUNK)
        parts.append(jnp.dot(tril, g1[sl], preferred_element_type=F32)
                     + jnp.dot(tril, g2[sl], preferred_element_type=F32)
                     + jnp.dot(tril, g3[sl], preferred_element_type=F32))
    cs = parts[0] if len(parts) == 1 else jnp.concatenate(parts, axis=0)
    b = pltpu.roll(cs, HEADS, 1)
    u = pltpu.roll(g, 3 * HEADS, 1) - pltpu.roll(cs, 2 * HEADS, 1)
    g = jnp.where((lane >= 2 * HEADS) & (lane < 3 * HEADS), b, g)
    g = jnp.where((lane >= 3 * HEADS) & (lane < 4 * HEADS), u, g)
    g_ref[...] = g
    gt_ref[...] = g.T[0:4 * HEADS, :]


def _mgate(x, w_in_t, b8, *, tm):
    m = x.shape[0]
    gate_rows = 8 * WIDTH // (2 * HEADS)
    return pl.pallas_call(
        _mgate_kernel,
        grid=(m // tm,),
        in_specs=[
            pl.BlockSpec((tm, D_MODEL), lambda i: (i, 0)),
            pl.BlockSpec((2 * HEADS, D_MODEL), lambda i: (gate_rows, 0)),
            pl.BlockSpec((1, LANES), lambda i: (0, 0)),
        ],
        out_specs=[
            pl.BlockSpec((tm, D_MODEL), lambda i: (i, 0)),
            pl.BlockSpec((tm, LANES), lambda i: (i, 0)),
            pl.BlockSpec((4 * HEADS, tm), lambda i: (0, i)),
        ],
        out_shape=[
            jax.ShapeDtypeStruct((m, D_MODEL), BF16),
            jax.ShapeDtypeStruct((m, LANES), F32),
            jax.ShapeDtypeStruct((4 * HEADS, m), F32),
        ],
        compiler_params=_params("arbitrary"),
        name="mlstm_gates",
    )(x, w_in_t, b8)


def _mixer_chunk(rows, causal, qkv_ref, g_ref, gt_ref, intra_ref, dec_ref, kdec_ref, rgn_ref, mgn_ref,
                 mix_ref, s_ref, c_ref, n_ref, m_scr):
    def head(group, h):
        lo = group * WIDTH + h * HEAD_DIM
        return qkv_ref[rows, lo:lo + HEAD_DIM]

    gw = []
    for h in range(HEADS):
        b_col = g_ref[rows, 2 * HEADS + h:2 * HEADS + h + 1]
        u_row = gt_ref[3 * HEADS + h:3 * HEADS + h + 1, rows]
        m_prev = m_scr[h:h + 1, 0:1]
        dlog = jnp.where(causal, b_col + u_row, -jnp.inf)
        inter = b_col + m_prev
        mt = jnp.maximum(inter, jnp.max(dlog, axis=1, keepdims=True))
        m_new = mt[CHUNK - 1:CHUNK, :]
        b_last = b_col[CHUNK - 1:CHUNK, :]
        gw.append(dict(
            dw=jnp.exp(dlog - mt), iw=jnp.exp(inter - mt), emt=jnp.exp(-mt),
            sw=jnp.exp(b_last - m_new + u_row), sd=jnp.exp(b_last + m_prev - m_new)))
        m_scr[h:h + 1, :] = jnp.broadcast_to(m_new, (1, LANES))

    st = []
    for h in range(HEADS):
        for kind, (qg, state_ref) in enumerate(((0, s_ref), (MO_GROUP - 3, c_ref))):
            q = head(qg, h)
            k = head(qg + 1, h)
            old = state_ref[0, 0, h]
            kt = k.T
            st.append(dict(
                h=h, kind=kind, q=q, k=k, kt=kt, v=head(qg + 2, h), old=old,
                sc=jnp.dot(q, kt, preferred_element_type=F32),
                inter=jnp.dot(q, old.astype(BF16), preferred_element_type=F32)))

    for e in st:
        h = e["h"]
        if e["kind"] == 0:
            kd = (e["kt"].astype(F32) * kdec_ref[h:h + 1, :]).astype(BF16)
            s_ref[0, 0, h] = (e["old"] * RET_STATE_DECAY[h]
                              + jnp.dot(kd, e["v"], preferred_element_type=F32))
        else:
            w = gw[h]
            ksw = (e["kt"].astype(F32) * w["sw"]).astype(BF16)
            c_ref[0, 0, h] = w["sd"] * e["old"] + jnp.dot(ksw, e["v"], preferred_element_type=F32)
            n_old = n_ref[0, 0, h:h + 1, :]
            e["qn"] = jnp.sum(e["q"].astype(F32) * n_old, axis=1, keepdims=True)
            sw8 = jnp.broadcast_to(w["sw"], (SUBLANES, CHUNK)).astype(BF16)
            n_ref[0, 0, h:h + 1, :] = (w["sd"] * n_old
                                       + jnp.dot(sw8, e["k"], preferred_element_type=F32)[0:1, :])

    for e in st:
        h = e["h"]
        e["sc"] = e["sc"] * (intra_ref[h] if e["kind"] == 0 else gw[h]["dw"])
        e["pv"] = jnp.dot(e["sc"].astype(BF16), e["v"], preferred_element_type=F32)

    for e in st:
        e["rows"] = rows
        if e["kind"]:
            e["iw"] = gw[e["h"]]["iw"]
            e["emt"] = gw[e["h"]]["emt"]
    return st


def _mixer_tail(streams, qkv_ref, dec_ref, rgn_ref, mgn_ref, mix_ref):
    dens = [jnp.sum(e["sc"], axis=1, keepdims=True) if e["kind"] else None for e in streams]
    outs = []
    for e, den in zip(streams, dens):
        if e["kind"] == 0:
            outs.append(e["pv"] + e["inter"] * dec_ref[e["h"]])
        else:
            num = e["pv"] + e["iw"] * e["inter"]
            den = den + e["iw"] * e["qn"]
            outs.append(num * (1.0 / jnp.maximum(jnp.abs(den), e["emt"])))
    mus = [jnp.mean(o, axis=-1, keepdims=True) for o in outs]
    cen = [o - mu for o, mu in zip(outs, mus)]
    var = [jnp.mean(jnp.square(oc), axis=-1, keepdims=True) for oc in cen]
    for e, oc, v in zip(streams, cen, var):
        rows = e["rows"]
        lo = e["h"] * HEAD_DIM
        gate_lo = (MO_GROUP if e["kind"] else RG_GROUP) * WIDTH + lo
        gain_ref = mgn_ref if e["kind"] else rgn_ref
        out_lo = e["kind"] * WIDTH + lo
        y = (oc * lax.rsqrt(v + GN_EPS) * gain_ref[:, lo:lo + HEAD_DIM]
             * qkv_ref[rows, gate_lo:gate_lo + HEAD_DIM].astype(F32))
        mix_ref[rows, out_lo:out_lo + HEAD_DIM] = y.astype(mix_ref.dtype)


def _mixers_kernel(scal_ref, rows_ref, s0_ref, c0_ref, n0_ref,
                   qkv_ref, g_ref, gt_ref, intra_ref, dec_ref, kdec_ref, rgn_ref, mgn_ref,
                   smix_ref, ss_ref, sc_ref, sn_ref,
                   mix_ref, s_ref, c_ref, n_ref, m_ref, m_scr):
    c = pl.program_id(1)
    _sample_tokens(pl.program_id(0) * pl.num_programs(1) + c, scal_ref, rows_ref, s0_ref, c0_ref,
                   n0_ref, rgn_ref, mgn_ref, smix_ref, ss_ref, sc_ref, sn_ref)

    @pl.when(c == 0)
    def _():
        s_ref[...] = jnp.zeros_like(s_ref)
        c_ref[...] = jnp.zeros_like(c_ref)
        n_ref[...] = jnp.zeros_like(n_ref)
        m_scr[...] = jnp.zeros_like(m_scr)

    ti = lax.broadcasted_iota(jnp.int32, (CHUNK, CHUNK), 0)
    si = lax.broadcasted_iota(jnp.int32, (CHUNK, CHUNK), 1)
    causal = ti >= si

    streams = []
    for ci in range(MIXER_CHUNKS_PER_STEP):
        streams += _mixer_chunk(slice(ci * CHUNK, (ci + 1) * CHUNK), causal, qkv_ref, g_ref, gt_ref,
                                intra_ref, dec_ref, kdec_ref, rgn_ref, mgn_ref, mix_ref, s_ref, c_ref,
                                n_ref, m_scr)
    _mixer_tail(streams, qkv_ref, dec_ref, rgn_ref, mgn_ref, mix_ref)

    @pl.when(c == pl.num_programs(1) - 1)
    def _():
        m_ref[0] = m_scr[...]


def _mixers(scal, rows_s, s0, c0, n0, qkv, g_tok, g_t, intra, qdec, kdec, rgn, mgn, *, batch, seq):
    rows = MIXER_CHUNKS_PER_STEP * CHUNK
    nc = seq // rows
    nb = rows_s.shape[0]
    tb = SAMPLE_TOKENS_PER_STEP
    assert nb == batch * nc * tb
    row = lambda b, c: (b * nc + c, 0)
    const2 = lambda b, c: (0, 0)
    state = pl.BlockSpec((1, 1, HEADS, HEAD_DIM, HEAD_DIM), lambda b, c: (0, b, 0, 0, 0))
    sstate = pl.BlockSpec((1, tb, HEADS, HEAD_DIM, HEAD_DIM), lambda b, c: (0, b * nc + c, 0, 0, 0))
    snorm = pl.BlockSpec((1, tb, HEADS, HEAD_DIM), lambda b, c: (0, b * nc + c, 0, 0))
    return pl.pallas_call(
        _mixers_kernel,
        grid=(batch, nc),
        in_specs=[
            pl.BlockSpec(memory_space=pltpu.SMEM),
            pl.BlockSpec((nb, 8 * WIDTH), const2, pipeline_mode=pl.Buffered(1)),
            sstate, sstate, snorm,
            pl.BlockSpec((rows, 8 * WIDTH), row),
            pl.BlockSpec((rows, LANES), row),
            pl.BlockSpec((4 * HEADS, rows), lambda b, c: (0, b * nc + c)),
            pl.BlockSpec((HEADS, CHUNK, CHUNK), lambda b, c: (0, 0, 0)),
            pl.BlockSpec((HEADS, CHUNK, HEAD_DIM), lambda b, c: (0, 0, 0)),
            pl.BlockSpec((HEADS, CHUNK), const2),
            pl.BlockSpec((1, WIDTH), const2),
            pl.BlockSpec((1, WIDTH), const2),
        ],
        out_specs=[
            pl.BlockSpec((nb, 2 * WIDTH), const2), sstate, sstate, snorm,
            pl.BlockSpec((rows, 2 * WIDTH), row),
            state,
            state,
            pl.BlockSpec((1, 1, HEADS, HEAD_DIM), lambda b, c: (0, b, 0, 0)),
            pl.BlockSpec((1, SUBLANES, LANES), lambda b, c: (b, 0, 0)),
        ],
        out_shape=[
            jax.ShapeDtypeStruct((nb, 2 * WIDTH), F32),
            jax.ShapeDtypeStruct(s0.shape, F32),
            jax.ShapeDtypeStruct(c0.shape, F32),
            jax.ShapeDtypeStruct(n0.shape, F32),
            jax.ShapeDtypeStruct((batch * seq, 2 * WIDTH), BF16),
            jax.ShapeDtypeStruct((1, batch, HEADS, HEAD_DIM, HEAD_DIM), F32),
            jax.ShapeDtypeStruct((1, batch, HEADS, HEAD_DIM, HEAD_DIM), F32),
            jax.ShapeDtypeStruct((1, batch, HEADS, HEAD_DIM), F32),
            jax.ShapeDtypeStruct((batch, SUBLANES, LANES), F32),
        ],
        scratch_shapes=[pltpu.VMEM((SUBLANES, LANES), F32)],
        compiler_params=_params("arbitrary", "arbitrary", vmem=MIXERS_VMEM_LIMIT),
        name="mixers",
    )(scal, rows_s, s0, c0, n0, qkv, g_tok, g_t, intra, qdec, kdec, rgn, mgn)


def _sample_gate_kernel(g_ref, m0_ref, mt_ref, dw_ref, iw_ref, emt_ref):
    ig = g_ref[...]
    lf = pltpu.roll(ig, LANES - HEADS, 1)
    inter = lf + m0_ref[...]
    mt = jnp.maximum(inter, ig)
    mt_ref[...] = mt
    dw_ref[...] = jnp.exp(ig - mt)
    iw_ref[...] = jnp.exp(inter - mt)
    emt_ref[...] = jnp.exp(-mt)


def _sample_gates(g_tok, m0_pad):
    m = g_tok.shape[0]
    spec = pl.BlockSpec((m, LANES), lambda: (0, 0))
    return pl.pallas_call(
        _sample_gate_kernel,
        in_specs=[spec, spec],
        out_specs=[spec] * 4,
        out_shape=[jax.ShapeDtypeStruct((m, LANES), F32)] * 4,
        name="sample_gates",
    )(g_tok, m0_pad)


def _sample_tokens(step, scal_ref, rows_ref, s0_ref, c0_ref, n0_ref,
                   rgn_ref, mgn_ref, mix_ref, s_ref, c_ref, n_ref):
    first = lax.broadcasted_iota(jnp.int32, (4 * HEADS, HEAD_DIM), 0) == 0
    for t in range(SAMPLE_TOKENS_PER_STEP):
        b = step * SAMPLE_TOKENS_PER_STEP + t

        def row(group, h):
            lo = group * WIDTH + h * HEAD_DIM
            return rows_ref[pl.ds(b, 1), lo:lo + HEAD_DIM]

        def outer(k, v):
            kp = jnp.where(first, jnp.broadcast_to(k, first.shape), 0.0).astype(BF16)
            vp = jnp.where(first, jnp.broadcast_to(v, first.shape), 0.0).astype(BF16)
            return lax.dot_general(kp, vp, _TN, preferred_element_type=F32)

        def apply(q_row, state):
            q8 = jnp.broadcast_to(q_row, (SUBLANES, HEAD_DIM)).astype(BF16)
            return jnp.dot(q8, state.astype(BF16), preferred_element_type=F32)[0:1, :]

        dw = [scal_ref[b, HEADS + h] for h in range(HEADS)]
        iw = [scal_ref[b, 2 * HEADS + h] for h in range(HEADS)]
        emt = [scal_ref[b, 3 * HEADS + h] for h in range(HEADS)]
        kv_s = [outer(row(1, h), row(2, h)) for h in range(HEADS)]
        kv_c = [outer(row(5, h), dw[h] * row(6, h)) for h in range(HEADS)]
        s_new, c_new = [], []
        for h in range(HEADS):
            s_new.append(RET_G[h] * s0_ref[0, t, h] + kv_s[h])
            s_ref[0, t, h] = s_new[h]
            c_new.append(iw[h] * c0_ref[0, t, h] + kv_c[h])
            c_ref[0, t, h] = c_new[h]
        o = [apply(row(0, h), s_new[h]) for h in range(HEADS)]
        num = [apply(row(4, h), c_new[h]) for h in range(HEADS)]
        for h in range(HEADS):
            lo = h * HEAD_DIM
            mix_ref[pl.ds(b, 1), lo:lo + HEAD_DIM] = (_head_norm(o[h]) * rgn_ref[:, lo:lo + HEAD_DIM]
                                               * row(RG_GROUP, h))
            n_new = iw[h] * n0_ref[0, t, h:h + 1, :] + dw[h] * row(5, h)
            n_ref[0, t, h:h + 1, :] = n_new
            den = jnp.sum(row(4, h) * n_new, axis=1, keepdims=True)
            hid = num[h] * (1.0 / jnp.maximum(jnp.abs(den), emt[h]))
            mix_ref[pl.ds(b, 1), WIDTH + lo:WIDTH + lo + HEAD_DIM] = (
                _head_norm(hid) * mgn_ref[:, lo:lo + HEAD_DIM] * row(MO_GROUP, h))


def _outproj_kernel(a_ref, w_ref, x_ref, g_ref, b_ref, x1_ref, x1b_ref, *wb_out):
    if wb_out:
        wb_out[0][...] = w_ref[...].astype(BF16)
        w_ref = wb_out[0]
    tm = a_ref.shape[0]
    for rows in _row_splits(tm):
        mix = jnp.dot(a_ref[rows, :].astype(BF16), w_ref[...], preferred_element_type=F32)
        x1 = _layer_norm(ALPHA * x_ref[rows, :] + mix, g_ref[...], b_ref[...])
        x1_ref[rows, :] = x1
        x1b_ref[rows, :] = x1.astype(BF16)


def _outproj(a, w, x, g, b, *, tm):
    m = a.shape[0]
    cast_w = w.dtype != BF16
    assert not cast_w or m == tm
    row = pl.BlockSpec((tm, D_MODEL), lambda i: (i, 0))
    vec = pl.BlockSpec((1, D_MODEL), lambda i: (0, 0))
    out_specs = [row, row]
    out_shape = [jax.ShapeDtypeStruct((m, D_MODEL), F32), jax.ShapeDtypeStruct((m, D_MODEL), BF16)]
    if cast_w:
        out_specs.append(pl.BlockSpec((D_MODEL, D_MODEL), lambda i: (0, 0)))
        out_shape.append(jax.ShapeDtypeStruct((D_MODEL, D_MODEL), BF16))
    return pl.pallas_call(
        _outproj_kernel,
        grid=(m // tm,),
        in_specs=[row, pl.BlockSpec((D_MODEL, D_MODEL), lambda i: (0, 0), pipeline_mode=pl.Buffered(1)),
                  row, vec, vec],
        out_specs=out_specs,
        out_shape=out_shape,
        compiler_params=_params("arbitrary"),
        name="outproj_ln1_cast" if cast_w else "outproj_ln1",
    )(a, w, x, g, b)


def _ffn_kernel(x1b_ref, w1_ref, w2_ref, o_ref, *wb_out):
    @pl.when(pl.program_id(1) == 0)
    def _():
        o_ref[...] = jnp.zeros_like(o_ref)

    w1 = w1_ref[...]
    w2 = w2_ref[...]
    if wb_out:
        w1 = w1.astype(BF16)
        w2 = w2.astype(BF16)
        wb_out[0][...] = w1
        wb_out[1][...] = w2
    hid = jnp.dot(x1b_ref[...], w1, preferred_element_type=F32)
    hid = jnp.square(jnp.maximum(hid, 0.0)).astype(BF16)
    o_ref[...] += jnp.dot(hid, w2, preferred_element_type=F32)


def _ffn(x1b, w1, w2, *, tm, tf):
    m = x1b.shape[0]
    cast_w = w1.dtype != BF16
    assert not cast_w or m == tm
    row = pl.BlockSpec((tm, D_MODEL), lambda i, f: (i, 0))
    w1spec = pl.BlockSpec((D_MODEL, tf), lambda i, f: (0, f))
    w2spec = pl.BlockSpec((tf, D_MODEL), lambda i, f: (f, 0))
    oshape = jax.ShapeDtypeStruct((m, D_MODEL), F32)
    return pl.pallas_call(
        _ffn_kernel,
        grid=(m // tm, D_FF // tf),
        in_specs=[row, w1spec, w2spec],
        out_specs=[row, w1spec, w2spec] if cast_w else row,
        out_shape=[oshape, jax.ShapeDtypeStruct(w1.shape, BF16),
                   jax.ShapeDtypeStruct(w2.shape, BF16)] if cast_w else oshape,
        compiler_params=_params("arbitrary", "arbitrary"),
        name="ffn_cast" if cast_w else "ffn",
    )(x1b, w1, w2)


def _pe_ln2_kernel(x1_ref, x1b_ref, p_ref, ff_ref, wpe_ref, wg_ref, g_ref, b_ref, o_ref, *wb_out):
    if wb_out:
        wb_out[0][...] = wpe_ref[...].astype(BF16)
        wb_out[1][...] = wg_ref[...].astype(BF16)
        wpe_ref, wg_ref = wb_out
    for rows in _row_splits(x1_ref.shape[0]):
        gate = _sigmoid(jnp.dot(x1b_ref[rows, :], wg_ref[...], preferred_element_type=F32))
        pe = jnp.dot(p_ref[rows, :].astype(BF16), wpe_ref[...], preferred_element_type=F32)
        y = ALPHA * x1_ref[rows, :] + ff_ref[rows, :] + pe * gate
        o_ref[rows, :] = _layer_norm(y, g_ref[...], b_ref[...])


def _pe_ln2(x1, x1b, p, ff, wpe, wg, g, b, *, tm):
    m = x1.shape[0]
    cast_w = wg.dtype != BF16
    assert not cast_w or m == tm
    row = pl.BlockSpec((tm, D_MODEL), lambda i: (i, 0))
    vec = pl.BlockSpec((1, D_MODEL), lambda i: (0, 0))
    oshape = jax.ShapeDtypeStruct((m, D_MODEL), F32)
    wb_specs = [pl.BlockSpec((PLE_DIM, D_MODEL), lambda i: (0, 0)),
                pl.BlockSpec((D_MODEL, D_MODEL), lambda i: (0, 0))]
    wb_shapes = [jax.ShapeDtypeStruct(wpe.shape, BF16), jax.ShapeDtypeStruct(wg.shape, BF16)]
    return pl.pallas_call(
        _pe_ln2_kernel,
        grid=(m // tm,),
        in_specs=[row, row, pl.BlockSpec((tm, PLE_DIM), lambda i: (i, 0)), row,
                  pl.BlockSpec((PLE_DIM, D_MODEL), lambda i: (0, 0), pipeline_mode=pl.Buffered(1)),
                  pl.BlockSpec((D_MODEL, D_MODEL), lambda i: (0, 0), pipeline_mode=pl.Buffered(1)),
                  vec, vec],
        out_specs=[row] + wb_specs if cast_w else row,
        out_shape=[oshape] + wb_shapes if cast_w else oshape,
        compiler_params=_params("arbitrary"),
        name="pe_ln2_cast" if cast_w else "pe_ln2",
    )(x1, x1b, p, ff, wpe, wg, g, b)


def _rope_tables(pos):
    inv = ROPE_BASE ** (-jnp.arange(HALF, dtype=F32) * 2.0 / HEAD_DIM)
    ang = pos[:, None] * inv[None, :]
    return jnp.cos(ang), jnp.sin(ang)


def _decay_tables():
    lg = jnp.log(1.0 - 2.0 ** (-5.0 - jnp.arange(HEADS, dtype=F32)))
    t = jnp.arange(CHUNK, dtype=F32)
    causal = t[:, None] >= t[None, :]
    intra = jnp.exp(jnp.where(causal, (t[:, None] - t[None, :]) * lg[:, None, None], -jnp.inf))
    q_decay = jnp.exp(lg[:, None] * (t + 1.0))
    k_decay = jnp.exp(lg[:, None] * (CHUNK - 1.0 - t))
    return intra, jnp.broadcast_to(q_decay[:, :, None], (HEADS, CHUNK, HEAD_DIM)), k_decay


def _tail(x, mixed, p, w, *, tm, tm_ffn, tf):
    casts = {}
    res = _outproj(mixed, w["out"], x, w["ln1_g"], w["ln1_b"], tm=tm)
    x1, x1b = res[0], res[1]
    if len(res) > 2:
        casts["out"] = res[2]
    res = _ffn(x1b, w["ff1"], w["ff2"], tm=tm_ffn, tf=tf)
    if isinstance(res, (list, tuple)):
        ff, casts["ff1"], casts["ff2"] = res
    else:
        ff = res
    res = _pe_ln2(x1, x1b, p, ff, w["pe"], w["pe_gate"], w["ln2_g"], w["ln2_b"], tm=tm)
    if isinstance(res, (list, tuple)):
        y, casts["pe"], casts["pe_gate"] = res
    else:
        y = res
    return y, casts


def kernel(x_prompt, x_sample, state_ret, state_mlstm_C, state_mlstm_n, state_mlstm_m, p_prompt, p_sample, w_in, b_gate, ret_gn_w, mlstm_gn_w, w_out, ln1_g, ln1_b, w_ff1, w_ff2, w_pe, w_pe_gate, ln2_g, ln2_b):
    batch, seq, _ = x_prompt.shape
    nb = x_sample.shape[0]

    w_in_t = w_in[0].T
    b8 = jnp.pad(b_gate[0].astype(F32), (0, LANES - 2 * HEADS)).reshape(1, LANES)
    w = {
        "out": w_out[0], "ff1": w_ff1[0], "ff2": w_ff2[0], "pe": w_pe[0], "pe_gate": w_pe_gate[0],
        "ln1_g": ln1_g[0].reshape(1, D_MODEL), "ln1_b": ln1_b[0].reshape(1, D_MODEL),
        "ln2_g": ln2_g[0].reshape(1, D_MODEL), "ln2_b": ln2_b[0].reshape(1, D_MODEL),
    }
    rgn = ret_gn_w[0].reshape(1, WIDTH)
    mgn = mlstm_gn_w[0].reshape(1, WIDTH)
    intra, qdec, kdec = _decay_tables()

    xs = x_sample.reshape(nb, D_MODEL)
    xsb, gs_tok, _ = _mgate(xs, w_in_t, b8, tm=nb)
    ctab, stab = _rope_tables(jnp.full((nb,), PAST_LEN, dtype=F32))
    proj_s, w_in_b = _inproj(xsb, w_in_t, ctab, stab, tm=nb)
    proj_s = proj_s.astype(F32)
    m0 = jnp.pad(state_mlstm_m[0], ((0, 0), (0, LANES - HEADS)))
    mt, dw, iw, emt = _sample_gates(gs_tok, m0)
    scal = jnp.concatenate([mt[:, :HEADS], dw[:, :HEADS], iw[:, :HEADS], emt[:, :HEADS]], axis=1)
    m_s = mt[:, :HEADS].reshape(1, nb, HEADS)

    xp = x_prompt.reshape(batch * seq, D_MODEL)
    xpb, g_tok, g_t = _mgate(xp, w_in_t, b8, tm=GATE_TM)
    ctab, stab = _rope_tables(jnp.arange(seq, dtype=F32))
    proj_p = _inproj(xpb, w_in_b, ctab, stab, tm=INPROJ_TM)
    mixed_s, s_s, c_s, n_s, mixed_p, s_p, c_p, n_p, m_rows = _mixers(
        scal, proj_s, state_ret, state_mlstm_C, state_mlstm_n,
        proj_p, g_tok, g_t, intra, qdec, kdec, rgn, mgn, batch=batch, seq=seq)
    m_p = m_rows[:, :HEADS, 0].reshape(1, batch, HEADS)

    y_s, w_bf16 = _tail(xs, mixed_s, p_sample[0].reshape(nb, PLE_DIM), w,
                        tm=nb, tm_ffn=nb, tf=CAST_FFN_TF)
    w = {**w, **w_bf16}
    y_p, _ = _tail(xp, mixed_p, p_prompt[0].reshape(batch * seq, PLE_DIM), w,
                   tm=TAIL_TM, tm_ffn=FFN_TM, tf=FFN_TF)

    return (y_p.reshape(batch, seq, D_MODEL), y_s.reshape(nb, 1, D_MODEL),
            s_p, c_p, n_p, m_p, s_s, c_s, n_s, m_s)
```

```python
import jax
import jax.numpy as jnp
import numpy as np
from jax import lax
from jax.experimental import pallas as pl
from jax.experimental.pallas import tpu as pltpu

F32 = jnp.float32
BF16 = jnp.bfloat16

D_MODEL = 2048
HEADS = 4
HEAD_DIM = 256
HALF = HEAD_DIM // 2
WIDTH = HEADS * HEAD_DIM
D_FF = 4 * D_MODEL
PLE_DIM = 256
CHUNK = 128
PAST_LEN = 16384
ROPE_BASE = 10000.0
LN_EPS = 1e-5
GN_EPS = 1e-6
DEPTH = 1
ALPHA = (2 * DEPTH) ** 0.25
QK_SCALE = HEAD_DIM ** -0.5
LANES = 128
SUBLANES = 8
MXU_ROWS = 256
VMEM_LIMIT = 56 * 1024 * 1024
MIXER_CHUNKS_PER_STEP = 2
GATE_TM = 1024
INPROJ_TM = 512
TAIL_TM = 512
FFN_TM = 1024
FFN_TF = 1024
CAST_FFN_TF = 512

RET_G = tuple(1.0 - 2.0 ** (-5.0 - h) for h in range(HEADS))
RET_LOG_G = tuple(float(np.log(np.float32(g))) for g in RET_G)
RET_STATE_DECAY = tuple(float(np.exp(np.float32(lg) * np.float32(CHUNK))) for lg in RET_LOG_G)

_NT = (((1,), (1,)), ((), ()))
_TN = (((0,), (0,)), ((), ()))


def _params(*sem):
    return pltpu.CompilerParams(dimension_semantics=sem, vmem_limit_bytes=VMEM_LIMIT)


def _sigmoid(x):
    return 0.5 * jnp.tanh(0.5 * x) + 0.5


def _log_sigmoid(x):
    return jnp.minimum(x, 0.0) - jnp.log(1.0 + jnp.exp(-jnp.abs(x)))


def _layer_norm(y, g, b):
    mu = jnp.mean(y, axis=-1, keepdims=True)
    yc = y - mu
    var = jnp.mean(jnp.square(yc), axis=-1, keepdims=True)
    return yc * lax.rsqrt(var + LN_EPS) * g + b


def _row_splits(tm, min_rows=MXU_ROWS):
    if tm < 2 * min_rows:
        return [slice(0, tm)]
    return [slice(0, tm // 2), slice(tm // 2, tm)]


def _head_norm(o):
    mu = jnp.mean(o, axis=-1, keepdims=True)
    oc = o - mu
    var = jnp.mean(jnp.square(oc), axis=-1, keepdims=True)
    return oc * lax.rsqrt(var + GN_EPS)


RG_GROUP, MO_GROUP = 3, 7


def _inproj_body(x_ref, wb_ref, c_ref, s_ref, o_ref, hosted=()):
    j = pl.program_id(0)
    is_gate = (j == RG_GROUP) | (j == MO_GROUP)
    pieces = [(h, rows) for h in range(HEADS) for rows in _row_splits(x_ref.shape[0], 2 * MXU_ROWS)]

    def piece_acc(h, rows):
        return lax.dot_general(x_ref[rows, :], wb_ref[h * HEAD_DIM:(h + 1) * HEAD_DIM, :], _NT,
                               preferred_element_type=F32)

    def gate_piece(h, rows):
        acc = piece_acc(h, rows)
        sig = _sigmoid(acc)
        o_ref[rows, h * HEAD_DIM:(h + 1) * HEAD_DIM] = jnp.where(
            j == RG_GROUP, acc * sig, sig).astype(o_ref.dtype)

    def qkv_piece(h, rows):
        rotary = j <= 1
        scale = jnp.where((j == 1) | (j == MO_GROUP - 2), QK_SCALE, 1.0).astype(F32)
        lo = h * HEAD_DIM
        c = jnp.where(rotary, c_ref[rows, :], 1.0) * scale
        s = jnp.where(rotary, s_ref[rows, :], 0.0) * scale
        acc = piece_acc(h, rows)
        x1 = acc[:, :HALF]
        x2 = acc[:, HALF:]
        o_ref[rows, lo:lo + HALF] = (x1 * c - x2 * s).astype(o_ref.dtype)
        o_ref[rows, lo + HALF:lo + HEAD_DIM] = (x1 * s + x2 * c).astype(o_ref.dtype)

    def run(piece):
        for n, (h, rows) in enumerate(pieces):
            piece(h, rows)
            if n < len(hosted):
                hosted[n]()

    pl.when(is_gate)(lambda: run(gate_piece))
    pl.when(jnp.logical_not(is_gate))(lambda: run(qkv_piece))


def _inproj_cast_kernel(x_ref, wt_ref, c_ref, s_ref, o_ref, wb_ref):
    wb_ref[...] = wt_ref[...].astype(BF16)
    _inproj_body(x_ref, wb_ref, c_ref, s_ref, o_ref)


def _inproj_host_kernel(x_ref, wt_ref, c_ref, s_ref, scal_ref, rows_ref, s0_ref, c0_ref, n0_ref,
                        rgn_ref, mgn_ref, o_ref, smix_ref, ss_ref, sc_ref, sn_ref):
    token = pl.program_id(0) * pl.num_programs(1) + pl.program_id(1)
    hosted = _sample_token_stages(token, scal_ref, rows_ref, s0_ref, c0_ref, n0_ref, rgn_ref, mgn_ref,
                                  smix_ref, ss_ref, sc_ref, sn_ref)
    _inproj_body(x_ref, wt_ref, c_ref, s_ref, o_ref, hosted)


def _inproj(xb, wt, cos, sin, *, tm, sample=None):
    m = xb.shape[0]
    n_i = m // tm
    n_pos = cos.shape[0] // tm
    wspec = pl.BlockSpec((WIDTH, D_MODEL), lambda j, i: (j, 0))
    ospec = pl.BlockSpec((tm, WIDTH), lambda j, i: (i, j))
    oshape = jax.ShapeDtypeStruct((m, 8 * WIDTH), BF16)
    tspec = pl.BlockSpec((tm, HALF), lambda j, i: (i % n_pos, 0))
    in_specs = [pl.BlockSpec((tm, D_MODEL), lambda j, i: (i, 0)), wspec, tspec, tspec]
    if wt.dtype != BF16:
        assert n_i == 1 and sample is None
        return pl.pallas_call(
            _inproj_cast_kernel,
            grid=(8, 1),
            in_specs=in_specs,
            out_specs=[ospec, wspec],
            out_shape=[oshape, jax.ShapeDtypeStruct((8 * WIDTH, D_MODEL), BF16)],
            compiler_params=_params("arbitrary", "arbitrary"),
            name="inproj_cast",
        )(xb, wt, cos, sin)
    scal, rows_s, s0, c0, n0, rgn, mgn = sample
    nb = rows_s.shape[0]
    assert nb == 8 * n_i
    const2 = lambda j, i: (0, 0)
    sstate = pl.BlockSpec((1, 1, HEADS, HEAD_DIM, HEAD_DIM), lambda j, i: (0, j * n_i + i, 0, 0, 0))
    snorm = pl.BlockSpec((1, 1, HEADS, HEAD_DIM), lambda j, i: (0, j * n_i + i, 0, 0))
    return pl.pallas_call(
        _inproj_host_kernel,
        grid=(8, n_i),
        in_specs=in_specs + [
            pl.BlockSpec(memory_space=pltpu.SMEM),
            pl.BlockSpec((nb, 8 * WIDTH), const2, pipeline_mode=pl.Buffered(1)),
            sstate, sstate, snorm,
            pl.BlockSpec((1, WIDTH), const2),
            pl.BlockSpec((1, WIDTH), const2),
        ],
        out_specs=[ospec, pl.BlockSpec((nb, 2 * WIDTH), const2), sstate, sstate, snorm],
        out_shape=[oshape, jax.ShapeDtypeStruct((nb, 2 * WIDTH), F32),
                   jax.ShapeDtypeStruct(s0.shape, F32), jax.ShapeDtypeStruct(c0.shape, F32),
                   jax.ShapeDtypeStruct(n0.shape, F32)],
        compiler_params=_params("arbitrary", "arbitrary"),
        name="inproj",
    )(xb, wt, cos, sin, scal, rows_s, s0, c0, n0, rgn, mgn)


def _mgate_kernel(x_ref, wt_ref, b_ref, xb_ref, g_ref, gt_ref):
    xb = x_ref[...].astype(BF16)
    xb_ref[...] = xb
    wt = jnp.concatenate([wt_ref[...], jnp.zeros((LANES - 2 * HEADS, D_MODEL), F32)], axis=0)
    g = lax.dot_general(xb, wt.astype(BF16), _NT, preferred_element_type=F32) + b_ref[...]
    lane = lax.broadcasted_iota(jnp.int32, g.shape, 1)
    g = jnp.where((lane >= HEADS) & (lane < 2 * HEADS), _log_sigmoid(g), g)
    r = lax.broadcasted_iota(jnp.int32, (CHUNK, CHUNK), 0)
    s = lax.broadcasted_iota(jnp.int32, (CHUNK, CHUNK), 1)
    tril = (r >= s).astype(BF16)
    g1 = g.astype(BF16)
    res = g - g1.astype(F32)
    g2 = res.astype(BF16)
    g3 = (res - g2.astype(F32)).astype(BF16)
    parts = []
    for c in range(g.shape[0] // CHUNK):
        sl = slice(c * CHUNK, (c + 1) * CHUNK)
        parts.append(jnp.dot(tril, g1[sl], preferred_element_type=F32)
                     + jnp.dot(tril, g2[sl], preferred_element_type=F32)
                     + jnp.dot(tril, g3[sl], preferred_element_type=F32))
    cs = parts[0] if len(parts) == 1 else jnp.concatenate(parts, axis=0)
    b = pltpu.roll(cs, HEADS, 1)
    u = pltpu.roll(g, 3 * HEADS, 1) - pltpu.roll(cs, 2 * HEADS, 1)
    g = jnp.where((lane >= 2 * HEADS) & (lane < 3 * HEADS), b, g)
    g = jnp.where((lane >= 3 * HEADS) & (lane < 4 * HEADS), u, g)
    g_ref[...] = g
    gt_ref[...] = g.T[0:4 * HEADS, :]


def _mgate(x, w_in_t, b8, *, tm):
    m = x.shape[0]
    gate_rows = 8 * WIDTH // (2 * HEADS)
    return pl.pallas_call(
        _mgate_kernel,
        grid=(m // tm,),
        in_specs=[
            pl.BlockSpec((tm, D_MODEL), lambda i: (i, 0)),
            pl.BlockSpec((2 * HEADS, D_MODEL), lambda i: (gate_rows, 0)),
            pl.BlockSpec((1, LANES), lambda i: (0, 0)),
        ],
        out_specs=[
            pl.BlockSpec((tm, D_MODEL), lambda i: (i, 0)),
            pl.BlockSpec((tm, LANES), lambda i: (i, 0)),
            pl.BlockSpec((4 * HEADS, tm), lambda i: (0, i)),
        ],
        out_shape=[
            jax.ShapeDtypeStruct((m, D_MODEL), BF16),
            jax.ShapeDtypeStruct((m, LANES), F32),
            jax.ShapeDtypeStruct((4 * HEADS, m), F32),
        ],
        compiler_params=_params("arbitrary"),
        name="mlstm_gates",
    )(x, w_in_t, b8)


def _mixer_chunk(rows, causal, qkv_ref, g_ref, gt_ref, intra_ref, dec_ref, kdec_ref, rgn_ref, mgn_ref,
                 mix_ref, s_ref, c_ref, n_ref, m_scr):
    def head(group, h):
        lo = group * WIDTH + h * HEAD_DIM
        return qkv_ref[rows, lo:lo + HEAD_DIM]

    gw = []
    for h in range(HEADS):
        b_col = g_ref[rows, 2 * HEADS + h:2 * HEADS + h + 1]
        u_row = gt_ref[3 * HEADS + h:3 * HEADS + h + 1, rows]
        m_prev = m_scr[h:h + 1, 0:1]
        dlog = jnp.where(causal, b_col + u_row, -jnp.inf)
        inter = b_col + m_prev
        mt = jnp.maximum(inter, jnp.max(dlog, axis=1, keepdims=True))
        m_new = mt[CHUNK - 1:CHUNK, :]
        b_last = b_col[CHUNK - 1:CHUNK, :]
        gw.append(dict(
            dw=jnp.exp(dlog - mt), iw=jnp.exp(inter - mt), emt=jnp.exp(-mt),
            sw=jnp.exp(b_last - m_new + u_row), sd=jnp.exp(b_last + m_prev - m_new)))
        m_scr[h:h + 1, :] = jnp.broadcast_to(m_new, (1, LANES))

    st = []
    for h in range(HEADS):
        for kind, (qg, state_ref) in enumerate(((0, s_ref), (MO_GROUP - 3, c_ref))):
            q = head(qg, h)
            k = head(qg + 1, h)
            old = state_ref[0, 0, h]
            kt = k.T
            st.append(dict(
                h=h, kind=kind, q=q, k=k, kt=kt, v=head(qg + 2, h), old=old,
                sc=jnp.dot(q, kt, preferred_element_type=F32),
                inter=jnp.dot(q, old.astype(BF16), preferred_element_type=F32)))

    for e in st:
        h = e["h"]
        if e["kind"] == 0:
            kd = (e["kt"].astype(F32) * kdec_ref[h:h + 1, :]).astype(BF16)
            s_ref[0, 0, h] = (e["old"] * RET_STATE_DECAY[h]
                              + jnp.dot(kd, e["v"], preferred_element_type=F32))
        else:
            w = gw[h]
            ksw = (e["kt"].astype(F32) * w["sw"]).astype(BF16)
            c_ref[0, 0, h] = w["sd"] * e["old"] + jnp.dot(ksw, e["v"], preferred_element_type=F32)
            n_old = n_ref[0, 0, h:h + 1, :]
            e["qn"] = jnp.sum(e["q"].astype(F32) * n_old, axis=1, keepdims=True)
            sw8 = jnp.broadcast_to(w["sw"], (SUBLANES, CHUNK)).astype(BF16)
            n_ref[0, 0, h:h + 1, :] = (w["sd"] * n_old
                                       + jnp.dot(sw8, e["k"], preferred_element_type=F32)[0:1, :])

    for e in st:
        h = e["h"]
        e["sc"] = e["sc"] * (intra_ref[h] if e["kind"] == 0 else gw[h]["dw"])
        e["pv"] = jnp.dot(e["sc"].astype(BF16), e["v"], preferred_element_type=F32)

    for e in st:
        e["rows"] = rows
        if e["kind"]:
            e["iw"] = gw[e["h"]]["iw"]
            e["emt"] = gw[e["h"]]["emt"]
    return st


def _mixer_tail(streams, qkv_ref, dec_ref, rgn_ref, mgn_ref, mix_ref):
    dens = [jnp.sum(e["sc"], axis=1, keepdims=True) if e["kind"] else None for e in streams]
    outs = []
    for e, den in zip(streams, dens):
        if e["kind"] == 0:
            outs.append(e["pv"] + e["inter"] * dec_ref[e["h"]])
        else:
            num = e["pv"] + e["iw"] * e["inter"]
            den = den + e["iw"] * e["qn"]
            outs.append(num * (1.0 / jnp.maximum(jnp.abs(den), e["emt"])))
    mus = [jnp.mean(o, axis=-1, keepdims=True) for o in outs]
    cen = [o - mu for o, mu in zip(outs, mus)]
    var = [jnp.mean(jnp.square(oc), axis=-1, keepdims=True) for oc in cen]
    for e, oc, v in zip(streams, cen, var):
        rows = e["rows"]
        lo = e["h"] * HEAD_DIM
        gate_lo = (MO_GROUP if e["kind"] else RG_GROUP) * WIDTH + lo
        gain_ref = mgn_ref if e["kind"] else rgn_ref
        out_lo = e["kind"] * WIDTH + lo
        y = (oc * lax.rsqrt(v + GN_EPS) * gain_ref[:, lo:lo + HEAD_DIM]
             * qkv_ref[rows, gate_lo:gate_lo + HEAD_DIM].astype(F32))
        mix_ref[rows, out_lo:out_lo + HEAD_DIM] = y.astype(mix_ref.dtype)


def _prompt_mixer_kernel(qkv_ref, g_ref, gt_ref, intra_ref, dec_ref, kdec_ref, rgn_ref, mgn_ref,
                         mix_ref, s_ref, c_ref, n_ref, m_ref, m_scr):
    c = pl.program_id(1)

    @pl.when(c == 0)
    def _():
        s_ref[...] = jnp.zeros_like(s_ref)
        c_ref[...] = jnp.zeros_like(c_ref)
        n_ref[...] = jnp.zeros_like(n_ref)
        m_scr[...] = jnp.zeros_like(m_scr)

    ti = lax.broadcasted_iota(jnp.int32, (CHUNK, CHUNK), 0)
    si = lax.broadcasted_iota(jnp.int32, (CHUNK, CHUNK), 1)
    causal = ti >= si

    streams = []
    for ci in range(MIXER_CHUNKS_PER_STEP):
        streams += _mixer_chunk(slice(ci * CHUNK, (ci + 1) * CHUNK), causal, qkv_ref, g_ref, gt_ref,
                                intra_ref, dec_ref, kdec_ref, rgn_ref, mgn_ref, mix_ref, s_ref, c_ref,
                                n_ref, m_scr)
    _mixer_tail(streams, qkv_ref, dec_ref, rgn_ref, mgn_ref, mix_ref)

    @pl.when(c == pl.num_programs(1) - 1)
    def _():
        m_ref[0] = m_scr[...]


def _prompt_mixer(qkv, g_tok, g_t, intra, qdec, kdec, rgn, mgn, *, batch, seq):
    rows = MIXER_CHUNKS_PER_STEP * CHUNK
    nc = seq // rows
    row = lambda b, c: (b * nc + c, 0)
    const2 = lambda b, c: (0, 0)
    state = pl.BlockSpec((1, 1, HEADS, HEAD_DIM, HEAD_DIM), lambda b, c: (0, b, 0, 0, 0))
    return pl.pallas_call(
        _prompt_mixer_kernel,
        grid=(batch, nc),
        in_specs=[
            pl.BlockSpec((rows, 8 * WIDTH), row),
            pl.BlockSpec((rows, LANES), row),
            pl.BlockSpec((4 * HEADS, rows), lambda b, c: (0, b * nc + c)),
            pl.BlockSpec((HEADS, CHUNK, CHUNK), lambda b, c: (0, 0, 0)),
            pl.BlockSpec((HEADS, CHUNK, HEAD_DIM), lambda b, c: (0, 0, 0)),
            pl.BlockSpec((HEADS, CHUNK), const2),
            pl.BlockSpec((1, WIDTH), const2),
            pl.BlockSpec((1, WIDTH), const2),
        ],
        out_specs=[
            pl.BlockSpec((rows, 2 * WIDTH), row),
            state,
            state,
            pl.BlockSpec((1, 1, HEADS, HEAD_DIM), lambda b, c: (0, b, 0, 0)),
            pl.BlockSpec((1, SUBLANES, LANES), lambda b, c: (b, 0, 0)),
        ],
        out_shape=[
            jax.ShapeDtypeStruct((batch * seq, 2 * WIDTH), BF16),
            jax.ShapeDtypeStruct((1, batch, HEADS, HEAD_DIM, HEAD_DIM), F32),
            jax.ShapeDtypeStruct((1, batch, HEADS, HEAD_DIM, HEAD_DIM), F32),
            jax.ShapeDtypeStruct((1, batch, HEADS, HEAD_DIM), F32),
            jax.ShapeDtypeStruct((batch, SUBLANES, LANES), F32),
        ],
        scratch_shapes=[pltpu.VMEM((SUBLANES, LANES), F32)],
        compiler_params=_params("arbitrary", "arbitrary"),
        name="prompt_mixer",
    )(qkv, g_tok, g_t, intra, qdec, kdec, rgn, mgn)


def _sample_gate_kernel(g_ref, m0_ref, mt_ref, dw_ref, iw_ref, emt_ref):
    ig = g_ref[...]
    lf = pltpu.roll(ig, LANES - HEADS, 1)
    inter = lf + m0_ref[...]
    mt = jnp.maximum(inter, ig)
    mt_ref[...] = mt
    dw_ref[...] = jnp.exp(ig - mt)
    iw_ref[...] = jnp.exp(inter - mt)
    emt_ref[...] = jnp.exp(-mt)


def _sample_gates(g_tok, m0_pad):
    m = g_tok.shape[0]
    spec = pl.BlockSpec((m, LANES), lambda: (0, 0))
    return pl.pallas_call(
        _sample_gate_kernel,
        in_specs=[spec, spec],
        out_specs=[spec] * 4,
        out_shape=[jax.ShapeDtypeStruct((m, LANES), F32)] * 4,
        name="sample_gates",
    )(g_tok, m0_pad)


def _sample_token_stages(b, scal_ref, rows_ref, s0_ref, c0_ref, n0_ref, rgn_ref, mgn_ref,
                         mix_ref, s_ref, c_ref, n_ref):
    first = lax.broadcasted_iota(jnp.int32, (4 * HEADS, HEAD_DIM), 0) == 0
    dw = [scal_ref[b, HEADS + h] for h in range(HEADS)]
    iw = [scal_ref[b, 2 * HEADS + h] for h in range(HEADS)]
    emt = [scal_ref[b, 3 * HEADS + h] for h in range(HEADS)]
    live = {}

    def row(group, h):
        lo = group * WIDTH + h * HEAD_DIM
        return rows_ref[pl.ds(b, 1), lo:lo + HEAD_DIM]

    def outer(k, v):
        kp = jnp.where(first, jnp.broadcast_to(k, first.shape), 0.0).astype(BF16)
        vp = jnp.where(first, jnp.broadcast_to(v, first.shape), 0.0).astype(BF16)
        return lax.dot_general(kp, vp, _TN, preferred_element_type=F32)

    def apply(q_row, state):
        q8 = jnp.broadcast_to(q_row, (SUBLANES, HEAD_DIM)).astype(BF16)
        return jnp.dot(q8, state.astype(BF16), preferred_element_type=F32)[0:1, :]

    def update_states():
        kv_s = [outer(row(1, h), row(2, h)) for h in range(HEADS)]
        kv_c = [outer(row(5, h), dw[h] * row(6, h)) for h in range(HEADS)]
        live["s"], live["c"] = [], []
        for h in range(HEADS):
            live["s"].append(RET_G[h] * s0_ref[0, 0, h] + kv_s[h])
            s_ref[0, 0, h] = live["s"][h]
            live["c"].append(iw[h] * c0_ref[0, 0, h] + kv_c[h])
            c_ref[0, 0, h] = live["c"][h]

    def read_states():
        live["o"] = [apply(row(0, h), live["s"][h]) for h in range(HEADS)]
        live["num"] = [apply(row(4, h), live["c"][h]) for h in range(HEADS)]

    def write_outputs():
        for h in range(HEADS):
            lo = h * HEAD_DIM
            mix_ref[pl.ds(b, 1), lo:lo + HEAD_DIM] = (
                _head_norm(live["o"][h]) * rgn_ref[:, lo:lo + HEAD_DIM] * row(RG_GROUP, h))
            n_new = iw[h] * n0_ref[0, 0, h:h + 1, :] + dw[h] * row(5, h)
            n_ref[0, 0, h:h + 1, :] = n_new
            den = jnp.sum(row(4, h) * n_new, axis=1, keepdims=True)
            hid = live["num"][h] * (1.0 / jnp.maximum(jnp.abs(den), emt[h]))
            mix_ref[pl.ds(b, 1), WIDTH + lo:WIDTH + lo + HEAD_DIM] = (
                _head_norm(hid) * mgn_ref[:, lo:lo + HEAD_DIM] * row(MO_GROUP, h))

    return [update_states, read_states, write_outputs]


def _outproj_kernel(a_ref, w_ref, x_ref, g_ref, b_ref, x1_ref, x1b_ref, *wb_out):
    if wb_out:
        wb_out[0][...] = w_ref[...].astype(BF16)
        w_ref = wb_out[0]
    tm = a_ref.shape[0]
    for rows in _row_splits(tm):
        mix = jnp.dot(a_ref[rows, :].astype(BF16), w_ref[...], preferred_element_type=F32)
        x1 = _layer_norm(ALPHA * x_ref[rows, :] + mix, g_ref[...], b_ref[...])
        x1_ref[rows, :] = x1
        x1b_ref[rows, :] = x1.astype(BF16)


def _outproj(a, w, x, g, b, *, tm):
    m = a.shape[0]
    cast_w = w.dtype != BF16
    assert not cast_w or m == tm
    row = pl.BlockSpec((tm, D_MODEL), lambda i: (i, 0))
    vec = pl.BlockSpec((1, D_MODEL), lambda i: (0, 0))
    out_specs = [row, row]
    out_shape = [jax.ShapeDtypeStruct((m, D_MODEL), F32), jax.ShapeDtypeStruct((m, D_MODEL), BF16)]
    if cast_w:
        out_specs.append(pl.BlockSpec((D_MODEL, D_MODEL), lambda i: (0, 0)))
        out_shape.append(jax.ShapeDtypeStruct((D_MODEL, D_MODEL), BF16))
    return pl.pallas_call(
        _outproj_kernel,
        grid=(m // tm,),
        in_specs=[row, pl.BlockSpec((D_MODEL, D_MODEL), lambda i: (0, 0), pipeline_mode=pl.Buffered(1)),
                  row, vec, vec],
        out_specs=out_specs,
        out_shape=out_shape,
        compiler_params=_params("arbitrary"),
        name="outproj_ln1_cast" if cast_w else "outproj_ln1",
    )(a, w, x, g, b)


def _ffn_kernel(x1b_ref, w1_ref, w2_ref, o_ref, *wb_out):
    @pl.when(pl.program_id(1) == 0)
    def _():
        o_ref[...] = jnp.zeros_like(o_ref)

    w1 = w1_ref[...]
    w2 = w2_ref[...]
    if wb_out:
        w1 = w1.astype(BF16)
        w2 = w2.astype(BF16)
        wb_out[0][...] = w1
        wb_out[1][...] = w2
    hid = jnp.dot(x1b_ref[...], w1, preferred_element_type=F32)
    hid = jnp.square(jnp.maximum(hid, 0.0)).astype(BF16)
    o_ref[...] += jnp.dot(hid, w2, preferred_element_type=F32)


def _ffn(x1b, w1, w2, *, tm, tf):
    m = x1b.shape[0]
    cast_w = w1.dtype != BF16
    assert not cast_w or m == tm
    row = pl.BlockSpec((tm, D_MODEL), lambda i, f: (i, 0))
    w1spec = pl.BlockSpec((D_MODEL, tf), lambda i, f: (0, f))
    w2spec = pl.BlockSpec((tf, D_MODEL), lambda i, f: (f, 0))
    oshape = jax.ShapeDtypeStruct((m, D_MODEL), F32)
    return pl.pallas_call(
        _ffn_kernel,
        grid=(m // tm, D_FF // tf),
        in_specs=[row, w1spec, w2spec],
        out_specs=[row, w1spec, w2spec] if cast_w else row,
        out_shape=[oshape, jax.ShapeDtypeStruct(w1.shape, BF16),
                   jax.ShapeDtypeStruct(w2.shape, BF16)] if cast_w else oshape,
        compiler_params=_params("arbitrary", "arbitrary"),
        name="ffn_cast" if cast_w else "ffn",
    )(x1b, w1, w2)


def _pe_ln2_kernel(x1_ref, x1b_ref, p_ref, ff_ref, wpe_ref, wg_ref, g_ref, b_ref, o_ref, *wb_out):
    if wb_out:
        wb_out[0][...] = wpe_ref[...].astype(BF16)
        wb_out[1][...] = wg_ref[...].astype(BF16)
        wpe_ref, wg_ref = wb_out
    for rows in _row_splits(x1_ref.shape[0]):
        gate = _sigmoid(jnp.dot(x1b_ref[rows, :], wg_ref[...], preferred_element_type=F32))
        pe = jnp.dot(p_ref[rows, :].astype(BF16), wpe_ref[...], preferred_element_type=F32)
        y = ALPHA * x1_ref[rows, :] + ff_ref[rows, :] + pe * gate
        o_ref[rows, :] = _layer_norm(y, g_ref[...], b_ref[...])


def _pe_ln2(x1, x1b, p, ff, wpe, wg, g, b, *, tm):
    m = x1.shape[0]
    cast_w = wg.dtype != BF16
    assert not cast_w or m == tm
    row = pl.BlockSpec((tm, D_MODEL), lambda i: (i, 0))
    vec = pl.BlockSpec((1, D_MODEL), lambda i: (0, 0))
    oshape = jax.ShapeDtypeStruct((m, D_MODEL), F32)
    wb_specs = [pl.BlockSpec((PLE_DIM, D_MODEL), lambda i: (0, 0)),
                pl.BlockSpec((D_MODEL, D_MODEL), lambda i: (0, 0))]
    wb_shapes = [jax.ShapeDtypeStruct(wpe.shape, BF16), jax.ShapeDtypeStruct(wg.shape, BF16)]
    return pl.pallas_call(
        _pe_ln2_kernel,
        grid=(m // tm,),
        in_specs=[row, row, pl.BlockSpec((tm, PLE_DIM), lambda i: (i, 0)), row,
                  pl.BlockSpec((PLE_DIM, D_MODEL), lambda i: (0, 0), pipeline_mode=pl.Buffered(1)),
                  pl.BlockSpec((D_MODEL, D_MODEL), lambda i: (0, 0), pipeline_mode=pl.Buffered(1)),
                  vec, vec],
        out_specs=[row] + wb_specs if cast_w else row,
        out_shape=[oshape] + wb_shapes if cast_w else oshape,
        compiler_params=_params("arbitrary"),
        name="pe_ln2_cast" if cast_w else "pe_ln2",
    )(x1, x1b, p, ff, wpe, wg, g, b)


def _rope_tables(pos):
    inv = ROPE_BASE ** (-jnp.arange(HALF, dtype=F32) * 2.0 / HEAD_DIM)
    ang = pos[:, None] * inv[None, :]
    return jnp.cos(ang), jnp.sin(ang)


def _decay_tables():
    lg = jnp.log(1.0 - 2.0 ** (-5.0 - jnp.arange(HEADS, dtype=F32)))
    t = jnp.arange(CHUNK, dtype=F32)
    causal = t[:, None] >= t[None, :]
    intra = jnp.exp(jnp.where(causal, (t[:, None] - t[None, :]) * lg[:, None, None], -jnp.inf))
    q_decay = jnp.exp(lg[:, None] * (t + 1.0))
    k_decay = jnp.exp(lg[:, None] * (CHUNK - 1.0 - t))
    return intra, jnp.broadcast_to(q_decay[:, :, None], (HEADS, CHUNK, HEAD_DIM)), k_decay


def _tail(x, mixed, p, w, *, tm, tm_ffn, tf):
    casts = {}
    res = _outproj(mixed, w["out"], x, w["ln1_g"], w["ln1_b"], tm=tm)
    x1, x1b = res[0], res[1]
    if len(res) > 2:
        casts["out"] = res[2]
    res = _ffn(x1b, w["ff1"], w["ff2"], tm=tm_ffn, tf=tf)
    if isinstance(res, (list, tuple)):
        ff, casts["ff1"], casts["ff2"] = res
    else:
        ff = res
    res = _pe_ln2(x1, x1b, p, ff, w["pe"], w["pe_gate"], w["ln2_g"], w["ln2_b"], tm=tm)
    if isinstance(res, (list, tuple)):
        y, casts["pe"], casts["pe_gate"] = res
    else:
        y = res
    return y, casts


def kernel(x_prompt, x_sample, state_ret, state_mlstm_C, state_mlstm_n, state_mlstm_m, p_prompt, p_sample, w_in, b_gate, ret_gn_w, mlstm_gn_w, w_out, ln1_g, ln1_b, w_ff1, w_ff2, w_pe, w_pe_gate, ln2_g, ln2_b):
    batch, seq, _ = x_prompt.shape
    nb = x_sample.shape[0]

    w_in_t = w_in[0].T
    b8 = jnp.pad(b_gate[0].astype(F32), (0, LANES - 2 * HEADS)).reshape(1, LANES)
    w = {
        "out": w_out[0], "ff1": w_ff1[0], "ff2": w_ff2[0], "pe": w_pe[0], "pe_gate": w_pe_gate[0],
        "ln1_g": ln1_g[0].reshape(1, D_MODEL), "ln1_b": ln1_b[0].reshape(1, D_MODEL),
        "ln2_g": ln2_g[0].reshape(1, D_MODEL), "ln2_b": ln2_b[0].reshape(1, D_MODEL),
    }
    rgn = ret_gn_w[0].reshape(1, WIDTH)
    mgn = mlstm_gn_w[0].reshape(1, WIDTH)
    intra, qdec, kdec = _decay_tables()

    xs = x_sample.reshape(nb, D_MODEL)
    xsb, gs_tok, _ = _mgate(xs, w_in_t, b8, tm=nb)
    ctab, stab = _rope_tables(jnp.full((nb,), PAST_LEN, dtype=F32))
    proj_s, w_in_b = _inproj(xsb, w_in_t, ctab, stab, tm=nb)
    proj_s = proj_s.astype(F32)
    m0 = jnp.pad(state_mlstm_m[0], ((0, 0), (0, LANES - HEADS)))
    mt, dw, iw, emt = _sample_gates(gs_tok, m0)
    scal = jnp.concatenate([mt[:, :HEADS], dw[:, :HEADS], iw[:, :HEADS], emt[:, :HEADS]], axis=1)
    m_s = mt[:, :HEADS].reshape(1, nb, HEADS)

    xp = x_prompt.reshape(batch * seq, D_MODEL)
    xpb, g_tok, g_t = _mgate(xp, w_in_t, b8, tm=GATE_TM)
    ctab, stab = _rope_tables(jnp.arange(seq, dtype=F32))
    proj_p, mixed_s, s_s, c_s, n_s = _inproj(
        xpb, w_in_b, ctab, stab, tm=INPROJ_TM,
        sample=(scal, proj_s, state_ret, state_mlstm_C, state_mlstm_n, rgn, mgn))
    mixed_p, s_p, c_p, n_p, m_rows = _prompt_mixer(proj_p, g_tok, g_t, intra, qdec, kdec, rgn, mgn,
                                                   batch=batch, seq=seq)
    m_p = m_rows[:, :HEADS, 0].reshape(1, batch, HEADS)

    y_s, w_bf16 = _tail(xs, mixed_s, p_sample[0].reshape(nb, PLE_DIM), w,
                        tm=nb, tm_ffn=nb, tf=CAST_FFN_TF)
    w = {**w, **w_bf16}
    y_p, _ = _tail(xp, mixed_p, p_prompt[0].reshape(batch * seq, PLE_DIM), w,
                   tm=TAIL_TM, tm_ffn=FFN_TM, tf=FFN_TF)

    return (y_p.reshape(batch, seq, D_MODEL), y_s.reshape(nb, 1, D_MODEL),
            s_p, c_p, n_p, m_p, s_s, c_s, n_s, m_s)
```

```python
import jax
import jax.numpy as jnp
import numpy as np
from jax import lax
from jax.experimental import pallas as pl
from jax.experimental.pallas import tpu as pltpu

F32 = jnp.float32
BF16 = jnp.bfloat16

D_MODEL = 2048
HEADS = 4
HEAD_DIM = 256
HALF = HEAD_DIM // 2
WIDTH = HEADS * HEAD_DIM
D_FF = 4 * D_MODEL
PLE_DIM = 256
CHUNK = 128
PAST_LEN = 16384
ROPE_BASE = 10000.0
LN_EPS = 1e-5
GN_EPS = 1e-6
DEPTH = 1
ALPHA = (2 * DEPTH) ** 0.25
QK_SCALE = HEAD_DIM ** -0.5
LANES = 128
SUBLANES = 8
MXU_ROWS = 256
VMEM_LIMIT = 56 * 1024 * 1024
MIXER_CHUNKS_PER_STEP = 2
GATE_TM = 1024
INPROJ_TM = 512
TAIL_TM = 512
FFN_TM = 1024
FFN_TF = 1024
CAST_FFN_TF = 512

RET_G = tuple(1.0 - 2.0 ** (-5.0 - h) for h in range(HEADS))
RET_LOG_G = tuple(float(np.log(np.float32(g))) for g in RET_G)
RET_STATE_DECAY = tuple(float(np.exp(np.float32(lg) * np.float32(CHUNK))) for lg in RET_LOG_G)

_NT = (((1,), (1,)), ((), ()))
_TN = (((0,), (0,)), ((), ()))


def _params(*sem):
    return pltpu.CompilerParams(dimension_semantics=sem, vmem_limit_bytes=VMEM_LIMIT)


def _sigmoid(x):
    return 0.5 * jnp.tanh(0.5 * x) + 0.5


def _log_sigmoid(x):
    return jnp.minimum(x, 0.0) - jnp.log(1.0 + jnp.exp(-jnp.abs(x)))


def _layer_norm(y, g, b):
    mu = jnp.mean(y, axis=-1, keepdims=True)
    yc = y - mu
    var = jnp.mean(jnp.square(yc), axis=-1, keepdims=True)
    return yc * lax.rsqrt(var + LN_EPS) * g + b


def _row_splits(tm, min_rows=MXU_ROWS):
    if tm < 2 * min_rows:
        return [slice(0, tm)]
    return [slice(0, tm // 2), slice(tm // 2, tm)]


def _head_norm(o):
    mu = jnp.mean(o, axis=-1, keepdims=True)
    oc = o - mu
    var = jnp.mean(jnp.square(oc), axis=-1, keepdims=True)
    return oc * lax.rsqrt(var + GN_EPS)


RG_GROUP, MO_GROUP = 3, 7


def _inproj_body(x_ref, wb_ref, c_ref, s_ref, o_ref, hosted=()):
    j = pl.program_id(0)
    is_gate = (j == RG_GROUP) | (j == MO_GROUP)
    pieces = [(h, rows) for h in range(HEADS) for rows in _row_splits(x_ref.shape[0], 2 * MXU_ROWS)]

    def piece_acc(h, rows):
        return lax.dot_general(x_ref[rows, :], wb_ref[h * HEAD_DIM:(h + 1) * HEAD_DIM, :], _NT,
                               preferred_element_type=F32)

    def gate_piece(h, rows):
        acc = piece_acc(h, rows)
        sig = _sigmoid(acc)
        o_ref[rows, h * HEAD_DIM:(h + 1) * HEAD_DIM] = jnp.where(
            j == RG_GROUP, acc * sig, sig).astype(o_ref.dtype)

    def qkv_piece(h, rows):
        rotary = j <= 1
        scale = jnp.where((j == 1) | (j == MO_GROUP - 2), QK_SCALE, 1.0).astype(F32)
        lo = h * HEAD_DIM
        c = jnp.where(rotary, c_ref[rows, :], 1.0) * scale
        s = jnp.where(rotary, s_ref[rows, :], 0.0) * scale
        acc = piece_acc(h, rows)
        x1 = acc[:, :HALF]
        x2 = acc[:, HALF:]
        o_ref[rows, lo:lo + HALF] = (x1 * c - x2 * s).astype(o_ref.dtype)
        o_ref[rows, lo + HALF:lo + HEAD_DIM] = (x1 * s + x2 * c).astype(o_ref.dtype)

    def run(piece):
        for n, (h, rows) in enumerate(pieces):
            if n < len(hosted):
                hosted[n]()
            piece(h, rows)

    pl.when(is_gate)(lambda: run(gate_piece))
    pl.when(jnp.logical_not(is_gate))(lambda: run(qkv_piece))


def _inproj_cast_kernel(x_ref, wt_ref, c_ref, s_ref, o_ref, wb_ref):
    wb_ref[...] = wt_ref[...].astype(BF16)
    _inproj_body(x_ref, wb_ref, c_ref, s_ref, o_ref)


def _inproj_host_kernel(x_ref, wt_ref, c_ref, s_ref, scal_ref, cols_ref, rows_ref, s0_ref, c0_ref,
                        n0_ref, rgn_ref, mgn_ref, o_ref, smix_ref, ss_ref, sc_ref, sn_ref):
    token = pl.program_id(0) * pl.num_programs(1) + pl.program_id(1)
    hosted = _sample_token_chunks(token, scal_ref, cols_ref, rows_ref, s0_ref, c0_ref, n0_ref,
                                  rgn_ref, mgn_ref, smix_ref, ss_ref, sc_ref, sn_ref)
    _inproj_body(x_ref, wt_ref, c_ref, s_ref, o_ref, hosted)


def _inproj(xb, wt, cos, sin, *, tm, sample=None):
    m = xb.shape[0]
    n_i = m // tm
    n_pos = cos.shape[0] // tm
    wspec = pl.BlockSpec((WIDTH, D_MODEL), lambda j, i: (j, 0))
    ospec = pl.BlockSpec((tm, WIDTH), lambda j, i: (i, j))
    oshape = jax.ShapeDtypeStruct((m, 8 * WIDTH), BF16)
    tspec = pl.BlockSpec((tm, HALF), lambda j, i: (i % n_pos, 0))
    in_specs = [pl.BlockSpec((tm, D_MODEL), lambda j, i: (i, 0)), wspec, tspec, tspec]
    if wt.dtype != BF16:
        assert n_i == 1 and sample is None
        return pl.pallas_call(
            _inproj_cast_kernel,
            grid=(8, 1),
            in_specs=in_specs,
            out_specs=[ospec, wspec],
            out_shape=[oshape, jax.ShapeDtypeStruct((8 * WIDTH, D_MODEL), BF16)],
            compiler_params=_params("arbitrary", "arbitrary"),
            name="inproj_cast",
        )(xb, wt, cos, sin)
    scal, cols_s, rows_s, s0, c0, n0, rgn, mgn = sample
    nb = rows_s.shape[0]
    assert nb == 8 * n_i
    const2 = lambda j, i: (0, 0)
    sstate = pl.BlockSpec((1, 1, HEADS, HEAD_DIM, HEAD_DIM), lambda j, i: (0, j * n_i + i, 0, 0, 0))
    snorm = pl.BlockSpec((1, 1, HEADS, HEAD_DIM), lambda j, i: (0, j * n_i + i, 0, 0))
    return pl.pallas_call(
        _inproj_host_kernel,
        grid=(8, n_i),
        in_specs=in_specs + [
            pl.BlockSpec(memory_space=pltpu.SMEM),
            pl.BlockSpec((1, HEAD_DIM, 4 * HEADS), lambda j, i: (j * n_i + i, 0, 0)),
            pl.BlockSpec((nb, 8 * WIDTH), const2, pipeline_mode=pl.Buffered(1)),
            sstate, sstate, snorm,
            pl.BlockSpec((1, WIDTH), const2),
            pl.BlockSpec((1, WIDTH), const2),
        ],
        out_specs=[ospec, pl.BlockSpec((nb, 2 * WIDTH), const2), sstate, sstate, snorm],
        out_shape=[oshape, jax.ShapeDtypeStruct((nb, 2 * WIDTH), F32),
                   jax.ShapeDtypeStruct(s0.shape, F32), jax.ShapeDtypeStruct(c0.shape, F32),
                   jax.ShapeDtypeStruct(n0.shape, F32)],
        compiler_params=_params("arbitrary", "arbitrary"),
        name="inproj",
    )(xb, wt, cos, sin, scal, cols_s, rows_s, s0, c0, n0, rgn, mgn)


def _mgate_kernel(x_ref, wt_ref, b_ref, xb_ref, g_ref, gt_ref):
    xb = x_ref[...].astype(BF16)
    xb_ref[...] = xb
    wt = jnp.concatenate([wt_ref[...], jnp.zeros((LANES - 2 * HEADS, D_MODEL), F32)], axis=0)
    g = lax.dot_general(xb, wt.astype(BF16), _NT, preferred_element_type=F32) + b_ref[...]
    lane = lax.broadcasted_iota(jnp.int32, g.shape, 1)
    g = jnp.where((lane >= HEADS) & (lane < 2 * HEADS), _log_sigmoid(g), g)
    r = lax.broadcasted_iota(jnp.int32, (CHUNK, CHUNK), 0)
    s = lax.broadcasted_iota(jnp.int32, (CHUNK, CHUNK), 1)
    tril = (r >= s).astype(BF16)
    g1 = g.astype(BF16)
    res = g - g1.astype(F32)
    g2 = res.astype(BF16)
    g3 = (res - g2.astype(F32)).astype(BF16)
    parts = []
    for c in range(g.shape[0] // CHUNK):
        sl = slice(c * CHUNK, (c + 1) * CHUNK)
        parts.append(jnp.dot(tril, g1[sl], preferred_element_type=F32)
                     + jnp.dot(tril, g2[sl], preferred_element_type=F32)
                     + jnp.dot(tril, g3[sl], preferred_element_type=F32))
    cs = parts[0] if len(parts) == 1 else jnp.concatenate(parts, axis=0)
    b = pltpu.roll(cs, HEADS, 1)
    u = pltpu.roll(g, 3 * HEADS, 1) - pltpu.roll(cs, 2 * HEADS, 1)
    g = jnp.where((lane >= 2 * HEADS) & (lane < 3 * HEADS), b, g)
    g = jnp.where((lane >= 3 * HEADS) & (lane < 4 * HEADS), u, g)
    g_ref[...] = g
    gt_ref[...] = g.T[0:4 * HEADS, :]


def _mgate(x, w_in_t, b8, *, tm):
    m = x.shape[0]
    gate_rows = 8 * WIDTH // (2 * HEADS)
    return pl.pallas_call(
        _mgate_kernel,
        grid=(m // tm,),
        in_specs=[
            pl.BlockSpec((tm, D_MODEL), lambda i: (i, 0)),
            pl.BlockSpec((2 * HEADS, D_MODEL), lambda i: (gate_rows, 0)),
            pl.BlockSpec((1, LANES), lambda i: (0, 0)),
        ],
        out_specs=[
            pl.BlockSpec((tm, D_MODEL), lambda i: (i, 0)),
            pl.BlockSpec((tm, LANES), lambda i: (i, 0)),
            pl.BlockSpec((4 * HEADS, tm), lambda i: (0, i)),
        ],
        out_shape=[
            jax.ShapeDtypeStruct((m, D_MODEL), BF16),
            jax.ShapeDtypeStruct((m, LANES), F32),
            jax.ShapeDtypeStruct((4 * HEADS, m), F32),
        ],
        compiler_params=_params("arbitrary"),
        name="mlstm_gates",
    )(x, w_in_t, b8)


def _mixer_chunk(rows, causal, qkv_ref, g_ref, gt_ref, intra_ref, dec_ref, kdec_ref, rgn_ref, mgn_ref,
                 mix_ref, s_ref, c_ref, n_ref, m_scr):
    def head(group, h):
        lo = group * WIDTH + h * HEAD_DIM
        return qkv_ref[rows, lo:lo + HEAD_DIM]

    gw = []
    for h in range(HEADS):
        b_col = g_ref[rows, 2 * HEADS + h:2 * HEADS + h + 1]
        u_row = gt_ref[3 * HEADS + h:3 * HEADS + h + 1, rows]
        m_prev = m_scr[h:h + 1, 0:1]
        dlog = jnp.where(causal, b_col + u_row, -jnp.inf)
        inter = b_col + m_prev
        mt = jnp.maximum(inter, jnp.max(dlog, axis=1, keepdims=True))
        m_new = mt[CHUNK - 1:CHUNK, :]
        b_last = b_col[CHUNK - 1:CHUNK, :]
        gw.append(dict(
            dw=jnp.exp(dlog - mt), iw=jnp.exp(inter - mt), emt=jnp.exp(-mt),
            sw=jnp.exp(b_last - m_new + u_row), sd=jnp.exp(b_last + m_prev - m_new)))
        m_scr[h:h + 1, :] = jnp.broadcast_to(m_new, (1, LANES))

    st = []
    for h in range(HEADS):
        for kind, (qg, state_ref) in enumerate(((0, s_ref), (MO_GROUP - 3, c_ref))):
            q = head(qg, h)
            k = head(qg + 1, h)
            old = state_ref[0, 0, h]
            kt = k.T
            st.append(dict(
                h=h, kind=kind, q=q, k=k, kt=kt, v=head(qg + 2, h), old=old,
                sc=jnp.dot(q, kt, preferred_element_type=F32),
                inter=jnp.dot(q, old.astype(BF16), preferred_element_type=F32)))

    for e in st:
        h = e["h"]
        if e["kind"] == 0:
            kd = (e["kt"].astype(F32) * kdec_ref[h:h + 1, :]).astype(BF16)
            s_ref[0, 0, h] = (e["old"] * RET_STATE_DECAY[h]
                              + jnp.dot(kd, e["v"], preferred_element_type=F32))
        else:
            w = gw[h]
            ksw = (e["kt"].astype(F32) * w["sw"]).astype(BF16)
            c_ref[0, 0, h] = w["sd"] * e["old"] + jnp.dot(ksw, e["v"], preferred_element_type=F32)
            n_old = n_ref[0, 0, h:h + 1, :]
            e["qn"] = jnp.sum(e["q"].astype(F32) * n_old, axis=1, keepdims=True)
            sw8 = jnp.broadcast_to(w["sw"], (SUBLANES, CHUNK)).astype(BF16)
            n_ref[0, 0, h:h + 1, :] = (w["sd"] * n_old
                                       + jnp.dot(sw8, e["k"], preferred_element_type=F32)[0:1, :])

    for e in st:
        h = e["h"]
        e["sc"] = e["sc"] * (intra_ref[h] if e["kind"] == 0 else gw[h]["dw"])
        e["pv"] = jnp.dot(e["sc"].astype(BF16), e["v"], preferred_element_type=F32)

    for e in st:
        e["rows"] = rows
        if e["kind"]:
            e["iw"] = gw[e["h"]]["iw"]
            e["emt"] = gw[e["h"]]["emt"]
    return st


def _mixer_tail(streams, qkv_ref, dec_ref, rgn_ref, mgn_ref, mix_ref):
    dens = [jnp.sum(e["sc"], axis=1, keepdims=True) if e["kind"] else None for e in streams]
    outs = []
    for e, den in zip(streams, dens):
        if e["kind"] == 0:
            outs.append(e["pv"] + e["inter"] * dec_ref[e["h"]])
        else:
            num = e["pv"] + e["iw"] * e["inter"]
            den = den + e["iw"] * e["qn"]
            outs.append(num * (1.0 / jnp.maximum(jnp.abs(den), e["emt"])))
    mus = [jnp.mean(o, axis=-1, keepdims=True) for o in outs]
    cen = [o - mu for o, mu in zip(outs, mus)]
    var = [jnp.mean(jnp.square(oc), axis=-1, keepdims=True) for oc in cen]
    for e, oc, v in zip(streams, cen, var):
        rows = e["rows"]
        lo = e["h"] * HEAD_DIM
        gate_lo = (MO_GROUP if e["kind"] else RG_GROUP) * WIDTH + lo
        gain_ref = mgn_ref if e["kind"] else rgn_ref
        out_lo = e["kind"] * WIDTH + lo
        y = (oc * lax.rsqrt(v + GN_EPS) * gain_ref[:, lo:lo + HEAD_DIM]
             * qkv_ref[rows, gate_lo:gate_lo + HEAD_DIM].astype(F32))
        mix_ref[rows, out_lo:out_lo + HEAD_DIM] = y.astype(mix_ref.dtype)


def _prompt_mixer_kernel(qkv_ref, g_ref, gt_ref, intra_ref, dec_ref, kdec_ref, rgn_ref, mgn_ref,
                         mix_ref, s_ref, c_ref, n_ref, m_ref, m_scr):
    c = pl.program_id(1)

    @pl.when(c == 0)
    def _():
        s_ref[...] = jnp.zeros_like(s_ref)
        c_ref[...] = jnp.zeros_like(c_ref)
        n_ref[...] = jnp.zeros_like(n_ref)
        m_scr[...] = jnp.zeros_like(m_scr)

    ti = lax.broadcasted_iota(jnp.int32, (CHUNK, CHUNK), 0)
    si = lax.broadcasted_iota(jnp.int32, (CHUNK, CHUNK), 1)
    causal = ti >= si

    streams = []
    for ci in range(MIXER_CHUNKS_PER_STEP):
        streams += _mixer_chunk(slice(ci * CHUNK, (ci + 1) * CHUNK), causal, qkv_ref, g_ref, gt_ref,
                                intra_ref, dec_ref, kdec_ref, rgn_ref, mgn_ref, mix_ref, s_ref, c_ref,
                                n_ref, m_scr)
    _mixer_tail(streams, qkv_ref, dec_ref, rgn_ref, mgn_ref, mix_ref)

    @pl.when(c == pl.num_programs(1) - 1)
    def _():
        m_ref[0] = m_scr[...]


def _prompt_mixer(qkv, g_tok, g_t, intra, qdec, kdec, rgn, mgn, *, batch, seq):
    rows = MIXER_CHUNKS_PER_STEP * CHUNK
    nc = seq // rows
    row = lambda b, c: (b * nc + c, 0)
    const2 = lambda b, c: (0, 0)
    state = pl.BlockSpec((1, 1, HEADS, HEAD_DIM, HEAD_DIM), lambda b, c: (0, b, 0, 0, 0))
    return pl.pallas_call(
        _prompt_mixer_kernel,
        grid=(batch, nc),
        in_specs=[
            pl.BlockSpec((rows, 8 * WIDTH), row),
            pl.BlockSpec((rows, LANES), row),
            pl.BlockSpec((4 * HEADS, rows), lambda b, c: (0, b * nc + c)),
            pl.BlockSpec((HEADS, CHUNK, CHUNK), lambda b, c: (0, 0, 0)),
            pl.BlockSpec((HEADS, CHUNK, HEAD_DIM), lambda b, c: (0, 0, 0)),
            pl.BlockSpec((HEADS, CHUNK), const2),
            pl.BlockSpec((1, WIDTH), const2),
            pl.BlockSpec((1, WIDTH), const2),
        ],
        out_specs=[
            pl.BlockSpec((rows, 2 * WIDTH), row),
            state,
            state,
            pl.BlockSpec((1, 1, HEADS, HEAD_DIM), lambda b, c: (0, b, 0, 0)),
            pl.BlockSpec((1, SUBLANES, LANES), lambda b, c: (b, 0, 0)),
        ],
        out_shape=[
            jax.ShapeDtypeStruct((batch * seq, 2 * WIDTH), BF16),
            jax.ShapeDtypeStruct((1, batch, HEADS, HEAD_DIM, HEAD_DIM), F32),
            jax.ShapeDtypeStruct((1, batch, HEADS, HEAD_DIM, HEAD_DIM), F32),
            jax.ShapeDtypeStruct((1, batch, HEADS, HEAD_DIM), F32),
            jax.ShapeDtypeStruct((batch, SUBLANES, LANES), F32),
        ],
        scratch_shapes=[pltpu.VMEM((SUBLANES, LANES), F32)],
        compiler_params=_params("arbitrary", "arbitrary"),
        name="prompt_mixer",
    )(qkv, g_tok, g_t, intra, qdec, kdec, rgn, mgn)


def _sample_gate_kernel(g_ref, m0_ref, mt_ref, dw_ref, iw_ref, emt_ref):
    ig = g_ref[...]
    lf = pltpu.roll(ig, LANES - HEADS, 1)
    inter = lf + m0_ref[...]
    mt = jnp.maximum(inter, ig)
    mt_ref[...] = mt
    dw_ref[...] = jnp.exp(ig - mt)
    iw_ref[...] = jnp.exp(inter - mt)
    emt_ref[...] = jnp.exp(-mt)


def _sample_gates(g_tok, m0_pad):
    m = g_tok.shape[0]
    spec = pl.BlockSpec((m, LANES), lambda: (0, 0))
    return pl.pallas_call(
        _sample_gate_kernel,
        in_specs=[spec, spec],
        out_specs=[spec] * 4,
        out_shape=[jax.ShapeDtypeStruct((m, LANES), F32)] * 4,
        name="sample_gates",
    )(g_tok, m0_pad)


def _sample_token_chunks(b, scal_ref, cols_ref, rows_ref, s0_ref, c0_ref, n0_ref, rgn_ref, mgn_ref,
                         mix_ref, s_ref, c_ref, n_ref):
    def row(group, h):
        lo = group * WIDTH + h * HEAD_DIM
        return rows_ref[pl.ds(b, 1), lo:lo + HEAD_DIM]

    def col(group, h):
        return cols_ref[0, :, group * HEADS + h:group * HEADS + h + 1]

    def chunk(h):
        lo = h * HEAD_DIM
        s_new = RET_G[h] * s0_ref[0, 0, h] + col(1, h) * row(2, h)
        s_ref[0, 0, h] = s_new
        o = jnp.sum(col(0, h) * s_new, axis=0, keepdims=True)
        mix_ref[pl.ds(b, 1), lo:lo + HEAD_DIM] = (
            _head_norm(o) * rgn_ref[:, lo:lo + HEAD_DIM] * row(RG_GROUP, h))

        dw = scal_ref[b, HEADS + h]
        iw = scal_ref[b, 2 * HEADS + h]
        emt = scal_ref[b, 3 * HEADS + h]
        c_new = iw * c0_ref[0, 0, h] + col(3, h) * (dw * row(6, h))
        c_ref[0, 0, h] = c_new
        n_new = iw * n0_ref[0, 0, h:h + 1, :] + dw * row(5, h)
        n_ref[0, 0, h:h + 1, :] = n_new
        num = jnp.sum(col(2, h) * c_new, axis=0, keepdims=True)
        den = jnp.sum(row(4, h) * n_new, axis=1, keepdims=True)
        hid = num * (1.0 / jnp.maximum(jnp.abs(den), emt))
        mix_ref[pl.ds(b, 1), WIDTH + lo:WIDTH + lo + HEAD_DIM] = (
            _head_norm(hid) * mgn_ref[:, lo:lo + HEAD_DIM] * row(MO_GROUP, h))

    return [lambda h=h: chunk(h) for h in range(HEADS)]


def _outproj_kernel(a_ref, w_ref, x_ref, g_ref, b_ref, x1_ref, x1b_ref, *wb_out):
    if wb_out:
        wb_out[0][...] = w_ref[...].astype(BF16)
        w_ref = wb_out[0]
    tm = a_ref.shape[0]
    for rows in _row_splits(tm):
        mix = jnp.dot(a_ref[rows, :].astype(BF16), w_ref[...], preferred_element_type=F32)
        x1 = _layer_norm(ALPHA * x_ref[rows, :] + mix, g_ref[...], b_ref[...])
        x1_ref[rows, :] = x1
        x1b_ref[rows, :] = x1.astype(BF16)


def _outproj(a, w, x, g, b, *, tm):
    m = a.shape[0]
    cast_w = w.dtype != BF16
    assert not cast_w or m == tm
    row = pl.BlockSpec((tm, D_MODEL), lambda i: (i, 0))
    vec = pl.BlockSpec((1, D_MODEL), lambda i: (0, 0))
    out_specs = [row, row]
    out_shape = [jax.ShapeDtypeStruct((m, D_MODEL), F32), jax.ShapeDtypeStruct((m, D_MODEL), BF16)]
    if cast_w:
        out_specs.append(pl.BlockSpec((D_MODEL, D_MODEL), lambda i: (0, 0)))
        out_shape.append(jax.ShapeDtypeStruct((D_MODEL, D_MODEL), BF16))
    return pl.pallas_call(
        _outproj_kernel,
        grid=(m // tm,),
        in_specs=[row, pl.BlockSpec((D_MODEL, D_MODEL), lambda i: (0, 0), pipeline_mode=pl.Buffered(1)),
                  row, vec, vec],
        out_specs=out_specs,
        out_shape=out_shape,
        compiler_params=_params("arbitrary"),
        name="outproj_ln1_cast" if cast_w else "outproj_ln1",
    )(a, w, x, g, b)


def _ffn_kernel(x1b_ref, w1_ref, w2_ref, o_ref, *wb_out):
    @pl.when(pl.program_id(1) == 0)
    def _():
        o_ref[...] = jnp.zeros_like(o_ref)

    w1 = w1_ref[...]
    w2 = w2_ref[...]
    if wb_out:
        w1 = w1.astype(BF16)
        w2 = w2.astype(BF16)
        wb_out[0][...] = w1
        wb_out[1][...] = w2
    hid = jnp.dot(x1b_ref[...], w1, preferred_element_type=F32)
    hid = jnp.square(jnp.maximum(hid, 0.0)).astype(BF16)
    o_ref[...] += jnp.dot(hid, w2, preferred_element_type=F32)


def _ffn(x1b, w1, w2, *, tm, tf):
    m = x1b.shape[0]
    cast_w = w1.dtype != BF16
    assert not cast_w or m == tm
    row = pl.BlockSpec((tm, D_MODEL), lambda i, f: (i, 0))
    w1spec = pl.BlockSpec((D_MODEL, tf), lambda i, f: (0, f))
    w2spec = pl.BlockSpec((tf, D_MODEL), lambda i, f: (f, 0))
    oshape = jax.ShapeDtypeStruct((m, D_MODEL), F32)
    return pl.pallas_call(
        _ffn_kernel,
        grid=(m // tm, D_FF // tf),
        in_specs=[row, w1spec, w2spec],
        out_specs=[row, w1spec, w2spec] if cast_w else row,
        out_shape=[oshape, jax.ShapeDtypeStruct(w1.shape, BF16),
                   jax.ShapeDtypeStruct(w2.shape, BF16)] if cast_w else oshape,
        compiler_params=_params("arbitrary", "arbitrary"),
        name="ffn_cast" if cast_w else "ffn",
    )(x1b, w1, w2)


def _pe_ln2_kernel(x1_ref, x1b_ref, p_ref, ff_ref, wpe_ref, wg_ref, g_ref, b_ref, o_ref, *wb_out):
    if wb_out:
        wb_out[0][...] = wpe_ref[...].astype(BF16)
        wb_out[1][...] = wg_ref[...].astype(BF16)
        wpe_ref, wg_ref = wb_out
    for rows in _row_splits(x1_ref.shape[0]):
        gate = _sigmoid(jnp.dot(x1b_ref[rows, :], wg_ref[...], preferred_element_type=F32))
        pe = jnp.dot(p_ref[rows, :].astype(BF16), wpe_ref[...], preferred_element_type=F32)
        y = ALPHA * x1_ref[rows, :] + ff_ref[rows, :] + pe * gate
        o_ref[rows, :] = _layer_norm(y, g_ref[...], b_ref[...])


def _pe_ln2(x1, x1b, p, ff, wpe, wg, g, b, *, tm):
    m = x1.shape[0]
    cast_w = wg.dtype != BF16
    assert not cast_w or m == tm
    row = pl.BlockSpec((tm, D_MODEL), lambda i: (i, 0))
    vec = pl.BlockSpec((1, D_MODEL), lambda i: (0, 0))
    oshape = jax.ShapeDtypeStruct((m, D_MODEL), F32)
    wb_specs = [pl.BlockSpec((PLE_DIM, D_MODEL), lambda i: (0, 0)),
                pl.BlockSpec((D_MODEL, D_MODEL), lambda i: (0, 0))]
    wb_shapes = [jax.ShapeDtypeStruct(wpe.shape, BF16), jax.ShapeDtypeStruct(wg.shape, BF16)]
    return pl.pallas_call(
        _pe_ln2_kernel,
        grid=(m // tm,),
        in_specs=[row, row, pl.BlockSpec((tm, PLE_DIM), lambda i: (i, 0)), row,
                  pl.BlockSpec((PLE_DIM, D_MODEL), lambda i: (0, 0), pipeline_mode=pl.Buffered(1)),
                  pl.BlockSpec((D_MODEL, D_MODEL), lambda i: (0, 0), pipeline_mode=pl.Buffered(1)),
                  vec, vec],
        out_specs=[row] + wb_specs if cast_w else row,
        out_shape=[oshape] + wb_shapes if cast_w else oshape,
        compiler_params=_params("arbitrary"),
        name="pe_ln2_cast" if cast_w else "pe_ln2",
    )(x1, x1b, p, ff, wpe, wg, g, b)


def _rope_tables(pos):
    inv = ROPE_BASE ** (-jnp.arange(HALF, dtype=F32) * 2.0 / HEAD_DIM)
    ang = pos[:, None] * inv[None, :]
    return jnp.cos(ang), jnp.sin(ang)


def _decay_tables():
    lg = jnp.log(1.0 - 2.0 ** (-5.0 - jnp.arange(HEADS, dtype=F32)))
    t = jnp.arange(CHUNK, dtype=F32)
    causal = t[:, None] >= t[None, :]
    intra = jnp.exp(jnp.where(causal, (t[:, None] - t[None, :]) * lg[:, None, None], -jnp.inf))
    q_decay = jnp.exp(lg[:, None] * (t + 1.0))
    k_decay = jnp.exp(lg[:, None] * (CHUNK - 1.0 - t))
    return intra, jnp.broadcast_to(q_decay[:, :, None], (HEADS, CHUNK, HEAD_DIM)), k_decay


def _tail(x, mixed, p, w, *, tm, tm_ffn, tf):
    casts = {}
    res = _outproj(mixed, w["out"], x, w["ln1_g"], w["ln1_b"], tm=tm)
    x1, x1b = res[0], res[1]
    if len(res) > 2:
        casts["out"] = res[2]
    res = _ffn(x1b, w["ff1"], w["ff2"], tm=tm_ffn, tf=tf)
    if isinstance(res, (list, tuple)):
        ff, casts["ff1"], casts["ff2"] = res
    else:
        ff = res
    res = _pe_ln2(x1, x1b, p, ff, w["pe"], w["pe_gate"], w["ln2_g"], w["ln2_b"], tm=tm)
    if isinstance(res, (list, tuple)):
        y, casts["pe"], casts["pe_gate"] = res
    else:
        y = res
    return y, casts


def kernel(x_prompt, x_sample, state_ret, state_mlstm_C, state_mlstm_n, state_mlstm_m, p_prompt, p_sample, w_in, b_gate, ret_gn_w, mlstm_gn_w, w_out, ln1_g, ln1_b, w_ff1, w_ff2, w_pe, w_pe_gate, ln2_g, ln2_b):
    batch, seq, _ = x_prompt.shape
    nb = x_sample.shape[0]

    w_in_t = w_in[0].T
    b8 = jnp.pad(b_gate[0].astype(F32), (0, LANES - 2 * HEADS)).reshape(1, LANES)
    w = {
        "out": w_out[0], "ff1": w_ff1[0], "ff2": w_ff2[0], "pe": w_pe[0], "pe_gate": w_pe_gate[0],
        "ln1_g": ln1_g[0].reshape(1, D_MODEL), "ln1_b": ln1_b[0].reshape(1, D_MODEL),
        "ln2_g": ln2_g[0].reshape(1, D_MODEL), "ln2_b": ln2_b[0].reshape(1, D_MODEL),
    }
    rgn = ret_gn_w[0].reshape(1, WIDTH)
    mgn = mlstm_gn_w[0].reshape(1, WIDTH)
    intra, qdec, kdec = _decay_tables()

    xs = x_sample.reshape(nb, D_MODEL)
    xsb, gs_tok, _ = _mgate(xs, w_in_t, b8, tm=nb)
    ctab, stab = _rope_tables(jnp.full((nb,), PAST_LEN, dtype=F32))
    proj_s, w_in_b = _inproj(xsb, w_in_t, ctab, stab, tm=nb)
    proj_s = proj_s.astype(F32)
    m0 = jnp.pad(state_mlstm_m[0], ((0, 0), (0, LANES - HEADS)))
    mt, dw, iw, emt = _sample_gates(gs_tok, m0)
    scal = jnp.concatenate([mt[:, :HEADS], dw[:, :HEADS], iw[:, :HEADS], emt[:, :HEADS]], axis=1)
    m_s = mt[:, :HEADS].reshape(1, nb, HEADS)

    xp = x_prompt.reshape(batch * seq, D_MODEL)
    xpb, g_tok, g_t = _mgate(xp, w_in_t, b8, tm=GATE_TM)
    ctab, stab = _rope_tables(jnp.arange(seq, dtype=F32))
    qk = jnp.concatenate([proj_s[:, 0:2 * WIDTH], proj_s[:, 4 * WIDTH:6 * WIDTH]], axis=1)
    cols = jnp.transpose(qk.reshape(nb, 4 * HEADS, HEAD_DIM), (0, 2, 1))
    proj_p, mixed_s, s_s, c_s, n_s = _inproj(
        xpb, w_in_b, ctab, stab, tm=INPROJ_TM,
        sample=(scal, cols, proj_s, state_ret, state_mlstm_C, state_mlstm_n, rgn, mgn))
    mixed_p, s_p, c_p, n_p, m_rows = _prompt_mixer(proj_p, g_tok, g_t, intra, qdec, kdec, rgn, mgn,
                                                   batch=batch, seq=seq)
    m_p = m_rows[:, :HEADS, 0].reshape(1, batch, HEADS)

    y_s, w_bf16 = _tail(xs, mixed_s, p_sample[0].reshape(nb, PLE_DIM), w,
                        tm=nb, tm_ffn=nb, tf=CAST_FFN_TF)
    w = {**w, **w_bf16}
    y_p, _ = _tail(xp, mixed_p, p_prompt[0].reshape(batch * seq, PLE_DIM), w,
                   tm=TAIL_TM, tm_ffn=FFN_TM, tf=FFN_TF)

    return (y_p.reshape(batch, seq, D_MODEL), y_s.reshape(nb, 1, D_MODEL),
            s_p, c_p, n_p, m_p, s_s, c_s, n_s, m_s)
```

```python
import jax
import jax.numpy as jnp
import numpy as np
from jax import lax
from jax.experimental import pallas as pl
from jax.experimental.pallas import tpu as pltpu

F32 = jnp.float32
BF16 = jnp.bfloat16

D_MODEL = 2048
HEADS = 4
HEAD_DIM = 256
HALF = HEAD_DIM // 2
WIDTH = HEADS * HEAD_DIM
D_FF = 4 * D_MODEL
PLE_DIM = 256
CHUNK = 128
PAST_LEN = 16384
ROPE_BASE = 10000.0
LN_EPS = 1e-5
GN_EPS = 1e-6
DEPTH = 1
ALPHA = (2 * DEPTH) ** 0.25
QK_SCALE = HEAD_DIM ** -0.5
LANES = 128
SUBLANES = 8
MXU_ROWS = 256
VMEM_LIMIT = 56 * 1024 * 1024
MIXERS_VMEM_LIMIT = 60 * 1024 * 1024
SAMPLE_TOKENS_PER_STEP = 3
HOSTED_TOKENS = 32
MIXER_CHUNKS_PER_STEP = 2
GATE_TM = 1024
INPROJ_TM = 2048
TAIL_TM = 512
FFN_TM = 1024
FFN_TF = 1024
CAST_FFN_TF = 512

RET_G = tuple(1.0 - 2.0 ** (-5.0 - h) for h in range(HEADS))
RET_LOG_G = tuple(float(np.log(np.float32(g))) for g in RET_G)
RET_STATE_DECAY = tuple(float(np.exp(np.float32(lg) * np.float32(CHUNK))) for lg in RET_LOG_G)

_NT = (((1,), (1,)), ((), ()))
_TN = (((0,), (0,)), ((), ()))


def _params(*sem, vmem=None):
    return pltpu.CompilerParams(dimension_semantics=sem, vmem_limit_bytes=vmem or VMEM_LIMIT)


def _sigmoid(x):
    return 0.5 * jnp.tanh(0.5 * x) + 0.5


def _log_sigmoid(x):
    return jnp.minimum(x, 0.0) - jnp.log(1.0 + jnp.exp(-jnp.abs(x)))


def _layer_norm(y, g, b):
    mu = jnp.mean(y, axis=-1, keepdims=True)
    yc = y - mu
    var = jnp.mean(jnp.square(yc), axis=-1, keepdims=True)
    return yc * lax.rsqrt(var + LN_EPS) * g + b


def _row_splits(tm):
    if tm < 2 * MXU_ROWS:
        return [slice(0, tm)]
    return [slice(0, tm // 2), slice(tm // 2, tm)]


def _head_norm(o):
    mu = jnp.mean(o, axis=-1, keepdims=True)
    oc = o - mu
    var = jnp.mean(jnp.square(oc), axis=-1, keepdims=True)
    return oc * lax.rsqrt(var + GN_EPS)


RG_GROUP, MO_GROUP = 3, 7


def _inproj_body(x_ref, wb_ref, c_ref, s_ref, o_ref, hosted=()):
    j = pl.program_id(0)
    is_gate = (j == RG_GROUP) | (j == MO_GROUP)
    pieces = [(h, rows) for h in range(HEADS) for rows in _row_splits(x_ref.shape[0])]

    def piece_acc(h, rows):
        return lax.dot_general(x_ref[rows, :], wb_ref[h * HEAD_DIM:(h + 1) * HEAD_DIM, :], _NT,
                               preferred_element_type=F32)

    def gate_piece(h, rows):
        acc = piece_acc(h, rows)
        sig = _sigmoid(acc)
        o_ref[rows, h * HEAD_DIM:(h + 1) * HEAD_DIM] = jnp.where(
            j == RG_GROUP, acc * sig, sig).astype(o_ref.dtype)

    def qkv_piece(h, rows):
        rotary = j <= 1
        scale = jnp.where((j == 1) | (j == MO_GROUP - 2), QK_SCALE, 1.0).astype(F32)
        lo = h * HEAD_DIM
        c = jnp.where(rotary, c_ref[rows, :], 1.0) * scale
        s = jnp.where(rotary, s_ref[rows, :], 0.0) * scale
        acc = piece_acc(h, rows)
        x1 = acc[:, :HALF]
        x2 = acc[:, HALF:]
        o_ref[rows, lo:lo + HALF] = (x1 * c - x2 * s).astype(o_ref.dtype)
        o_ref[rows, lo + HALF:lo + HEAD_DIM] = (x1 * s + x2 * c).astype(o_ref.dtype)

    def run(piece):
        for n, (h, rows) in enumerate(pieces):
            if n < len(hosted):
                hosted[n]()
            piece(h, rows)

    pl.when(is_gate)(lambda: run(gate_piece))
    pl.when(jnp.logical_not(is_gate))(lambda: run(qkv_piece))


def _inproj_cast_kernel(x_ref, wt_ref, c_ref, s_ref, o_ref, wb_ref):
    wb_ref[...] = wt_ref[...].astype(BF16)
    _inproj_body(x_ref, wb_ref, c_ref, s_ref, o_ref)


def _inproj_host_kernel(x_ref, wt_ref, c_ref, s_ref, scal_ref, cols_ref, rows_ref, s0_ref, c0_ref,
                        n0_ref, rgn_ref, mgn_ref, o_ref, smix_ref, ss_ref, sc_ref, sn_ref):
    local = pl.program_id(0) * pl.num_programs(1) + pl.program_id(1)
    first = scal_ref.shape[0] - rows_ref.shape[0]
    hosted = _sample_token_chunks(first + local, local, scal_ref, cols_ref, rows_ref, s0_ref, c0_ref,
                                  n0_ref, rgn_ref, mgn_ref, smix_ref, ss_ref, sc_ref, sn_ref)
    _inproj_body(x_ref, wt_ref, c_ref, s_ref, o_ref, hosted)


def _inproj(xb, wt, cos, sin, *, tm, sample=None):
    m = xb.shape[0]
    n_i = m // tm
    n_pos = cos.shape[0] // tm
    wspec = pl.BlockSpec((WIDTH, D_MODEL), lambda j, i: (j, 0))
    ospec = pl.BlockSpec((tm, WIDTH), lambda j, i: (i, j))
    oshape = jax.ShapeDtypeStruct((m, 8 * WIDTH), BF16)
    tspec = pl.BlockSpec((tm, HALF), lambda j, i: (i % n_pos, 0))
    in_specs = [pl.BlockSpec((tm, D_MODEL), lambda j, i: (i, 0)), wspec, tspec, tspec]
    if wt.dtype != BF16:
        assert n_i == 1 and sample is None
        return pl.pallas_call(
            _inproj_cast_kernel,
            grid=(8, 1),
            in_specs=in_specs,
            out_specs=[ospec, wspec],
            out_shape=[oshape, jax.ShapeDtypeStruct((8 * WIDTH, D_MODEL), BF16)],
            compiler_params=_params("arbitrary", "arbitrary"),
            name="inproj_cast",
        )(xb, wt, cos, sin)
    scal, cols_s, rows_s, s0, c0, n0, rgn, mgn = sample
    nh = rows_s.shape[0]
    first = scal.shape[0] - nh
    assert nh == 8 * n_i
    const2 = lambda j, i: (0, 0)
    sstate = pl.BlockSpec((1, 1, HEADS, HEAD_DIM, HEAD_DIM),
                          lambda j, i: (0, first + j * n_i + i, 0, 0, 0))
    snorm = pl.BlockSpec((1, 1, HEADS, HEAD_DIM), lambda j, i: (0, first + j * n_i + i, 0, 0))
    return pl.pallas_call(
        _inproj_host_kernel,
        grid=(8, n_i),
        in_specs=in_specs + [
            pl.BlockSpec(memory_space=pltpu.SMEM),
            pl.BlockSpec((1, HEAD_DIM, 4 * HEADS), lambda j, i: (j * n_i + i, 0, 0)),
            pl.BlockSpec((nh, 8 * WIDTH), const2, pipeline_mode=pl.Buffered(1)),
            sstate, sstate, snorm,
            pl.BlockSpec((1, WIDTH), const2),
            pl.BlockSpec((1, WIDTH), const2),
        ],
        out_specs=[ospec, pl.BlockSpec((nh, 2 * WIDTH), const2), sstate, sstate, snorm],
        out_shape=[oshape, jax.ShapeDtypeStruct((nh, 2 * WIDTH), F32),
                   jax.ShapeDtypeStruct(s0.shape, F32), jax.ShapeDtypeStruct(c0.shape, F32),
                   jax.ShapeDtypeStruct(n0.shape, F32)],
        compiler_params=_params("arbitrary", "arbitrary"),
        name="inproj",
    )(xb, wt, cos, sin, scal, cols_s, rows_s, s0, c0, n0, rgn, mgn)


def _mgate_kernel(x_ref, wt_ref, b_ref, xb_ref, g_ref, gt_ref):
    xb = x_ref[...].astype(BF16)
    xb_ref[...] = xb
    wt = jnp.concatenate([wt_ref[...], jnp.zeros((LANES - 2 * HEADS, D_MODEL), F32)], axis=0)
    g = lax.dot_general(xb, wt.astype(BF16), _NT, preferred_element_type=F32) + b_ref[...]
    lane = lax.broadcasted_iota(jnp.int32, g.shape, 1)
    g = jnp.where((lane >= HEADS) & (lane < 2 * HEADS), _log_sigmoid(g), g)
    r = lax.broadcasted_iota(jnp.int32, (CHUNK, CHUNK), 0)
    s = lax.broadcasted_iota(jnp.int32, (CHUNK, CHUNK), 1)
    tril = (r >= s).astype(BF16)
    g1 = g.astype(BF16)
    res = g - g1.astype(F32)
    g2 = res.astype(BF16)
    g3 = (res - g2.astype(F32)).astype(BF16)
    parts = []
    for c in range(g.shape[0] // CHUNK):
        sl = slice(c * CHUNK, (c + 1) * CHUNK)
        parts.append(jnp.dot(tril, g1[sl], preferred_element_type=F32)
                     + jnp.dot(tril, g2[sl], preferred_element_type=F32)
                     + jnp.dot(tril, g3[sl], preferred_element_type=F32))
    cs = parts[0] if len(parts) == 1 else jnp.concatenate(parts, axis=0)
    b = pltpu.roll(cs, HEADS, 1)
    u = pltpu.roll(g, 3 * HEADS, 1) - pltpu.roll(cs, 2 * HEADS, 1)
    g = jnp.where((lane >= 2 * HEADS) & (lane < 3 * HEADS), b, g)
    g = jnp.where((lane >= 3 * HEADS) & (lane < 4 * HEADS), u, g)
    g_ref[...] = g
    gt_ref[...] = g.T[0:4 * HEADS, :]


def _mgate(x, w_in_t, b8, *, tm):
    m = x.shape[0]
    gate_rows = 8 * WIDTH // (2 * HEADS)
    return pl.pallas_call(
        _mgate_kernel,
        grid=(m // tm,),
        in_specs=[
            pl.BlockSpec((tm, D_MODEL), lambda i: (i, 0)),
            pl.BlockSpec((2 * HEADS, D_MODEL), lambda i: (gate_rows, 0)),
            pl.BlockSpec((1, LANES), lambda i: (0, 0)),
        ],
        out_specs=[
            pl.BlockSpec((tm, D_MODEL), lambda i: (i, 0)),
            pl.BlockSpec((tm, LANES), lambda i: (i, 0)),
            pl.BlockSpec((4 * HEADS, tm), lambda i: (0, i)),
        ],
        out_shape=[
            jax.ShapeDtypeStruct((m, D_MODEL), BF16),
            jax.ShapeDtypeStruct((m, LANES), F32),
            jax.ShapeDtypeStruct((4 * HEADS, m), F32),
        ],
        compiler_params=_params("arbitrary"),
        name="mlstm_gates",
    )(x, w_in_t, b8)


def _mixer_chunk(rows, causal, qkv_ref, g_ref, gt_ref, intra_ref, dec_ref, kdec_ref, rgn_ref, mgn_ref,
                 mix_ref, s_ref, c_ref, n_ref, m_scr):
    def head(group, h):
        lo = group * WIDTH + h * HEAD_DIM
        return qkv_ref[rows, lo:lo + HEAD_DIM]

    gw = []
    for h in range(HEADS):
        b_col = g_ref[rows, 2 * HEADS + h:2 * HEADS + h + 1]
        u_row = gt_ref[3 * HEADS + h:3 * HEADS + h + 1, rows]
        m_prev = m_scr[h:h + 1, 0:1]
        dlog = jnp.where(causal, b_col + u_row, -jnp.inf)
        inter = b_col + m_prev
        mt = jnp.maximum(inter, jnp.max(dlog, axis=1, keepdims=True))
        m_new = mt[CHUNK - 1:CHUNK, :]
        b_last = b_col[CHUNK - 1:CHUNK, :]
        gw.append(dict(
            dw=jnp.exp(dlog - mt), iw=jnp.exp(inter - mt), emt=jnp.exp(-mt),
            sw=jnp.exp(b_last - m_new + u_row), sd=jnp.exp(b_last + m_prev - m_new)))
        m_scr[h:h + 1, :] = jnp.broadcast_to(m_new, (1, LANES))

    st = []
    for h in range(HEADS):
        for kind, (qg, state_ref) in enumerate(((0, s_ref), (MO_GROUP - 3, c_ref))):
            q = head(qg, h)
            k = head(qg + 1, h)
            old = state_ref[0, 0, h]
            kt = k.T
            st.append(dict(
                h=h, kind=kind, q=q, k=k, kt=kt, v=head(qg + 2, h), old=old,
                sc=jnp.dot(q, kt, preferred_element_type=F32),
                inter=jnp.dot(q, old.astype(BF16), preferred_element_type=F32)))

    for e in st:
        h = e["h"]
        if e["kind"] == 0:
            kd = (e["kt"].astype(F32) * kdec_ref[h:h + 1, :]).astype(BF16)
            s_ref[0, 0, h] = (e["old"] * RET_STATE_DECAY[h]
                              + jnp.dot(kd, e["v"], preferred_element_type=F32))
        else:
            w = gw[h]
            ksw = (e["kt"].astype(F32) * w["sw"]).astype(BF16)
            c_ref[0, 0, h] = w["sd"] * e["old"] + jnp.dot(ksw, e["v"], preferred_element_type=F32)
            n_old = n_ref[0, 0, h:h + 1, :]
            e["qn"] = jnp.sum(e["q"].astype(F32) * n_old, axis=1, keepdims=True)
            sw8 = jnp.broadcast_to(w["sw"], (SUBLANES, CHUNK)).astype(BF16)
            n_ref[0, 0, h:h + 1, :] = (w["sd"] * n_old
                                       + jnp.dot(sw8, e["k"], preferred_element_type=F32)[0:1, :])

    for e in st:
        h = e["h"]
        e["sc"] = e["sc"] * (intra_ref[h] if e["kind"] == 0 else gw[h]["dw"])
        e["pv"] = jnp.dot(e["sc"].astype(BF16), e["v"], preferred_element_type=F32)

    for e in st:
        e["rows"] = rows
        if e["kind"]:
            e["iw"] = gw[e["h"]]["iw"]
            e["emt"] = gw[e["h"]]["emt"]
    return st


def _mixer_tail(streams, qkv_ref, dec_ref, rgn_ref, mgn_ref, mix_ref):
    dens = [jnp.sum(e["sc"], axis=1, keepdims=True) if e["kind"] else None for e in streams]
    outs = []
    for e, den in zip(streams, dens):
        if e["kind"] == 0:
            outs.append(e["pv"] + e["inter"] * dec_ref[e["h"]])
        else:
            num = e["pv"] + e["iw"] * e["inter"]
            den = den + e["iw"] * e["qn"]
            outs.append(num * (1.0 / jnp.maximum(jnp.abs(den), e["emt"])))
    mus = [jnp.mean(o, axis=-1, keepdims=True) for o in outs]
    cen = [o - mu for o, mu in zip(outs, mus)]
    var = [jnp.mean(jnp.square(oc), axis=-1, keepdims=True) for oc in cen]
    for e, oc, v in zip(streams, cen, var):
        rows = e["rows"]
        lo = e["h"] * HEAD_DIM
        gate_lo = (MO_GROUP if e["kind"] else RG_GROUP) * WIDTH + lo
        gain_ref = mgn_ref if e["kind"] else rgn_ref
        out_lo = e["kind"] * WIDTH + lo
        y = (oc * lax.rsqrt(v + GN_EPS) * gain_ref[:, lo:lo + HEAD_DIM]
             * qkv_ref[rows, gate_lo:gate_lo + HEAD_DIM].astype(F32))
        mix_ref[rows, out_lo:out_lo + HEAD_DIM] = y.astype(mix_ref.dtype)


def _mixers_kernel(scal_ref, rows_ref, s0_ref, c0_ref, n0_ref,
                   qkv_ref, g_ref, gt_ref, intra_ref, dec_ref, kdec_ref, rgn_ref, mgn_ref,
                   sh_ref, ch_ref, nh_ref,
                   smix_ref, ss_ref, sc_ref, sn_ref,
                   mix_ref, s_ref, c_ref, n_ref, m_ref, m_scr):
    del sh_ref, ch_ref, nh_ref
    c = pl.program_id(1)
    _sample_tokens(pl.program_id(0) * pl.num_programs(1) + c, scal_ref, rows_ref, s0_ref, c0_ref,
                   n0_ref, rgn_ref, mgn_ref, smix_ref, ss_ref, sc_ref, sn_ref)

    @pl.when(c == 0)
    def _():
        s_ref[...] = jnp.zeros_like(s_ref)
        c_ref[...] = jnp.zeros_like(c_ref)
        n_ref[...] = jnp.zeros_like(n_ref)
        m_scr[...] = jnp.zeros_like(m_scr)

    ti = lax.broadcasted_iota(jnp.int32, (CHUNK, CHUNK), 0)
    si = lax.broadcasted_iota(jnp.int32, (CHUNK, CHUNK), 1)
    causal = ti >= si

    streams = []
    for ci in range(MIXER_CHUNKS_PER_STEP):
        streams += _mixer_chunk(slice(ci * CHUNK, (ci + 1) * CHUNK), causal, qkv_ref, g_ref, gt_ref,
                                intra_ref, dec_ref, kdec_ref, rgn_ref, mgn_ref, mix_ref, s_ref, c_ref,
                                n_ref, m_scr)
    _mixer_tail(streams, qkv_ref, dec_ref, rgn_ref, mgn_ref, mix_ref)

    @pl.when(c == pl.num_programs(1) - 1)
    def _():
        m_ref[0] = m_scr[...]


def _mixers(scal, rows_s, s0, c0, n0, qkv, g_tok, g_t, intra, qdec, kdec, rgn, mgn, hosted, *,
            batch, seq):
    rows = MIXER_CHUNKS_PER_STEP * CHUNK
    nc = seq // rows
    nb = rows_s.shape[0]
    tb = SAMPLE_TOKENS_PER_STEP
    nm = batch * nc * tb
    assert nm <= nb
    any_spec = pl.BlockSpec(memory_space=pl.ANY)
    row = lambda b, c: (b * nc + c, 0)
    const2 = lambda b, c: (0, 0)
    state = pl.BlockSpec((1, 1, HEADS, HEAD_DIM, HEAD_DIM), lambda b, c: (0, b, 0, 0, 0))
    sstate = pl.BlockSpec((1, tb, HEADS, HEAD_DIM, HEAD_DIM), lambda b, c: (0, b * nc + c, 0, 0, 0))
    snorm = pl.BlockSpec((1, tb, HEADS, HEAD_DIM), lambda b, c: (0, b * nc + c, 0, 0))
    return pl.pallas_call(
        _mixers_kernel,
        grid=(batch, nc),
        in_specs=[
            pl.BlockSpec(memory_space=pltpu.SMEM),
            pl.BlockSpec((nb, 8 * WIDTH), const2, pipeline_mode=pl.Buffered(1)),
            sstate, sstate, snorm,
            pl.BlockSpec((rows, 8 * WIDTH), row),
            pl.BlockSpec((rows, LANES), row),
            pl.BlockSpec((4 * HEADS, rows), lambda b, c: (0, b * nc + c)),
            pl.BlockSpec((HEADS, CHUNK, CHUNK), lambda b, c: (0, 0, 0)),
            pl.BlockSpec((HEADS, CHUNK, HEAD_DIM), lambda b, c: (0, 0, 0)),
            pl.BlockSpec((HEADS, CHUNK), const2),
            pl.BlockSpec((1, WIDTH), const2),
            pl.BlockSpec((1, WIDTH), const2),
            any_spec, any_spec, any_spec,
        ],
        out_specs=[
            pl.BlockSpec((nm, 2 * WIDTH), const2), sstate, sstate, snorm,
            pl.BlockSpec((rows, 2 * WIDTH), row),
            state,
            state,
            pl.BlockSpec((1, 1, HEADS, HEAD_DIM), lambda b, c: (0, b, 0, 0)),
            pl.BlockSpec((1, SUBLANES, LANES), lambda b, c: (b, 0, 0)),
        ],
        out_shape=[
            jax.ShapeDtypeStruct((nm, 2 * WIDTH), F32),
            jax.ShapeDtypeStruct(s0.shape, F32),
            jax.ShapeDtypeStruct(c0.shape, F32),
            jax.ShapeDtypeStruct(n0.shape, F32),
            jax.ShapeDtypeStruct((batch * seq, 2 * WIDTH), BF16),
            jax.ShapeDtypeStruct((1, batch, HEADS, HEAD_DIM, HEAD_DIM), F32),
            jax.ShapeDtypeStruct((1, batch, HEADS, HEAD_DIM, HEAD_DIM), F32),
            jax.ShapeDtypeStruct((1, batch, HEADS, HEAD_DIM), F32),
            jax.ShapeDtypeStruct((batch, SUBLANES, LANES), F32),
        ],
        scratch_shapes=[pltpu.VMEM((SUBLANES, LANES), F32)],
        input_output_aliases={13: 1, 14: 2, 15: 3},
        compiler_params=_params("arbitrary", "arbitrary", vmem=MIXERS_VMEM_LIMIT),
        name="mixers",
    )(scal, rows_s, s0, c0, n0, qkv, g_tok, g_t, intra, qdec, kdec, rgn, mgn, *hosted)


def _sample_gate_kernel(g_ref, m0_ref, mt_ref, dw_ref, iw_ref, emt_ref):
    ig = g_ref[...]
    lf = pltpu.roll(ig, LANES - HEADS, 1)
    inter = lf + m0_ref[...]
    mt = jnp.maximum(inter, ig)
    mt_ref[...] = mt
    dw_ref[...] = jnp.exp(ig - mt)
    iw_ref[...] = jnp.exp(inter - mt)
    emt_ref[...] = jnp.exp(-mt)


def _sample_gates(g_tok, m0_pad):
    m = g_tok.shape[0]
    spec = pl.BlockSpec((m, LANES), lambda: (0, 0))
    return pl.pallas_call(
        _sample_gate_kernel,
        in_specs=[spec, spec],
        out_specs=[spec] * 4,
        out_shape=[jax.ShapeDtypeStruct((m, LANES), F32)] * 4,
        name="sample_gates",
    )(g_tok, m0_pad)


def _sample_tokens(step, scal_ref, rows_ref, s0_ref, c0_ref, n0_ref,
                   rgn_ref, mgn_ref, mix_ref, s_ref, c_ref, n_ref):
    first = lax.broadcasted_iota(jnp.int32, (4 * HEADS, HEAD_DIM), 0) == 0
    for t in range(SAMPLE_TOKENS_PER_STEP):
        b = step * SAMPLE_TOKENS_PER_STEP + t

        def row(group, h):
            lo = group * WIDTH + h * HEAD_DIM
            return rows_ref[pl.ds(b, 1), lo:lo + HEAD_DIM]

        def outer(k, v):
            kp = jnp.where(first, jnp.broadcast_to(k, first.shape), 0.0).astype(BF16)
            vp = jnp.where(first, jnp.broadcast_to(v, first.shape), 0.0).astype(BF16)
            return lax.dot_general(kp, vp, _TN, preferred_element_type=F32)

        def apply(q_row, state):
            q8 = jnp.broadcast_to(q_row, (SUBLANES, HEAD_DIM)).astype(BF16)
            return jnp.dot(q8, state.astype(BF16), preferred_element_type=F32)[0:1, :]

        dw = [scal_ref[b, HEADS + h] for h in range(HEADS)]
        iw = [scal_ref[b, 2 * HEADS + h] for h in range(HEADS)]
        emt = [scal_ref[b, 3 * HEADS + h] for h in range(HEADS)]
        kv_s = [outer(row(1, h), row(2, h)) for h in range(HEADS)]
        kv_c = [outer(row(5, h), dw[h] * row(6, h)) for h in range(HEADS)]
        s_new, c_new = [], []
        for h in range(HEADS):
            s_new.append(RET_G[h] * s0_ref[0, t, h] + kv_s[h])
            s_ref[0, t, h] = s_new[h]
            c_new.append(iw[h] * c0_ref[0, t, h] + kv_c[h])
            c_ref[0, t, h] = c_new[h]
        o = [apply(row(0, h), s_new[h]) for h in range(HEADS)]
        num = [apply(row(4, h), c_new[h]) for h in range(HEADS)]
        for h in range(HEADS):
            lo = h * HEAD_DIM
            mix_ref[pl.ds(b, 1), lo:lo + HEAD_DIM] = (_head_norm(o[h]) * rgn_ref[:, lo:lo + HEAD_DIM]
                                               * row(RG_GROUP, h))
            n_new = iw[h] * n0_ref[0, t, h:h + 1, :] + dw[h] * row(5, h)
            n_ref[0, t, h:h + 1, :] = n_new
            den = jnp.sum(row(4, h) * n_new, axis=1, keepdims=True)
            hid = num[h] * (1.0 / jnp.maximum(jnp.abs(den), emt[h]))
            mix_ref[pl.ds(b, 1), WIDTH + lo:WIDTH + lo + HEAD_DIM] = (
                _head_norm(hid) * mgn_ref[:, lo:lo + HEAD_DIM] * row(MO_GROUP, h))


def _sample_token_chunks(b, local, scal_ref, cols_ref, rows_ref, s0_ref, c0_ref, n0_ref, rgn_ref,
                         mgn_ref, mix_ref, s_ref, c_ref, n_ref):
    def row(group, h):
        lo = group * WIDTH + h * HEAD_DIM
        return rows_ref[pl.ds(local, 1), lo:lo + HEAD_DIM]

    def col(group, h):
        return cols_ref[0, :, group * HEADS + h:group * HEADS + h + 1]

    def rounded(a):
        return a.astype(BF16).astype(F32)

    def chunk(h):
        lo = h * HEAD_DIM
        s_new = RET_G[h] * s0_ref[0, 0, h] + col(1, h) * row(2, h)
        s_ref[0, 0, h] = s_new
        o = jnp.sum(col(0, h) * rounded(s_new), axis=0, keepdims=True)
        mix_ref[pl.ds(local, 1), lo:lo + HEAD_DIM] = (
            _head_norm(o) * rgn_ref[:, lo:lo + HEAD_DIM] * row(RG_GROUP, h))

        dw = scal_ref[b, HEADS + h]
        iw = scal_ref[b, 2 * HEADS + h]
        emt = scal_ref[b, 3 * HEADS + h]
        c_new = iw * c0_ref[0, 0, h] + col(3, h) * rounded(dw * row(6, h))
        c_ref[0, 0, h] = c_new
        n_new = iw * n0_ref[0, 0, h:h + 1, :] + dw * row(5, h)
        n_ref[0, 0, h:h + 1, :] = n_new
        num = jnp.sum(col(2, h) * rounded(c_new), axis=0, keepdims=True)
        den = jnp.sum(row(4, h) * n_new, axis=1, keepdims=True)
        hid = num * (1.0 / jnp.maximum(jnp.abs(den), emt))
        mix_ref[pl.ds(local, 1), WIDTH + lo:WIDTH + lo + HEAD_DIM] = (
            _head_norm(hid) * mgn_ref[:, lo:lo + HEAD_DIM] * row(MO_GROUP, h))

    return [lambda h=h: chunk(h) for h in range(HEADS)]


def _outproj_kernel(a_ref, w_ref, x_ref, g_ref, b_ref, x1_ref, x1b_ref, *wb_out):
    if wb_out:
        wb_out[0][...] = w_ref[...].astype(BF16)
        w_ref = wb_out[0]
    tm = a_ref.shape[0]
    for rows in _row_splits(tm):
        mix = jnp.dot(a_ref[rows, :].astype(BF16), w_ref[...], preferred_element_type=F32)
        x1 = _layer_norm(ALPHA * x_ref[rows, :] + mix, g_ref[...], b_ref[...])
        x1_ref[rows, :] = x1
        x1b_ref[rows, :] = x1.astype(BF16)


def _outproj(a, w, x, g, b, *, tm):
    m = a.shape[0]
    cast_w = w.dtype != BF16
    assert not cast_w or m == tm
    row = pl.BlockSpec((tm, D_MODEL), lambda i: (i, 0))
    vec = pl.BlockSpec((1, D_MODEL), lambda i: (0, 0))
    out_specs = [row, row]
    out_shape = [jax.ShapeDtypeStruct((m, D_MODEL), F32), jax.ShapeDtypeStruct((m, D_MODEL), BF16)]
    if cast_w:
        out_specs.append(pl.BlockSpec((D_MODEL, D_MODEL), lambda i: (0, 0)))
        out_shape.append(jax.ShapeDtypeStruct((D_MODEL, D_MODEL), BF16))
    return pl.pallas_call(
        _outproj_kernel,
        grid=(m // tm,),
        in_specs=[row, pl.BlockSpec((D_MODEL, D_MODEL), lambda i: (0, 0), pipeline_mode=pl.Buffered(1)),
                  row, vec, vec],
        out_specs=out_specs,
        out_shape=out_shape,
        compiler_params=_params("arbitrary"),
        name="outproj_ln1_cast" if cast_w else "outproj_ln1",
    )(a, w, x, g, b)


def _ffn_kernel(x1b_ref, w1_ref, w2_ref, o_ref, *wb_out):
    @pl.when(pl.program_id(1) == 0)
    def _():
        o_ref[...] = jnp.zeros_like(o_ref)

    w1 = w1_ref[...]
    w2 = w2_ref[...]
    if wb_out:
        w1 = w1.astype(BF16)
        w2 = w2.astype(BF16)
        wb_out[0][...] = w1
        wb_out[1][...] = w2
    hid = jnp.dot(x1b_ref[...], w1, preferred_element_type=F32)
    hid = jnp.square(jnp.maximum(hid, 0.0)).astype(BF16)
    o_ref[...] += jnp.dot(hid, w2, preferred_element_type=F32)


def _ffn(x1b, w1, w2, *, tm, tf):
    m = x1b.shape[0]
    cast_w = w1.dtype != BF16
    assert not cast_w or m == tm
    row = pl.BlockSpec((tm, D_MODEL), lambda i, f: (i, 0))
    w1spec = pl.BlockSpec((D_MODEL, tf), lambda i, f: (0, f))
    w2spec = pl.BlockSpec((tf, D_MODEL), lambda i, f: (f, 0))
    oshape = jax.ShapeDtypeStruct((m, D_MODEL), F32)
    return pl.pallas_call(
        _ffn_kernel,
        grid=(m // tm, D_FF // tf),
        in_specs=[row, w1spec, w2spec],
        out_specs=[row, w1spec, w2spec] if cast_w else row,
        out_shape=[oshape, jax.ShapeDtypeStruct(w1.shape, BF16),
                   jax.ShapeDtypeStruct(w2.shape, BF16)] if cast_w else oshape,
        compiler_params=_params("arbitrary", "arbitrary"),
        name="ffn_cast" if cast_w else "ffn",
    )(x1b, w1, w2)


def _pe_ln2_kernel(x1_ref, x1b_ref, p_ref, ff_ref, wpe_ref, wg_ref, g_ref, b_ref, o_ref, *wb_out):
    if wb_out:
        wb_out[0][...] = wpe_ref[...].astype(BF16)
        wb_out[1][...] = wg_ref[...].astype(BF16)
        wpe_ref, wg_ref = wb_out
    for rows in _row_splits(x1_ref.shape[0]):
        gate = _sigmoid(jnp.dot(x1b_ref[rows, :], wg_ref[...], preferred_element_type=F32))
        pe = jnp.dot(p_ref[rows, :].astype(BF16), wpe_ref[...], preferred_element_type=F32)
        y = ALPHA * x1_ref[rows, :] + ff_ref[rows, :] + pe * gate
        o_ref[rows, :] = _layer_norm(y, g_ref[...], b_ref[...])


def _pe_ln2(x1, x1b, p, ff, wpe, wg, g, b, *, tm):
    m = x1.shape[0]
    cast_w = wg.dtype != BF16
    assert not cast_w or m == tm
    row = pl.BlockSpec((tm, D_MODEL), lambda i: (i, 0))
    vec = pl.BlockSpec((1, D_MODEL), lambda i: (0, 0))
    oshape = jax.ShapeDtypeStruct((m, D_MODEL), F32)
    wb_specs = [pl.BlockSpec((PLE_DIM, D_MODEL), lambda i: (0, 0)),
                pl.BlockSpec((D_MODEL, D_MODEL), lambda i: (0, 0))]
    wb_shapes = [jax.ShapeDtypeStruct(wpe.shape, BF16), jax.ShapeDtypeStruct(wg.shape, BF16)]
    return pl.pallas_call(
        _pe_ln2_kernel,
        grid=(m // tm,),
        in_specs=[row, row, pl.BlockSpec((tm, PLE_DIM), lambda i: (i, 0)), row,
                  pl.BlockSpec((PLE_DIM, D_MODEL), lambda i: (0, 0), pipeline_mode=pl.Buffered(1)),
                  pl.BlockSpec((D_MODEL, D_MODEL), lambda i: (0, 0), pipeline_mode=pl.Buffered(1)),
                  vec, vec],
        out_specs=[row] + wb_specs if cast_w else row,
        out_shape=[oshape] + wb_shapes if cast_w else oshape,
        compiler_params=_params("arbitrary"),
        name="pe_ln2_cast" if cast_w else "pe_ln2",
    )(x1, x1b, p, ff, wpe, wg, g, b)


def _rope_tables(pos):
    inv = ROPE_BASE ** (-jnp.arange(HALF, dtype=F32) * 2.0 / HEAD_DIM)
    ang = pos[:, None] * inv[None, :]
    return jnp.cos(ang), jnp.sin(ang)


def _decay_tables():
    lg = jnp.log(1.0 - 2.0 ** (-5.0 - jnp.arange(HEADS, dtype=F32)))
    t = jnp.arange(CHUNK, dtype=F32)
    causal = t[:, None] >= t[None, :]
    intra = jnp.exp(jnp.where(causal, (t[:, None] - t[None, :]) * lg[:, None, None], -jnp.inf))
    q_decay = jnp.exp(lg[:, None] * (t + 1.0))
    k_decay = jnp.exp(lg[:, None] * (CHUNK - 1.0 - t))
    return intra, jnp.broadcast_to(q_decay[:, :, None], (HEADS, CHUNK, HEAD_DIM)), k_decay


def _tail(x, mixed, p, w, *, tm, tm_ffn, tf):
    casts = {}
    res = _outproj(mixed, w["out"], x, w["ln1_g"], w["ln1_b"], tm=tm)
    x1, x1b = res[0], res[1]
    if len(res) > 2:
        casts["out"] = res[2]
    res = _ffn(x1b, w["ff1"], w["ff2"], tm=tm_ffn, tf=tf)
    if isinstance(res, (list, tuple)):
        ff, casts["ff1"], casts["ff2"] = res
    else:
        ff = res
    res = _pe_ln2(x1, x1b, p, ff, w["pe"], w["pe_gate"], w["ln2_g"], w["ln2_b"], tm=tm)
    if isinstance(res, (list, tuple)):
        y, casts["pe"], casts["pe_gate"] = res
    else:
        y = res
    return y, casts


def kernel(x_prompt, x_sample, state_ret, state_mlstm_C, state_mlstm_n, state_mlstm_m, p_prompt, p_sample, w_in, b_gate, ret_gn_w, mlstm_gn_w, w_out, ln1_g, ln1_b, w_ff1, w_ff2, w_pe, w_pe_gate, ln2_g, ln2_b):
    batch, seq, _ = x_prompt.shape
    nb = x_sample.shape[0]

    w_in_t = w_in[0].T
    b8 = jnp.pad(b_gate[0].astype(F32), (0, LANES - 2 * HEADS)).reshape(1, LANES)
    w = {
        "out": w_out[0], "ff1": w_ff1[0], "ff2": w_ff2[0], "pe": w_pe[0], "pe_gate": w_pe_gate[0],
        "ln1_g": ln1_g[0].reshape(1, D_MODEL), "ln1_b": ln1_b[0].reshape(1, D_MODEL),
        "ln2_g": ln2_g[0].reshape(1, D_MODEL), "ln2_b": ln2_b[0].reshape(1, D_MODEL),
    }
    rgn = ret_gn_w[0].reshape(1, WIDTH)
    mgn = mlstm_gn_w[0].reshape(1, WIDTH)
    intra, qdec, kdec = _decay_tables()

    xs = x_sample.reshape(nb, D_MODEL)
    xsb, gs_tok, _ = _mgate(xs, w_in_t, b8, tm=nb)
    ctab, stab = _rope_tables(jnp.full((nb,), PAST_LEN, dtype=F32))
    proj_s, w_in_b = _inproj(xsb, w_in_t, ctab, stab, tm=nb)
    proj_s = proj_s.astype(F32)
    m0 = jnp.pad(state_mlstm_m[0], ((0, 0), (0, LANES - HEADS)))
    mt, dw, iw, emt = _sample_gates(gs_tok, m0)
    scal = jnp.concatenate([mt[:, :HEADS], dw[:, :HEADS], iw[:, :HEADS], emt[:, :HEADS]], axis=1)
    m_s = mt[:, :HEADS].reshape(1, nb, HEADS)

    xp = x_prompt.reshape(batch * seq, D_MODEL)
    xpb, g_tok, g_t = _mgate(xp, w_in_t, b8, tm=GATE_TM)
    ctab, stab = _rope_tables(jnp.arange(seq, dtype=F32))
    rows_h = proj_s[nb - HOSTED_TOKENS:]
    qk = jnp.concatenate([rows_h[:, 0:2 * WIDTH], rows_h[:, 4 * WIDTH:6 * WIDTH]], axis=1)
    cols_h = jnp.transpose(qk.reshape(HOSTED_TOKENS, 4 * HEADS, HEAD_DIM), (0, 2, 1))
    proj_p, mixed_h, s_h, c_h, n_h = _inproj(
        xpb, w_in_b, ctab, stab, tm=INPROJ_TM,
        sample=(scal, cols_h, rows_h, state_ret, state_mlstm_C, state_mlstm_n, rgn, mgn))
    mixed_m, s_s, c_s, n_s, mixed_p, s_p, c_p, n_p, m_rows = _mixers(
        scal, proj_s, state_ret, state_mlstm_C, state_mlstm_n,
        proj_p, g_tok, g_t, intra, qdec, kdec, rgn, mgn, (s_h, c_h, n_h), batch=batch, seq=seq)
    assert mixed_m.shape[0] + HOSTED_TOKENS == nb
    mixed_s = jnp.concatenate([mixed_m, mixed_h], axis=0)
    m_p = m_rows[:, :HEADS, 0].reshape(1, batch, HEADS)

    y_s, w_bf16 = _tail(xs, mixed_s, p_sample[0].reshape(nb, PLE_DIM), w,
                        tm=nb, tm_ffn=nb, tf=CAST_FFN_TF)
    w = {**w, **w_bf16}
    y_p, _ = _tail(xp, mixed_p, p_prompt[0].reshape(batch * seq, PLE_DIM), w,
                   tm=TAIL_TM, tm_ffn=FFN_TM, tf=FFN_TF)

    return (y_p.reshape(batch, seq, D_MODEL), y_s.reshape(nb, 1, D_MODEL),
            s_p, c_p, n_p, m_p, s_s, c_s, n_s, m_s)
```

```python
import jax
import jax.numpy as jnp
import numpy as np
from jax import lax
from jax.experimental import pallas as pl
from jax.experimental.pallas import tpu as pltpu

F32 = jnp.float32
BF16 = jnp.bfloat16

D_MODEL = 2048
HEADS = 4
HEAD_DIM = 256
HALF = HEAD_DIM // 2
WIDTH = HEADS * HEAD_DIM
D_FF = 4 * D_MODEL
PLE_DIM = 256
CHUNK = 128
PAST_LEN = 16384
ROPE_BASE = 10000.0
LN_EPS = 1e-5
GN_EPS = 1e-6
DEPTH = 1
ALPHA = (2 * DEPTH) ** 0.25
QK_SCALE = HEAD_DIM ** -0.5
LANES = 128
SUBLANES = 8
MXU_ROWS = 256
VMEM_LIMIT = 56 * 1024 * 1024
MIXERS_VMEM_LIMIT = 60 * 1024 * 1024
SAMPLE_TOKENS_PER_STEP = 2
HOSTED_TOKENS = 64
MIXER_CHUNKS_PER_STEP = 2
STATE_RING_SLOTS = 3
GATE_TM = 1024
INPROJ_TM = 2048
INPROJ_PIECE_ROWS = 1024
TAIL_TM = 512
FFN_TM = 1024
FFN_TF = 1024
CAST_FFN_TF = 512

RET_G = tuple(1.0 - 2.0 ** (-5.0 - h) for h in range(HEADS))
RET_LOG_G = tuple(float(np.log(np.float32(g))) for g in RET_G)
RET_STATE_DECAY = tuple(float(np.exp(np.float32(lg) * np.float32(CHUNK))) for lg in RET_LOG_G)

_NT = (((1,), (1,)), ((), ()))
_TN = (((0,), (0,)), ((), ()))


def _params(*sem, vmem=None):
    return pltpu.CompilerParams(dimension_semantics=sem, vmem_limit_bytes=vmem or VMEM_LIMIT)


def _sigmoid(x):
    return 0.5 * jnp.tanh(0.5 * x) + 0.5


def _log_sigmoid(x):
    return jnp.minimum(x, 0.0) - jnp.log(1.0 + jnp.exp(-jnp.abs(x)))


def _layer_norm(y, g, b):
    mu = jnp.mean(y, axis=-1, keepdims=True)
    yc = y - mu
    var = jnp.mean(jnp.square(yc), axis=-1, keepdims=True)
    return yc * lax.rsqrt(var + LN_EPS) * g + b


def _row_splits(tm, rows=None):
    if rows is not None and tm >= rows:
        return [slice(r, r + rows) for r in range(0, tm, rows)]
    if tm < 2 * MXU_ROWS:
        return [slice(0, tm)]
    return [slice(0, tm // 2), slice(tm // 2, tm)]


def _head_norm(o):
    mu = jnp.mean(o, axis=-1, keepdims=True)
    oc = o - mu
    var = jnp.mean(jnp.square(oc), axis=-1, keepdims=True)
    return oc * lax.rsqrt(var + GN_EPS)


RG_GROUP, MO_GROUP = 3, 7


def _inproj_body(x_ref, wb_ref, c_ref, s_ref, o_ref, hosted=()):
    j = pl.program_id(1)
    is_gate = (j == RG_GROUP) | (j == MO_GROUP)
    pieces = [(h, rows) for h in range(HEADS) for rows in _row_splits(x_ref.shape[0], INPROJ_PIECE_ROWS)]

    def piece_acc(h, rows):
        return lax.dot_general(x_ref[rows, :], wb_ref[h * HEAD_DIM:(h + 1) * HEAD_DIM, :], _NT,
                               preferred_element_type=F32)

    def gate_piece(h, rows):
        acc = piece_acc(h, rows)
        sig = _sigmoid(acc)
        o_ref[rows, h * HEAD_DIM:(h + 1) * HEAD_DIM] = jnp.where(
            j == RG_GROUP, acc * sig, sig).astype(o_ref.dtype)

    def rotary_piece(h, rows):
        scale = jnp.where(j == 1, QK_SCALE, 1.0).astype(F32)
        lo = h * HEAD_DIM
        c = c_ref[rows, :] * scale
        s = s_ref[rows, :] * scale
        acc = piece_acc(h, rows)
        x1 = acc[:, :HALF]
        x2 = acc[:, HALF:]
        o_ref[rows, lo:lo + HALF] = (x1 * c - x2 * s).astype(o_ref.dtype)
        o_ref[rows, lo + HALF:lo + HEAD_DIM] = (x1 * s + x2 * c).astype(o_ref.dtype)

    def plain_piece(h, rows):
        scale = jnp.where(j == MO_GROUP - 2, QK_SCALE, 1.0).astype(F32)
        o_ref[rows, h * HEAD_DIM:(h + 1) * HEAD_DIM] = (piece_acc(h, rows) * scale).astype(o_ref.dtype)

    def run(piece):
        for n, (h, rows) in enumerate(pieces):
            if n < len(hosted):
                hosted[n]()
            piece(h, rows)

    is_rotary = j <= 1
    pl.when(is_gate)(lambda: run(gate_piece))
    pl.when(is_rotary)(lambda: run(rotary_piece))
    pl.when(jnp.logical_not(is_gate | is_rotary))(lambda: run(plain_piece))


def _inproj_cast_kernel(x_ref, wt_ref, c_ref, s_ref, o_ref, wb_ref):
    wb_ref[...] = wt_ref[...].astype(BF16)
    _inproj_body(x_ref, wb_ref, c_ref, s_ref, o_ref)


def _inproj_host_kernel(x_ref, wt_ref, c_ref, s_ref, scal_ref, cols_ref, rows_ref, s0_ref, c0_ref,
                        n0_ref, rgn_ref, mgn_ref, o_ref, smix_ref, ss_ref, sc_ref, sn_ref):
    th = cols_ref.shape[0]
    step = pl.program_id(0) * pl.num_programs(1) + pl.program_id(1)
    first = scal_ref.shape[0] - rows_ref.shape[0]
    hosted = []
    for t in range(th):
        local = step * th + t
        hosted += _sample_token_chunks(first + local, local, t, scal_ref, cols_ref, rows_ref, s0_ref,
                                       c0_ref, n0_ref, rgn_ref, mgn_ref, smix_ref, ss_ref, sc_ref,
                                       sn_ref)
    _inproj_body(x_ref, wt_ref, c_ref, s_ref, o_ref, hosted)


def _inproj(xb, wt, cos, sin, *, tm, sample=None):
    m = xb.shape[0]
    n_i = m // tm
    n_pos = cos.shape[0] // tm
    wspec = pl.BlockSpec((WIDTH, D_MODEL), lambda i, j: (j, 0))
    ospec = pl.BlockSpec((tm, WIDTH), lambda i, j: (i, j))
    oshape = jax.ShapeDtypeStruct((m, 8 * WIDTH), BF16)
    tspec = pl.BlockSpec((tm, HALF), lambda i, j: (i % n_pos, 0))
    in_specs = [pl.BlockSpec((tm, D_MODEL), lambda i, j: (i, 0)), wspec, tspec, tspec]
    if wt.dtype != BF16:
        assert n_i == 1 and sample is None
        return pl.pallas_call(
            _inproj_cast_kernel,
            grid=(1, 8),
            in_specs=in_specs,
            out_specs=[ospec, wspec],
            out_shape=[oshape, jax.ShapeDtypeStruct((8 * WIDTH, D_MODEL), BF16)],
            compiler_params=_params("arbitrary", "arbitrary"),
            name="inproj_cast",
        )(xb, wt, cos, sin)
    scal, cols_s, rows_s, s0, c0, n0, rgn, mgn = sample
    nh = rows_s.shape[0]
    th = nh // (8 * n_i)
    first = (scal.shape[0] - nh) // th
    assert nh == 8 * n_i * th and first * th + nh == scal.shape[0]
    const2 = lambda i, j: (0, 0)
    sstate = pl.BlockSpec((1, th, HEADS, HEAD_DIM, HEAD_DIM),
                          lambda i, j: (0, first + i * 8 + j, 0, 0, 0))
    snorm = pl.BlockSpec((1, th, HEADS, HEAD_DIM), lambda i, j: (0, first + i * 8 + j, 0, 0))
    return pl.pallas_call(
        _inproj_host_kernel,
        grid=(n_i, 8),
        in_specs=in_specs + [
            pl.BlockSpec(memory_space=pltpu.SMEM),
            pl.BlockSpec((th, HEAD_DIM, 4 * HEADS), lambda i, j: (i * 8 + j, 0, 0)),
            pl.BlockSpec((nh, 8 * WIDTH), const2, pipeline_mode=pl.Buffered(1)),
            sstate, sstate, snorm,
            pl.BlockSpec((1, WIDTH), const2),
            pl.BlockSpec((1, WIDTH), const2),
        ],
        out_specs=[ospec, pl.BlockSpec((nh, 2 * WIDTH), const2), sstate, sstate, snorm],
        out_shape=[oshape, jax.ShapeDtypeStruct((nh, 2 * WIDTH), F32),
                   jax.ShapeDtypeStruct(s0.shape, F32), jax.ShapeDtypeStruct(c0.shape, F32),
                   jax.ShapeDtypeStruct(n0.shape, F32)],
        compiler_params=_params("arbitrary", "arbitrary"),
        name="inproj",
    )(xb, wt, cos, sin, scal, cols_s, rows_s, s0, c0, n0, rgn, mgn)


def _mgate_kernel(x_ref, wt_ref, b_ref, xb_ref, g_ref, gt_ref):
    xb = x_ref[...].astype(BF16)
    xb_ref[...] = xb
    wt = jnp.concatenate([wt_ref[...], jnp.zeros((LANES - 2 * HEADS, D_MODEL), F32)], axis=0)
    g = lax.dot_general(xb, wt.astype(BF16), _NT, preferred_element_type=F32) + b_ref[...]
    lane = lax.broadcasted_iota(jnp.int32, g.shape, 1)
    g = jnp.where((lane >= HEADS) & (lane < 2 * HEADS), _log_sigmoid(g), g)
    r = lax.broadcasted_iota(jnp.int32, (CHUNK, CHUNK), 0)
    s = lax.broadcasted_iota(jnp.int32, (CHUNK, CHUNK), 1)
    tril = (r >= s).astype(BF16)
    g1 = g.astype(BF16)
    res = g - g1.astype(F32)
    g2 = res.astype(BF16)
    g3 = (res - g2.astype(F32)).astype(BF16)
    parts = []
    for c in range(g.shape[0] // CHUNK):
        sl = slice(c * CHUNK, (c + 1) * CHUNK)
        parts.append(jnp.dot(tril, g1[sl], preferred_element_type=F32)
                     + jnp.dot(tril, g2[sl], preferred_element_type=F32)
                     + jnp.dot(tril, g3[sl], preferred_element_type=F32))
    cs = parts[0] if len(parts) == 1 else jnp.concatenate(parts, axis=0)
    b = pltpu.roll(cs, HEADS, 1)
    u = pltpu.roll(g, 3 * HEADS, 1) - pltpu.roll(cs, 2 * HEADS, 1)
    g = jnp.where((lane >= 2 * HEADS) & (lane < 3 * HEADS), b, g)
    g = jnp.where((lane >= 3 * HEADS) & (lane < 4 * HEADS), u, g)
    g_ref[...] = g
    gt_ref[...] = g.T[0:4 * HEADS, :]


def _mgate(x, w_in_t, b8, *, tm):
    m = x.shape[0]
    gate_rows = 8 * WIDTH // (2 * HEADS)
    return pl.pallas_call(
        _mgate_kernel,
        grid=(m // tm,),
        in_specs=[
            pl.BlockSpec((tm, D_MODEL), lambda i: (i, 0)),
            pl.BlockSpec((2 * HEADS, D_MODEL), lambda i: (gate_rows, 0)),
            pl.BlockSpec((1, LANES), lambda i: (0, 0)),
        ],
        out_specs=[
            pl.BlockSpec((tm, D_MODEL), lambda i: (i, 0)),
            pl.BlockSpec((tm, LANES), lambda i: (i, 0)),
            pl.BlockSpec((4 * HEADS, tm), lambda i: (0, i)),
        ],
        out_shape=[
            jax.ShapeDtypeStruct((m, D_MODEL), BF16),
            jax.ShapeDtypeStruct((m, LANES), F32),
            jax.ShapeDtypeStruct((4 * HEADS, m), F32),
        ],
        compiler_params=_params("arbitrary"),
        name="mlstm_gates",
    )(x, w_in_t, b8)


def _mixer_chunk(rows, causal, qkv_ref, g_ref, gt_ref, intra_ref, dec_ref, kdec_ref, rgn_ref, mgn_ref,
                 mix_ref, s_ref, c_ref, n_ref, m_scr):
    def head(group, h):
        lo = group * WIDTH + h * HEAD_DIM
        return qkv_ref[rows, lo:lo + HEAD_DIM]

    gw = []
    for h in range(HEADS):
        b_col = g_ref[rows, 2 * HEADS + h:2 * HEADS + h + 1]
        u_row = gt_ref[3 * HEADS + h:3 * HEADS + h + 1, rows]
        m_prev = m_scr[h:h + 1, 0:1]
        dlog = jnp.where(causal, b_col + u_row, -jnp.inf)
        inter = b_col + m_prev
        mt = jnp.maximum(inter, jnp.max(dlog, axis=1, keepdims=True))
        m_new = mt[CHUNK - 1:CHUNK, :]
        b_last = b_col[CHUNK - 1:CHUNK, :]
        gw.append(dict(
            dw=jnp.exp(dlog - mt), iw=jnp.exp(inter - mt), emt=jnp.exp(-mt),
            sw=jnp.exp(b_last - m_new + u_row), sd=jnp.exp(b_last + m_prev - m_new)))
        m_scr[h:h + 1, :] = jnp.broadcast_to(m_new, (1, LANES))

    st = []
    for h in range(HEADS):
        for kind, (qg, state_ref) in enumerate(((0, s_ref), (MO_GROUP - 3, c_ref))):
            q = head(qg, h)
            k = head(qg + 1, h)
            old = state_ref[0, 0, h]
            kt = k.T
            st.append(dict(
                h=h, kind=kind, q=q, k=k, kt=kt, v=head(qg + 2, h), old=old,
                sc=jnp.dot(q, kt, preferred_element_type=F32),
                inter=jnp.dot(q, old.astype(BF16), preferred_element_type=F32)))

    for e in st:
        h = e["h"]
        if e["kind"] == 0:
            kd = (e["kt"].astype(F32) * kdec_ref[h:h + 1, :]).astype(BF16)
            s_ref[0, 0, h] = (e["old"] * RET_STATE_DECAY[h]
                              + jnp.dot(kd, e["v"], preferred_element_type=F32))
        else:
            w = gw[h]
            ksw = (e["kt"].astype(F32) * w["sw"]).astype(BF16)
            c_ref[0, 0, h] = w["sd"] * e["old"] + jnp.dot(ksw, e["v"], preferred_element_type=F32)
            n_old = n_ref[0, 0, h:h + 1, :]
            e["qn"] = jnp.sum(e["q"].astype(F32) * n_old, axis=1, keepdims=True)
            sw8 = jnp.broadcast_to(w["sw"], (SUBLANES, CHUNK)).astype(BF16)
            n_ref[0, 0, h:h + 1, :] = (w["sd"] * n_old
                                       + jnp.dot(sw8, e["k"], preferred_element_type=F32)[0:1, :])

    for e in st:
        h = e["h"]
        e["sc"] = e["sc"] * (intra_ref[h] if e["kind"] == 0 else gw[h]["dw"])
        e["pv"] = jnp.dot(e["sc"].astype(BF16), e["v"], preferred_element_type=F32)

    for e in st:
        e["rows"] = rows
        if e["kind"]:
            e["iw"] = gw[e["h"]]["iw"]
            e["emt"] = gw[e["h"]]["emt"]
    return st


def _mixer_tail(streams, qkv_ref, dec_ref, rgn_ref, mgn_ref, mix_ref):
    dens = [jnp.sum(e["sc"], axis=1, keepdims=True) if e["kind"] else None for e in streams]
    outs = []
    for e, den in zip(streams, dens):
        if e["kind"] == 0:
            outs.append(e["pv"] + e["inter"] * dec_ref[e["h"]])
        else:
            num = e["pv"] + e["iw"] * e["inter"]
            den = den + e["iw"] * e["qn"]
            outs.append(num * (1.0 / jnp.maximum(jnp.abs(den), e["emt"])))
    mus = [jnp.mean(o, axis=-1, keepdims=True) for o in outs]
    cen = [o - mu for o, mu in zip(outs, mus)]
    var = [jnp.mean(jnp.square(oc), axis=-1, keepdims=True) for oc in cen]
    for e, oc, v in zip(streams, cen, var):
        rows = e["rows"]
        lo = e["h"] * HEAD_DIM
        gate_lo = (MO_GROUP if e["kind"] else RG_GROUP) * WIDTH + lo
        gain_ref = mgn_ref if e["kind"] else rgn_ref
        out_lo = e["kind"] * WIDTH + lo
        y = (oc * lax.rsqrt(v + GN_EPS) * gain_ref[:, lo:lo + HEAD_DIM]
             * qkv_ref[rows, gate_lo:gate_lo + HEAD_DIM].astype(F32))
        mix_ref[rows, out_lo:out_lo + HEAD_DIM] = y.astype(mix_ref.dtype)


def _mixers_kernel(scal_ref, rows_ref, s0_ref, c0_ref, n0_ref,
                   qkv_ref, g_ref, gt_ref, intra_ref, dec_ref, kdec_ref, rgn_ref, mgn_ref,
                   sh_ref, ch_ref, nh_ref,
                   smix_ref, ss_ref, sc_ref, sn_ref,
                   mix_ref, s_ref, c_ref, n_ref, m_ref, m_scr, sbuf, cbuf, sems):
    del sh_ref, ch_ref, nh_ref
    c = pl.program_id(1)
    step = pl.program_id(0) * pl.num_programs(1) + c
    n_steps = pl.num_programs(0) * pl.num_programs(1)
    tb = sbuf.shape[2]

    def fetch(s):
        slot = s % STATE_RING_SLOTS
        tok = pl.ds(s * tb, tb)
        return (pltpu.make_async_copy(s0_ref.at[:, tok], sbuf.at[slot], sems.at[0, slot]),
                pltpu.make_async_copy(c0_ref.at[:, tok], cbuf.at[slot], sems.at[1, slot]))

    @pl.when(step == 0)
    def _():
        for s in range(STATE_RING_SLOTS - 1):
            for copy in fetch(s):
                copy.start()

    @pl.when(step + STATE_RING_SLOTS - 1 < n_steps)
    def _():
        for copy in fetch(step + STATE_RING_SLOTS - 1):
            copy.start()

    for copy in fetch(step):
        copy.wait()
    slot = step % STATE_RING_SLOTS
    _sample_tokens(step, scal_ref, rows_ref, sbuf.at[slot], cbuf.at[slot],
                   n0_ref, rgn_ref, mgn_ref, smix_ref, ss_ref, sc_ref, sn_ref)

    @pl.when(c == 0)
    def _():
        s_ref[...] = jnp.zeros_like(s_ref)
        c_ref[...] = jnp.zeros_like(c_ref)
        n_ref[...] = jnp.zeros_like(n_ref)
        m_scr[...] = jnp.zeros_like(m_scr)

    ti = lax.broadcasted_iota(jnp.int32, (CHUNK, CHUNK), 0)
    si = lax.broadcasted_iota(jnp.int32, (CHUNK, CHUNK), 1)
    causal = ti >= si

    streams = []
    for ci in range(MIXER_CHUNKS_PER_STEP):
        streams += _mixer_chunk(slice(ci * CHUNK, (ci + 1) * CHUNK), causal, qkv_ref, g_ref, gt_ref,
                                intra_ref, dec_ref, kdec_ref, rgn_ref, mgn_ref, mix_ref, s_ref, c_ref,
                                n_ref, m_scr)
    _mixer_tail(streams, qkv_ref, dec_ref, rgn_ref, mgn_ref, mix_ref)

    @pl.when(c == pl.num_programs(1) - 1)
    def _():
        m_ref[0] = m_scr[...]


def _mixers(scal, rows_s, s0, c0, n0, qkv, g_tok, g_t, intra, qdec, kdec, rgn, mgn, hosted, *,
            batch, seq):
    rows = MIXER_CHUNKS_PER_STEP * CHUNK
    nc = seq // rows
    nb = rows_s.shape[0]
    tb = SAMPLE_TOKENS_PER_STEP
    nm = batch * nc * tb
    assert nm <= nb
    any_spec = pl.BlockSpec(memory_space=pl.ANY)
    row = lambda b, c: (b * nc + c, 0)
    const2 = lambda b, c: (0, 0)
    state = pl.BlockSpec((1, 1, HEADS, HEAD_DIM, HEAD_DIM), lambda b, c: (0, b, 0, 0, 0))
    sstate = pl.BlockSpec((1, tb, HEADS, HEAD_DIM, HEAD_DIM), lambda b, c: (0, b * nc + c, 0, 0, 0))
    snorm = pl.BlockSpec((1, tb, HEADS, HEAD_DIM), lambda b, c: (0, b * nc + c, 0, 0))
    return pl.pallas_call(
        _mixers_kernel,
        grid=(batch, nc),
        in_specs=[
            pl.BlockSpec(memory_space=pltpu.SMEM),
            pl.BlockSpec((nb, 8 * WIDTH), const2, pipeline_mode=pl.Buffered(1)),
            any_spec, any_spec, snorm,
            pl.BlockSpec((rows, 8 * WIDTH), row),
            pl.BlockSpec((rows, LANES), row),
            pl.BlockSpec((4 * HEADS, rows), lambda b, c: (0, b * nc + c)),
            pl.BlockSpec((HEADS, CHUNK, CHUNK), lambda b, c: (0, 0, 0)),
            pl.BlockSpec((HEADS, CHUNK, HEAD_DIM), lambda b, c: (0, 0, 0)),
            pl.BlockSpec((HEADS, CHUNK), const2),
            pl.BlockSpec((1, WIDTH), const2),
            pl.BlockSpec((1, WIDTH), const2),
            any_spec, any_spec, any_spec,
        ],
        out_specs=[
            pl.BlockSpec((nm, 2 * WIDTH), const2), sstate, sstate, snorm,
            pl.BlockSpec((rows, 2 * WIDTH), row),
            state,
            state,
            pl.BlockSpec((1, 1, HEADS, HEAD_DIM), lambda b, c: (0, b, 0, 0)),
            pl.BlockSpec((1, SUBLANES, LANES), lambda b, c: (b, 0, 0)),
        ],
        out_shape=[
            jax.ShapeDtypeStruct((nm, 2 * WIDTH), F32),
            jax.ShapeDtypeStruct(s0.shape, F32),
            jax.ShapeDtypeStruct(c0.shape, F32),
            jax.ShapeDtypeStruct(n0.shape, F32),
            jax.ShapeDtypeStruct((batch * seq, 2 * WIDTH), BF16),
            jax.ShapeDtypeStruct((1, batch, HEADS, HEAD_DIM, HEAD_DIM), F32),
            jax.ShapeDtypeStruct((1, batch, HEADS, HEAD_DIM, HEAD_DIM), F32),
            jax.ShapeDtypeStruct((1, batch, HEADS, HEAD_DIM), F32),
            jax.ShapeDtypeStruct((batch, SUBLANES, LANES), F32),
        ],
        scratch_shapes=[pltpu.VMEM((SUBLANES, LANES), F32),
                        pltpu.VMEM((STATE_RING_SLOTS, 1, tb, HEADS, HEAD_DIM, HEAD_DIM), F32),
                        pltpu.VMEM((STATE_RING_SLOTS, 1, tb, HEADS, HEAD_DIM, HEAD_DIM), F32),
                        pltpu.SemaphoreType.DMA((2, STATE_RING_SLOTS))],
        input_output_aliases={13: 1, 14: 2, 15: 3},
        compiler_params=_params("arbitrary", "arbitrary", vmem=MIXERS_VMEM_LIMIT),
        name="mixers",
    )(scal, rows_s, s0, c0, n0, qkv, g_tok, g_t, intra, qdec, kdec, rgn, mgn, *hosted)


def _sample_gate_kernel(g_ref, m0_ref, mt_ref, dw_ref, iw_ref, emt_ref):
    ig = g_ref[...]
    lf = pltpu.roll(ig, LANES - HEADS, 1)
    inter = lf + m0_ref[...]
    mt = jnp.maximum(inter, ig)
    mt_ref[...] = mt
    dw_ref[...] = jnp.exp(ig - mt)
    iw_ref[...] = jnp.exp(inter - mt)
    emt_ref[...] = jnp.exp(-mt)


def _sample_gates(g_tok, m0_pad):
    m = g_tok.shape[0]
    spec = pl.BlockSpec((m, LANES), lambda: (0, 0))
    return pl.pallas_call(
        _sample_gate_kernel,
        in_specs=[spec, spec],
        out_specs=[spec] * 4,
        out_shape=[jax.ShapeDtypeStruct((m, LANES), F32)] * 4,
        name="sample_gates",
    )(g_tok, m0_pad)


def _sample_tokens(step, scal_ref, rows_ref, s0_ref, c0_ref, n0_ref,
                   rgn_ref, mgn_ref, mix_ref, s_ref, c_ref, n_ref):
    first = lax.broadcasted_iota(jnp.int32, (4 * HEADS, HEAD_DIM), 0) == 0
    for t in range(SAMPLE_TOKENS_PER_STEP):
        b = step * SAMPLE_TOKENS_PER_STEP + t

        def row(group, h):
            lo = group * WIDTH + h * HEAD_DIM
            return rows_ref[pl.ds(b, 1), lo:lo + HEAD_DIM]

        def outer(k, v):
            kp = jnp.where(first, jnp.broadcast_to(k, first.shape), 0.0).astype(BF16)
            vp = jnp.where(first, jnp.broadcast_to(v, first.shape), 0.0).astype(BF16)
            return lax.dot_general(kp, vp, _TN, preferred_element_type=F32)

        def apply(q_row, state):
            q8 = jnp.broadcast_to(q_row, (SUBLANES, HEAD_DIM)).astype(BF16)
            return jnp.dot(q8, state.astype(BF16), preferred_element_type=F32)[0:1, :]

        dw = [scal_ref[b, HEADS + h] for h in range(HEADS)]
        iw = [scal_ref[b, 2 * HEADS + h] for h in range(HEADS)]
        emt = [scal_ref[b, 3 * HEADS + h] for h in range(HEADS)]
        kv_s = [outer(row(1, h), row(2, h)) for h in range(HEADS)]
        kv_c = [outer(row(5, h), dw[h] * row(6, h)) for h in range(HEADS)]
        s_new, c_new = [], []
        for h in range(HEADS):
            s_new.append(RET_G[h] * s0_ref[0, t, h] + kv_s[h])
            s_ref[0, t, h] = s_new[h]
            c_new.append(iw[h] * c0_ref[0, t, h] + kv_c[h])
            c_ref[0, t, h] = c_new[h]
        o = [apply(row(0, h), s_new[h]) for h in range(HEADS)]
        num = [apply(row(4, h), c_new[h]) for h in range(HEADS)]
        for h in range(HEADS):
            lo = h * HEAD_DIM
            mix_ref[pl.ds(b, 1), lo:lo + HEAD_DIM] = (_head_norm(o[h]) * rgn_ref[:, lo:lo + HEAD_DIM]
                                               * row(RG_GROUP, h))
            n_new = iw[h] * n0_ref[0, t, h:h + 1, :] + dw[h] * row(5, h)
            n_ref[0, t, h:h + 1, :] = n_new
            den = jnp.sum(row(4, h) * n_new, axis=1, keepdims=True)
            hid = num[h] * (1.0 / jnp.maximum(jnp.abs(den), emt[h]))
            mix_ref[pl.ds(b, 1), WIDTH + lo:WIDTH + lo + HEAD_DIM] = (
                _head_norm(hid) * mgn_ref[:, lo:lo + HEAD_DIM] * row(MO_GROUP, h))


def _sample_token_chunks(b, local, t, scal_ref, cols_ref, rows_ref, s0_ref, c0_ref, n0_ref, rgn_ref,
                         mgn_ref, mix_ref, s_ref, c_ref, n_ref):
    def row(group, h):
        lo = group * WIDTH + h * HEAD_DIM
        return rows_ref[pl.ds(local, 1), lo:lo + HEAD_DIM]

    def col(group, h):
        return cols_ref[t, :, group * HEADS + h:group * HEADS + h + 1]

    def rounded(a):
        return a.astype(BF16).astype(F32)

    def chunk(h):
        lo = h * HEAD_DIM
        s_new = RET_G[h] * s0_ref[0, t, h] + col(1, h) * row(2, h)
        s_ref[0, t, h] = s_new
        o = jnp.sum(col(0, h) * rounded(s_new), axis=0, keepdims=True)
        mix_ref[pl.ds(local, 1), lo:lo + HEAD_DIM] = (
            _head_norm(o) * rgn_ref[:, lo:lo + HEAD_DIM] * row(RG_GROUP, h))

        dw = scal_ref[b, HEADS + h]
        iw = scal_ref[b, 2 * HEADS + h]
        emt = scal_ref[b, 3 * HEADS + h]
        c_new = iw * c0_ref[0, t, h] + col(3, h) * rounded(dw * row(6, h))
        c_ref[0, t, h] = c_new
        n_new = iw * n0_ref[0, t, h:h + 1, :] + dw * row(5, h)
        n_ref[0, t, h:h + 1, :] = n_new
        num = jnp.sum(col(2, h) * rounded(c_new), axis=0, keepdims=True)
        den = jnp.sum(row(4, h) * n_new, axis=1, keepdims=True)
        hid = num * (1.0 / jnp.maximum(jnp.abs(den), emt))
        mix_ref[pl.ds(local, 1), WIDTH + lo:WIDTH + lo + HEAD_DIM] = (
            _head_norm(hid) * mgn_ref[:, lo:lo + HEAD_DIM] * row(MO_GROUP, h))

    return [lambda h=h: chunk(h) for h in range(HEADS)]


def _outproj_kernel(a_ref, w_ref, x_ref, g_ref, b_ref, x1_ref, x1b_ref, *wb_out):
    if wb_out:
        wb_out[0][...] = w_ref[...].astype(BF16)
        w_ref = wb_out[0]
    tm = a_ref.shape[0]
    for rows in _row_splits(tm):
        mix = jnp.dot(a_ref[rows, :].astype(BF16), w_ref[...], preferred_element_type=F32)
        x1 = _layer_norm(ALPHA * x_ref[rows, :] + mix, g_ref[...], b_ref[...])
        x1_ref[rows, :] = x1
        x1b_ref[rows, :] = x1.astype(BF16)


def _outproj(a, w, x, g, b, *, tm):
    m = a.shape[0]
    cast_w = w.dtype != BF16
    assert not cast_w or m == tm
    row = pl.BlockSpec((tm, D_MODEL), lambda i: (i, 0))
    vec = pl.BlockSpec((1, D_MODEL), lambda i: (0, 0))
    out_specs = [row, row]
    out_shape = [jax.ShapeDtypeStruct((m, D_MODEL), F32), jax.ShapeDtypeStruct((m, D_MODEL), BF16)]
    if cast_w:
        out_specs.append(pl.BlockSpec((D_MODEL, D_MODEL), lambda i: (0, 0)))
        out_shape.append(jax.ShapeDtypeStruct((D_MODEL, D_MODEL), BF16))
    return pl.pallas_call(
        _outproj_kernel,
        grid=(m // tm,),
        in_specs=[row, pl.BlockSpec((D_MODEL, D_MODEL), lambda i: (0, 0), pipeline_mode=pl.Buffered(1)),
                  row, vec, vec],
        out_specs=out_specs,
        out_shape=out_shape,
        compiler_params=_params("arbitrary"),
        name="outproj_ln1_cast" if cast_w else "outproj_ln1",
    )(a, w, x, g, b)


def _ffn_kernel(x1b_ref, w1_ref, w2_ref, o_ref, *wb_out):
    @pl.when(pl.program_id(1) == 0)
    def _():
        o_ref[...] = jnp.zeros_like(o_ref)

    w1 = w1_ref[...]
    w2 = w2_ref[...]
    if wb_out:
        w1 = w1.astype(BF16)
        w2 = w2.astype(BF16)
        wb_out[0][...] = w1
        wb_out[1][...] = w2
    hid = jnp.dot(x1b_ref[...], w1, preferred_element_type=F32)
    hid = jnp.square(jnp.maximum(hid, 0.0)).astype(BF16)
    o_ref[...] += jnp.dot(hid, w2, preferred_element_type=F32)


def _ffn(x1b, w1, w2, *, tm, tf):
    m = x1b.shape[0]
    cast_w = w1.dtype != BF16
    assert not cast_w or m == tm
    row = pl.BlockSpec((tm, D_MODEL), lambda i, f: (i, 0))
    w1spec = pl.BlockSpec((D_MODEL, tf), lambda i, f: (0, f))
    w2spec = pl.BlockSpec((tf, D_MODEL), lambda i, f: (f, 0))
    oshape = jax.ShapeDtypeStruct((m, D_MODEL), F32)
    return pl.pallas_call(
        _ffn_kernel,
        grid=(m // tm, D_FF // tf),
        in_specs=[row, w1spec, w2spec],
        out_specs=[row, w1spec, w2spec] if cast_w else row,
        out_shape=[oshape, jax.ShapeDtypeStruct(w1.shape, BF16),
                   jax.ShapeDtypeStruct(w2.shape, BF16)] if cast_w else oshape,
        compiler_params=_params("arbitrary", "arbitrary"),
        name="ffn_cast" if cast_w else "ffn",
    )(x1b, w1, w2)


def _pe_ln2_kernel(x1_ref, x1b_ref, p_ref, ff_ref, wpe_ref, wg_ref, g_ref, b_ref, o_ref, *wb_out):
    if wb_out:
        wb_out[0][...] = wpe_ref[...].astype(BF16)
        wb_out[1][...] = wg_ref[...].astype(BF16)
        wpe_ref, wg_ref = wb_out
    for rows in _row_splits(x1_ref.shape[0]):
        gate = _sigmoid(jnp.dot(x1b_ref[rows, :], wg_ref[...], preferred_element_type=F32))
        pe = jnp.dot(p_ref[rows, :].astype(BF16), wpe_ref[...], preferred_element_type=F32)
        y = ALPHA * x1_ref[rows, :] + ff_ref[rows, :] + pe * gate
        o_ref[rows, :] = _layer_norm(y, g_ref[...], b_ref[...])


def _pe_ln2(x1, x1b, p, ff, wpe, wg, g, b, *, tm):
    m = x1.shape[0]
    cast_w = wg.dtype != BF16
    assert not cast_w or m == tm
    row = pl.BlockSpec((tm, D_MODEL), lambda i: (i, 0))
    vec = pl.BlockSpec((1, D_MODEL), lambda i: (0, 0))
    oshape = jax.ShapeDtypeStruct((m, D_MODEL), F32)
    wb_specs = [pl.BlockSpec((PLE_DIM, D_MODEL), lambda i: (0, 0)),
                pl.BlockSpec((D_MODEL, D_MODEL), lambda i: (0, 0))]
    wb_shapes = [jax.ShapeDtypeStruct(wpe.shape, BF16), jax.ShapeDtypeStruct(wg.shape, BF16)]
    return pl.pallas_call(
        _pe_ln2_kernel,
        grid=(m // tm,),
        in_specs=[row, row, pl.BlockSpec((tm, PLE_DIM), lambda i: (i, 0)), row,
                  pl.BlockSpec((PLE_DIM, D_MODEL), lambda i: (0, 0), pipeline_mode=pl.Buffered(1)),
                  pl.BlockSpec((D_MODEL, D_MODEL), lambda i: (0, 0), pipeline_mode=pl.Buffered(1)),
                  vec, vec],
        out_specs=[row] + wb_specs if cast_w else row,
        out_shape=[oshape] + wb_shapes if cast_w else oshape,
        compiler_params=_params("arbitrary"),
        name="pe_ln2_cast" if cast_w else "pe_ln2",
    )(x1, x1b, p, ff, wpe, wg, g, b)


def _rope_tables(pos):
    inv = ROPE_BASE ** (-jnp.arange(HALF, dtype=F32) * 2.0 / HEAD_DIM)
    ang = pos[:, None] * inv[None, :]
    return jnp.cos(ang), jnp.sin(ang)


def _decay_tables():
    lg = jnp.log(1.0 - 2.0 ** (-5.0 - jnp.arange(HEADS, dtype=F32)))
    t = jnp.arange(CHUNK, dtype=F32)
    causal = t[:, None] >= t[None, :]
    intra = jnp.exp(jnp.where(causal, (t[:, None] - t[None, :]) * lg[:, None, None], -jnp.inf))
    q_decay = jnp.exp(lg[:, None] * (t + 1.0))
    k_decay = jnp.exp(lg[:, None] * (CHUNK - 1.0 - t))
    return intra, jnp.broadcast_to(q_decay[:, :, None], (HEADS, CHUNK, HEAD_DIM)), k_decay


def _tail(x, mixed, p, w, *, tm, tm_ffn, tf):
    casts = {}
    res = _outproj(mixed, w["out"], x, w["ln1_g"], w["ln1_b"], tm=tm)
    x1, x1b = res[0], res[1]
    if len(res) > 2:
        casts["out"] = res[2]
    res = _ffn(x1b, w["ff1"], w["ff2"], tm=tm_ffn, tf=tf)
    if isinstance(res, (list, tuple)):
        ff, casts["ff1"], casts["ff2"] = res
    else:
        ff = res
    res = _pe_ln2(x1, x1b, p, ff, w["pe"], w["pe_gate"], w["ln2_g"], w["ln2_b"], tm=tm)
    if isinstance(res, (list, tuple)):
        y, casts["pe"], casts["pe_gate"] = res
    else:
        y = res
    return y, casts


def kernel(x_prompt, x_sample, state_ret, state_mlstm_C, state_mlstm_n, state_mlstm_m, p_prompt, p_sample, w_in, b_gate, ret_gn_w, mlstm_gn_w, w_out, ln1_g, ln1_b, w_ff1, w_ff2, w_pe, w_pe_gate, ln2_g, ln2_b):
    batch, seq, _ = x_prompt.shape
    nb = x_sample.shape[0]

    w_in_t = w_in[0].T
    b8 = jnp.pad(b_gate[0].astype(F32), (0, LANES - 2 * HEADS)).reshape(1, LANES)
    w = {
        "out": w_out[0], "ff1": w_ff1[0], "ff2": w_ff2[0], "pe": w_pe[0], "pe_gate": w_pe_gate[0],
        "ln1_g": ln1_g[0].reshape(1, D_MODEL), "ln1_b": ln1_b[0].reshape(1, D_MODEL),
        "ln2_g": ln2_g[0].reshape(1, D_MODEL), "ln2_b": ln2_b[0].reshape(1, D_MODEL),
    }
    rgn = ret_gn_w[0].reshape(1, WIDTH)
    mgn = mlstm_gn_w[0].reshape(1, WIDTH)
    intra, qdec, kdec = _decay_tables()

    xs = x_sample.reshape(nb, D_MODEL)
    xsb, gs_tok, _ = _mgate(xs, w_in_t, b8, tm=nb)
    ctab, stab = _rope_tables(jnp.full((nb,), PAST_LEN, dtype=F32))
    proj_s, w_in_b = _inproj(xsb, w_in_t, ctab, stab, tm=nb)
    proj_s = proj_s.astype(F32)
    m0 = jnp.pad(state_mlstm_m[0], ((0, 0), (0, LANES - HEADS)))
    mt, dw, iw, emt = _sample_gates(gs_tok, m0)
    scal = jnp.concatenate([mt[:, :HEADS], dw[:, :HEADS], iw[:, :HEADS], emt[:, :HEADS]], axis=1)
    m_s = mt[:, :HEADS].reshape(1, nb, HEADS)

    xp = x_prompt.reshape(batch * seq, D_MODEL)
    xpb, g_tok, g_t = _mgate(xp, w_in_t, b8, tm=GATE_TM)
    ctab, stab = _rope_tables(jnp.arange(seq, dtype=F32))
    rows_h = proj_s[nb - HOSTED_TOKENS:]
    qk = jnp.concatenate([rows_h[:, 0:2 * WIDTH], rows_h[:, 4 * WIDTH:6 * WIDTH]], axis=1)
    cols_h = jnp.transpose(qk.reshape(HOSTED_TOKENS, 4 * HEADS, HEAD_DIM), (0, 2, 1))
    proj_p, mixed_h, s_h, c_h, n_h = _inproj(
        xpb, w_in_b, ctab, stab, tm=INPROJ_TM,
        sample=(scal, cols_h, rows_h, state_ret, state_mlstm_C, state_mlstm_n, rgn, mgn))
    mixed_m, s_s, c_s, n_s, mixed_p, s_p, c_p, n_p, m_rows = _mixers(
        scal, proj_s, state_ret, state_mlstm_C, state_mlstm_n,
        proj_p, g_tok, g_t, intra, qdec, kdec, rgn, mgn, (s_h, c_h, n_h), batch=batch, seq=seq)
    assert mixed_m.shape[0] + HOSTED_TOKENS == nb
    mixed_s = jnp.concatenate([mixed_m, mixed_h], axis=0)
    m_p = m_rows[:, :HEADS, 0].reshape(1, batch, HEADS)

    y_s, w_bf16 = _tail(xs, mixed_s, p_sample[0].reshape(nb, PLE_DIM), w,
                        tm=nb, tm_ffn=nb, tf=CAST_FFN_TF)
    w = {**w, **w_bf16}
    y_p, _ = _tail(xp, mixed_p, p_prompt[0].reshape(batch * seq, PLE_DIM), w,
                   tm=TAIL_TM, tm_ffn=FFN_TM, tf=FFN_TF)

    return (y_p.reshape(batch, seq, D_MODEL), y_s.reshape(nb, 1, D_MODEL),
            s_p, c_p, n_p, m_p, s_s, c_s, n_s, m_s)
```

```python
import jax
import jax.numpy as jnp
import numpy as np
from jax import lax
from jax.experimental import pallas as pl
from jax.experimental.pallas import tpu as pltpu

F32 = jnp.float32
BF16 = jnp.bfloat16

D_MODEL = 2048
HEADS = 4
HEAD_DIM = 256
HALF = HEAD_DIM // 2
WIDTH = HEADS * HEAD_DIM
D_FF = 4 * D_MODEL
PLE_DIM = 256
CHUNK = 128
PAST_LEN = 16384
ROPE_BASE = 10000.0
LN_EPS = 1e-5
GN_EPS = 1e-6
DEPTH = 1
ALPHA = (2 * DEPTH) ** 0.25
QK_SCALE = HEAD_DIM ** -0.5
LANES = 128
SUBLANES = 8
MXU_ROWS = 256
VMEM_LIMIT = 56 * 1024 * 1024
MIXERS_VMEM_LIMIT = 60 * 1024 * 1024
SAMPLE_TOKENS_PER_STEP = 2
HOSTED_TOKENS = 64
MIXER_CHUNKS_PER_STEP = 2
STATE_RING_SLOTS = 3
INPROJ_VMEM_LIMIT = 60 * 1024 * 1024
GATE_TM = 1024
INPROJ_TM = 2048
INPROJ_PIECE_ROWS = 1024
TAIL_TM = 512
FFN_TM = 1024
FFN_TF = 1024
CAST_FFN_TF = 512

RET_G = tuple(1.0 - 2.0 ** (-5.0 - h) for h in range(HEADS))
RET_LOG_G = tuple(float(np.log(np.float32(g))) for g in RET_G)
RET_STATE_DECAY = tuple(float(np.exp(np.float32(lg) * np.float32(CHUNK))) for lg in RET_LOG_G)

_NT = (((1,), (1,)), ((), ()))
_TN = (((0,), (0,)), ((), ()))


def _params(*sem, vmem=None):
    return pltpu.CompilerParams(dimension_semantics=sem, vmem_limit_bytes=vmem or VMEM_LIMIT)


def _sigmoid(x):
    return 0.5 * jnp.tanh(0.5 * x) + 0.5


def _log_sigmoid(x):
    return jnp.minimum(x, 0.0) - jnp.log(1.0 + jnp.exp(-jnp.abs(x)))


def _layer_norm(y, g, b):
    mu = jnp.mean(y, axis=-1, keepdims=True)
    yc = y - mu
    var = jnp.mean(jnp.square(yc), axis=-1, keepdims=True)
    return yc * lax.rsqrt(var + LN_EPS) * g + b


def _row_splits(tm, rows=None):
    if rows is not None and tm >= rows:
        return [slice(r, r + rows) for r in range(0, tm, rows)]
    if tm < 2 * MXU_ROWS:
        return [slice(0, tm)]
    return [slice(0, tm // 2), slice(tm // 2, tm)]


def _head_norm(o):
    mu = jnp.mean(o, axis=-1, keepdims=True)
    oc = o - mu
    var = jnp.mean(jnp.square(oc), axis=-1, keepdims=True)
    return oc * lax.rsqrt(var + GN_EPS)


RG_GROUP, MO_GROUP = 3, 7


def _inproj_body(x_ref, wb_ref, c_ref, s_ref, o_ref, hosted=()):
    j = pl.program_id(1)
    is_gate = (j == RG_GROUP) | (j == MO_GROUP)
    pieces = [(h, rows) for h in range(HEADS) for rows in _row_splits(x_ref.shape[0], INPROJ_PIECE_ROWS)]

    def piece_acc(h, rows):
        return lax.dot_general(x_ref[rows, :], wb_ref[h * HEAD_DIM:(h + 1) * HEAD_DIM, :], _NT,
                               preferred_element_type=F32)

    def gate_piece(h, rows):
        acc = piece_acc(h, rows)
        sig = _sigmoid(acc)
        o_ref[rows, h * HEAD_DIM:(h + 1) * HEAD_DIM] = jnp.where(
            j == RG_GROUP, acc * sig, sig).astype(o_ref.dtype)

    def rotary_piece(h, rows):
        scale = jnp.where(j == 1, QK_SCALE, 1.0).astype(F32)
        lo = h * HEAD_DIM
        c = c_ref[rows, :] * scale
        s = s_ref[rows, :] * scale
        acc = piece_acc(h, rows)
        x1 = acc[:, :HALF]
        x2 = acc[:, HALF:]
        o_ref[rows, lo:lo + HALF] = (x1 * c - x2 * s).astype(o_ref.dtype)
        o_ref[rows, lo + HALF:lo + HEAD_DIM] = (x1 * s + x2 * c).astype(o_ref.dtype)

    def plain_piece(h, rows):
        scale = jnp.where(j == MO_GROUP - 2, QK_SCALE, 1.0).astype(F32)
        o_ref[rows, h * HEAD_DIM:(h + 1) * HEAD_DIM] = (piece_acc(h, rows) * scale).astype(o_ref.dtype)

    def run(piece):
        for n, (h, rows) in enumerate(pieces):
            if n < len(hosted):
                hosted[n]()
            piece(h, rows)

    is_rotary = j <= 1
    pl.when(is_gate)(lambda: run(gate_piece))
    pl.when(is_rotary)(lambda: run(rotary_piece))
    pl.when(jnp.logical_not(is_gate | is_rotary))(lambda: run(plain_piece))


def _inproj_cast_kernel(x_ref, wt_ref, c_ref, s_ref, o_ref, wb_ref):
    wb_ref[...] = wt_ref[...].astype(BF16)
    _inproj_body(x_ref, wb_ref, c_ref, s_ref, o_ref)


def _inproj_host_kernel(x_ref, wt_ref, c_ref, s_ref, scal_ref, cols_ref, rows_ref, s0_ref, c0_ref,
                        n0_ref, rgn_ref, mgn_ref, o_ref, smix_ref, ss_ref, sc_ref, sn_ref,
                        sbuf, cbuf, sems):
    th = cols_ref.shape[0]
    step = pl.program_id(0) * pl.num_programs(1) + pl.program_id(1)
    n_steps = pl.num_programs(0) * pl.num_programs(1)
    first = scal_ref.shape[0] - rows_ref.shape[0]

    def fetch(s):
        slot = s % STATE_RING_SLOTS
        tok = pl.ds(first + s * th, th)
        return (pltpu.make_async_copy(s0_ref.at[:, tok], sbuf.at[slot], sems.at[0, slot]),
                pltpu.make_async_copy(c0_ref.at[:, tok], cbuf.at[slot], sems.at[1, slot]))

    @pl.when(step == 0)
    def _():
        for s in range(STATE_RING_SLOTS - 1):
            for copy in fetch(s):
                copy.start()

    @pl.when(step + STATE_RING_SLOTS - 1 < n_steps)
    def _():
        for copy in fetch(step + STATE_RING_SLOTS - 1):
            copy.start()

    for copy in fetch(step):
        copy.wait()
    slot = step % STATE_RING_SLOTS
    hosted = []
    for t in range(th):
        local = step * th + t
        hosted += _sample_token_chunks(first + local, local, t, scal_ref, cols_ref, rows_ref,
                                       sbuf.at[slot], cbuf.at[slot], n0_ref, rgn_ref, mgn_ref,
                                       smix_ref, ss_ref, sc_ref, sn_ref)
    _inproj_body(x_ref, wt_ref, c_ref, s_ref, o_ref, hosted)


def _inproj(xb, wt, cos, sin, *, tm, sample=None):
    m = xb.shape[0]
    n_i = m // tm
    n_pos = cos.shape[0] // tm
    wspec = pl.BlockSpec((WIDTH, D_MODEL), lambda i, j: (j, 0))
    ospec = pl.BlockSpec((tm, WIDTH), lambda i, j: (i, j))
    oshape = jax.ShapeDtypeStruct((m, 8 * WIDTH), BF16)
    tspec = pl.BlockSpec((tm, HALF), lambda i, j: (i % n_pos, 0))
    in_specs = [pl.BlockSpec((tm, D_MODEL), lambda i, j: (i, 0)), wspec, tspec, tspec]
    if wt.dtype != BF16:
        assert n_i == 1 and sample is None
        return pl.pallas_call(
            _inproj_cast_kernel,
            grid=(1, 8),
            in_specs=in_specs,
            out_specs=[ospec, wspec],
            out_shape=[oshape, jax.ShapeDtypeStruct((8 * WIDTH, D_MODEL), BF16)],
            compiler_params=_params("arbitrary", "arbitrary"),
            name="inproj_cast",
        )(xb, wt, cos, sin)
    scal, cols_s, rows_s, s0, c0, n0, rgn, mgn = sample
    nh = rows_s.shape[0]
    th = nh // (8 * n_i)
    first = (scal.shape[0] - nh) // th
    assert nh == 8 * n_i * th and first * th + nh == scal.shape[0]
    const2 = lambda i, j: (0, 0)
    sstate = pl.BlockSpec((1, th, HEADS, HEAD_DIM, HEAD_DIM),
                          lambda i, j: (0, first + i * 8 + j, 0, 0, 0))
    snorm = pl.BlockSpec((1, th, HEADS, HEAD_DIM), lambda i, j: (0, first + i * 8 + j, 0, 0))
    return pl.pallas_call(
        _inproj_host_kernel,
        grid=(n_i, 8),
        in_specs=in_specs + [
            pl.BlockSpec(memory_space=pltpu.SMEM),
            pl.BlockSpec((th, HEAD_DIM, 4 * HEADS), lambda i, j: (i * 8 + j, 0, 0)),
            pl.BlockSpec((nh, 8 * WIDTH), const2, pipeline_mode=pl.Buffered(1)),
            pl.BlockSpec(memory_space=pl.ANY), pl.BlockSpec(memory_space=pl.ANY), snorm,
            pl.BlockSpec((1, WIDTH), const2),
            pl.BlockSpec((1, WIDTH), const2),
        ],
        out_specs=[ospec, pl.BlockSpec((nh, 2 * WIDTH), const2), sstate, sstate, snorm],
        out_shape=[oshape, jax.ShapeDtypeStruct((nh, 2 * WIDTH), F32),
                   jax.ShapeDtypeStruct(s0.shape, F32), jax.ShapeDtypeStruct(c0.shape, F32),
                   jax.ShapeDtypeStruct(n0.shape, F32)],
        scratch_shapes=[pltpu.VMEM((STATE_RING_SLOTS, 1, th, HEADS, HEAD_DIM, HEAD_DIM), F32),
                        pltpu.VMEM((STATE_RING_SLOTS, 1, th, HEADS, HEAD_DIM, HEAD_DIM), F32),
                        pltpu.SemaphoreType.DMA((2, STATE_RING_SLOTS))],
        compiler_params=_params("arbitrary", "arbitrary", vmem=INPROJ_VMEM_LIMIT),
        name="inproj",
    )(xb, wt, cos, sin, scal, cols_s, rows_s, s0, c0, n0, rgn, mgn)


def _mgate_kernel(x_ref, wt_ref, b_ref, xb_ref, g_ref, gt_ref):
    xb = x_ref[...].astype(BF16)
    xb_ref[...] = xb
    wt = jnp.concatenate([wt_ref[...], jnp.zeros((LANES - 2 * HEADS, D_MODEL), F32)], axis=0)
    g = lax.dot_general(xb, wt.astype(BF16), _NT, preferred_element_type=F32) + b_ref[...]
    lane = lax.broadcasted_iota(jnp.int32, g.shape, 1)
    g = jnp.where((lane >= HEADS) & (lane < 2 * HEADS), _log_sigmoid(g), g)
    r = lax.broadcasted_iota(jnp.int32, (CHUNK, CHUNK), 0)
    s = lax.broadcasted_iota(jnp.int32, (CHUNK, CHUNK), 1)
    tril = (r >= s).astype(BF16)
    g1 = g.astype(BF16)
    res = g - g1.astype(F32)
    g2 = res.astype(BF16)
    g3 = (res - g2.astype(F32)).astype(BF16)
    parts = []
    for c in range(g.shape[0] // CHUNK):
        sl = slice(c * CHUNK, (c + 1) * CHUNK)
        parts.append(jnp.dot(tril, g1[sl], preferred_element_type=F32)
                     + jnp.dot(tril, g2[sl], preferred_element_type=F32)
                     + jnp.dot(tril, g3[sl], preferred_element_type=F32))
    cs = parts[0] if len(parts) == 1 else jnp.concatenate(parts, axis=0)
    b = pltpu.roll(cs, HEADS, 1)
    u = pltpu.roll(g, 3 * HEADS, 1) - pltpu.roll(cs, 2 * HEADS, 1)
    g = jnp.where((lane >= 2 * HEADS) & (lane < 3 * HEADS), b, g)
    g = jnp.where((lane >= 3 * HEADS) & (lane < 4 * HEADS), u, g)
    g_ref[...] = g
    gt_ref[...] = g.T[0:4 * HEADS, :]


def _mgate(x, w_in_t, b8, *, tm):
    m = x.shape[0]
    gate_rows = 8 * WIDTH // (2 * HEADS)
    return pl.pallas_call(
        _mgate_kernel,
        grid=(m // tm,),
        in_specs=[
            pl.BlockSpec((tm, D_MODEL), lambda i: (i, 0)),
            pl.BlockSpec((2 * HEADS, D_MODEL), lambda i: (gate_rows, 0)),
            pl.BlockSpec((1, LANES), lambda i: (0, 0)),
        ],
        out_specs=[
            pl.BlockSpec((tm, D_MODEL), lambda i: (i, 0)),
            pl.BlockSpec((tm, LANES), lambda i: (i, 0)),
            pl.BlockSpec((4 * HEADS, tm), lambda i: (0, i)),
        ],
        out_shape=[
            jax.ShapeDtypeStruct((m, D_MODEL), BF16),
            jax.ShapeDtypeStruct((m, LANES), F32),
            jax.ShapeDtypeStruct((4 * HEADS, m), F32),
        ],
        compiler_params=_params("arbitrary"),
        name="mlstm_gates",
    )(x, w_in_t, b8)


def _mixer_chunk(rows, causal, qkv_ref, g_ref, gt_ref, intra_ref, dec_ref, kdec_ref, rgn_ref, mgn_ref,
                 mix_ref, s_ref, c_ref, n_ref, m_scr):
    def head(group, h):
        lo = group * WIDTH + h * HEAD_DIM
        return qkv_ref[rows, lo:lo + HEAD_DIM]

    gw = []
    for h in range(HEADS):
        b_col = g_ref[rows, 2 * HEADS + h:2 * HEADS + h + 1]
        u_row = gt_ref[3 * HEADS + h:3 * HEADS + h + 1, rows]
        m_prev = m_scr[h:h + 1, 0:1]
        dlog = jnp.where(causal, b_col + u_row, -jnp.inf)
        inter = b_col + m_prev
        mt = jnp.maximum(inter, jnp.max(dlog, axis=1, keepdims=True))
        m_new = mt[CHUNK - 1:CHUNK, :]
        b_last = b_col[CHUNK - 1:CHUNK, :]
        gw.append(dict(
            dw=jnp.exp(dlog - mt), iw=jnp.exp(inter - mt), emt=jnp.exp(-mt),
            sw=jnp.exp(b_last - m_new + u_row), sd=jnp.exp(b_last + m_prev - m_new)))
        m_scr[h:h + 1, :] = jnp.broadcast_to(m_new, (1, LANES))

    st = []
    for h in range(HEADS):
        for kind, (qg, state_ref) in enumerate(((0, s_ref), (MO_GROUP - 3, c_ref))):
            q = head(qg, h)
            k = head(qg + 1, h)
            old = state_ref[0, 0, h]
            kt = k.T
            st.append(dict(
                h=h, kind=kind, q=q, k=k, kt=kt, v=head(qg + 2, h), old=old,
                sc=jnp.dot(q, kt, preferred_element_type=F32),
                inter=jnp.dot(q, old.astype(BF16), preferred_element_type=F32)))

    for e in st:
        h = e["h"]
        if e["kind"] == 0:
            kd = (e["kt"].astype(F32) * kdec_ref[h:h + 1, :]).astype(BF16)
            s_ref[0, 0, h] = (e["old"] * RET_STATE_DECAY[h]
                              + jnp.dot(kd, e["v"], preferred_element_type=F32))
        else:
            w = gw[h]
            ksw = (e["kt"].astype(F32) * w["sw"]).astype(BF16)
            c_ref[0, 0, h] = w["sd"] * e["old"] + jnp.dot(ksw, e["v"], preferred_element_type=F32)
            n_old = n_ref[0, 0, h:h + 1, :]
            e["qn"] = jnp.sum(e["q"].astype(F32) * n_old, axis=1, keepdims=True)
            sw8 = jnp.broadcast_to(w["sw"], (SUBLANES, CHUNK)).astype(BF16)
            n_ref[0, 0, h:h + 1, :] = (w["sd"] * n_old
                                       + jnp.dot(sw8, e["k"], preferred_element_type=F32)[0:1, :])

    for e in st:
        h = e["h"]
        e["sc"] = e["sc"] * (intra_ref[h] if e["kind"] == 0 else gw[h]["dw"])
        e["pv"] = jnp.dot(e["sc"].astype(BF16), e["v"], preferred_element_type=F32)

    for e in st:
        e["rows"] = rows
        if e["kind"]:
            e["iw"] = gw[e["h"]]["iw"]
            e["emt"] = gw[e["h"]]["emt"]
    return st


def _mixer_tail(streams, qkv_ref, dec_ref, rgn_ref, mgn_ref, mix_ref):
    dens = [jnp.sum(e["sc"], axis=1, keepdims=True) if e["kind"] else None for e in streams]
    outs = []
    for e, den in zip(streams, dens):
        if e["kind"] == 0:
            outs.append(e["pv"] + e["inter"] * dec_ref[e["h"]])
        else:
            num = e["pv"] + e["iw"] * e["inter"]
            den = den + e["iw"] * e["qn"]
            outs.append(num * (1.0 / jnp.maximum(jnp.abs(den), e["emt"])))
    mus = [jnp.mean(o, axis=-1, keepdims=True) for o in outs]
    cen = [o - mu for o, mu in zip(outs, mus)]
    var = [jnp.mean(jnp.square(oc), axis=-1, keepdims=True) for oc in cen]
    for e, oc, v in zip(streams, cen, var):
        rows = e["rows"]
        lo = e["h"] * HEAD_DIM
        gate_lo = (MO_GROUP if e["kind"] else RG_GROUP) * WIDTH + lo
        gain_ref = mgn_ref if e["kind"] else rgn_ref
        out_lo = e["kind"] * WIDTH + lo
        y = (oc * lax.rsqrt(v + GN_EPS) * gain_ref[:, lo:lo + HEAD_DIM]
             * qkv_ref[rows, gate_lo:gate_lo + HEAD_DIM].astype(F32))
        mix_ref[rows, out_lo:out_lo + HEAD_DIM] = y.astype(mix_ref.dtype)


def _mixers_kernel(scal_ref, rows_ref, s0_ref, c0_ref, n0_ref,
                   qkv_ref, g_ref, gt_ref, intra_ref, dec_ref, kdec_ref, rgn_ref, mgn_ref,
                   sh_ref, ch_ref, nh_ref,
                   smix_ref, ss_ref, sc_ref, sn_ref,
                   mix_ref, s_ref, c_ref, n_ref, m_ref, m_scr, sbuf, cbuf, sems):
    del sh_ref, ch_ref, nh_ref
    c = pl.program_id(1)
    step = pl.program_id(0) * pl.num_programs(1) + c
    n_steps = pl.num_programs(0) * pl.num_programs(1)
    tb = sbuf.shape[2]

    def fetch(s):
        slot = s % STATE_RING_SLOTS
        tok = pl.ds(s * tb, tb)
        return (pltpu.make_async_copy(s0_ref.at[:, tok], sbuf.at[slot], sems.at[0, slot]),
                pltpu.make_async_copy(c0_ref.at[:, tok], cbuf.at[slot], sems.at[1, slot]))

    @pl.when(step == 0)
    def _():
        for s in range(STATE_RING_SLOTS - 1):
            for copy in fetch(s):
                copy.start()

    @pl.when(step + STATE_RING_SLOTS - 1 < n_steps)
    def _():
        for copy in fetch(step + STATE_RING_SLOTS - 1):
            copy.start()

    for copy in fetch(step):
        copy.wait()
    slot = step % STATE_RING_SLOTS
    _sample_tokens(step, scal_ref, rows_ref, sbuf.at[slot], cbuf.at[slot],
                   n0_ref, rgn_ref, mgn_ref, smix_ref, ss_ref, sc_ref, sn_ref)

    @pl.when(c == 0)
    def _():
        s_ref[...] = jnp.zeros_like(s_ref)
        c_ref[...] = jnp.zeros_like(c_ref)
        n_ref[...] = jnp.zeros_like(n_ref)
        m_scr[...] = jnp.zeros_like(m_scr)

    ti = lax.broadcasted_iota(jnp.int32, (CHUNK, CHUNK), 0)
    si = lax.broadcasted_iota(jnp.int32, (CHUNK, CHUNK), 1)
    causal = ti >= si

    streams = []
    for ci in range(MIXER_CHUNKS_PER_STEP):
        streams += _mixer_chunk(slice(ci * CHUNK, (ci + 1) * CHUNK), causal, qkv_ref, g_ref, gt_ref,
                                intra_ref, dec_ref, kdec_ref, rgn_ref, mgn_ref, mix_ref, s_ref, c_ref,
                                n_ref, m_scr)
    _mixer_tail(streams, qkv_ref, dec_ref, rgn_ref, mgn_ref, mix_ref)

    @pl.when(c == pl.num_programs(1) - 1)
    def _():
        m_ref[0] = m_scr[...]


def _mixers(scal, rows_s, s0, c0, n0, qkv, g_tok, g_t, intra, qdec, kdec, rgn, mgn, hosted, *,
            batch, seq):
    rows = MIXER_CHUNKS_PER_STEP * CHUNK
    nc = seq // rows
    nb = rows_s.shape[0]
    tb = SAMPLE_TOKENS_PER_STEP
    nm = batch * nc * tb
    assert nm <= nb
    any_spec = pl.BlockSpec(memory_space=pl.ANY)
    row = lambda b, c: (b * nc + c, 0)
    const2 = lambda b, c: (0, 0)
    state = pl.BlockSpec((1, 1, HEADS, HEAD_DIM, HEAD_DIM), lambda b, c: (0, b, 0, 0, 0))
    sstate = pl.BlockSpec((1, tb, HEADS, HEAD_DIM, HEAD_DIM), lambda b, c: (0, b * nc + c, 0, 0, 0))
    snorm = pl.BlockSpec((1, tb, HEADS, HEAD_DIM), lambda b, c: (0, b * nc + c, 0, 0))
    return pl.pallas_call(
        _mixers_kernel,
        grid=(batch, nc),
        in_specs=[
            pl.BlockSpec(memory_space=pltpu.SMEM),
            pl.BlockSpec((nb, 8 * WIDTH), const2, pipeline_mode=pl.Buffered(1)),
            any_spec, any_spec, snorm,
            pl.BlockSpec((rows, 8 * WIDTH), row),
            pl.BlockSpec((rows, LANES), row),
            pl.BlockSpec((4 * HEADS, rows), lambda b, c: (0, b * nc + c)),
            pl.BlockSpec((HEADS, CHUNK, CHUNK), lambda b, c: (0, 0, 0)),
            pl.BlockSpec((HEADS, CHUNK, HEAD_DIM), lambda b, c: (0, 0, 0)),
            pl.BlockSpec((HEADS, CHUNK), const2),
            pl.BlockSpec((1, WIDTH), const2),
            pl.BlockSpec((1, WIDTH), const2),
            any_spec, any_spec, any_spec,
        ],
        out_specs=[
            pl.BlockSpec((nm, 2 * WIDTH), const2), sstate, sstate, snorm,
            pl.BlockSpec((rows, 2 * WIDTH), row),
            state,
            state,
            pl.BlockSpec((1, 1, HEADS, HEAD_DIM), lambda b, c: (0, b, 0, 0)),
            pl.BlockSpec((1, SUBLANES, LANES), lambda b, c: (b, 0, 0)),
        ],
        out_shape=[
            jax.ShapeDtypeStruct((nm, 2 * WIDTH), F32),
            jax.ShapeDtypeStruct(s0.shape, F32),
            jax.ShapeDtypeStruct(c0.shape, F32),
            jax.ShapeDtypeStruct(n0.shape, F32),
            jax.ShapeDtypeStruct((batch * seq, 2 * WIDTH), BF16),
            jax.ShapeDtypeStruct((1, batch, HEADS, HEAD_DIM, HEAD_DIM), F32),
            jax.ShapeDtypeStruct((1, batch, HEADS, HEAD_DIM, HEAD_DIM), F32),
            jax.ShapeDtypeStruct((1, batch, HEADS, HEAD_DIM), F32),
            jax.ShapeDtypeStruct((batch, SUBLANES, LANES), F32),
        ],
        scratch_shapes=[pltpu.VMEM((SUBLANES, LANES), F32),
                        pltpu.VMEM((STATE_RING_SLOTS, 1, tb, HEADS, HEAD_DIM, HEAD_DIM), F32),
                        pltpu.VMEM((STATE_RING_SLOTS, 1, tb, HEADS, HEAD_DIM, HEAD_DIM), F32),
                        pltpu.SemaphoreType.DMA((2, STATE_RING_SLOTS))],
        input_output_aliases={13: 1, 14: 2, 15: 3},
        compiler_params=_params("arbitrary", "arbitrary", vmem=MIXERS_VMEM_LIMIT),
        name="mixers",
    )(scal, rows_s, s0, c0, n0, qkv, g_tok, g_t, intra, qdec, kdec, rgn, mgn, *hosted)


def _sample_gate_kernel(g_ref, m0_ref, mt_ref, dw_ref, iw_ref, emt_ref):
    ig = g_ref[...]
    lf = pltpu.roll(ig, LANES - HEADS, 1)
    inter = lf + m0_ref[...]
    mt = jnp.maximum(inter, ig)
    mt_ref[...] = mt
    dw_ref[...] = jnp.exp(ig - mt)
    iw_ref[...] = jnp.exp(inter - mt)
    emt_ref[...] = jnp.exp(-mt)


def _sample_gates(g_tok, m0_pad):
    m = g_tok.shape[0]
    spec = pl.BlockSpec((m, LANES), lambda: (0, 0))
    return pl.pallas_call(
        _sample_gate_kernel,
        in_specs=[spec, spec],
        out_specs=[spec] * 4,
        out_shape=[jax.ShapeDtypeStruct((m, LANES), F32)] * 4,
        name="sample_gates",
    )(g_tok, m0_pad)


def _sample_tokens(step, scal_ref, rows_ref, s0_ref, c0_ref, n0_ref,
                   rgn_ref, mgn_ref, mix_ref, s_ref, c_ref, n_ref):
    first = lax.broadcasted_iota(jnp.int32, (4 * HEADS, HEAD_DIM), 0) == 0
    for t in range(SAMPLE_TOKENS_PER_STEP):
        b = step * SAMPLE_TOKENS_PER_STEP + t

        def row(group, h):
            lo = group * WIDTH + h * HEAD_DIM
            return rows_ref[pl.ds(b, 1), lo:lo + HEAD_DIM]

        def outer(k, v):
            kp = jnp.where(first, jnp.broadcast_to(k, first.shape), 0.0).astype(BF16)
            vp = jnp.where(first, jnp.broadcast_to(v, first.shape), 0.0).astype(BF16)
            return lax.dot_general(kp, vp, _TN, preferred_element_type=F32)

        def apply(q_row, state):
            q8 = jnp.broadcast_to(q_row, (SUBLANES, HEAD_DIM)).astype(BF16)
            return jnp.dot(q8, state.astype(BF16), preferred_element_type=F32)[0:1, :]

        dw = [scal_ref[b, HEADS + h] for h in range(HEADS)]
        iw = [scal_ref[b, 2 * HEADS + h] for h in range(HEADS)]
        emt = [scal_ref[b, 3 * HEADS + h] for h in range(HEADS)]
        kv_s = [outer(row(1, h), row(2, h)) for h in range(HEADS)]
        kv_c = [outer(row(5, h), dw[h] * row(6, h)) for h in range(HEADS)]
        s_new, c_new = [], []
        for h in range(HEADS):
            s_new.append(RET_G[h] * s0_ref[0, t, h] + kv_s[h])
            s_ref[0, t, h] = s_new[h]
            c_new.append(iw[h] * c0_ref[0, t, h] + kv_c[h])
            c_ref[0, t, h] = c_new[h]
        o = [apply(row(0, h), s_new[h]) for h in range(HEADS)]
        num = [apply(row(4, h), c_new[h]) for h in range(HEADS)]
        for h in range(HEADS):
            lo = h * HEAD_DIM
            mix_ref[pl.ds(b, 1), lo:lo + HEAD_DIM] = (_head_norm(o[h]) * rgn_ref[:, lo:lo + HEAD_DIM]
                                               * row(RG_GROUP, h))
            n_new = iw[h] * n0_ref[0, t, h:h + 1, :] + dw[h] * row(5, h)
            n_ref[0, t, h:h + 1, :] = n_new
            den = jnp.sum(row(4, h) * n_new, axis=1, keepdims=True)
            hid = num[h] * (1.0 / jnp.maximum(jnp.abs(den), emt[h]))
            mix_ref[pl.ds(b, 1), WIDTH + lo:WIDTH + lo + HEAD_DIM] = (
                _head_norm(hid) * mgn_ref[:, lo:lo + HEAD_DIM] * row(MO_GROUP, h))


def _sample_token_chunks(b, local, t, scal_ref, cols_ref, rows_ref, s0_ref, c0_ref, n0_ref, rgn_ref,
                         mgn_ref, mix_ref, s_ref, c_ref, n_ref):
    def row(group, h):
        lo = group * WIDTH + h * HEAD_DIM
        return rows_ref[pl.ds(local, 1), lo:lo + HEAD_DIM]

    def col(group, h):
        return cols_ref[t, :, group * HEADS + h:group * HEADS + h + 1]

    def rounded(a):
        return a.astype(BF16).astype(F32)

    def chunk(h):
        lo = h * HEAD_DIM
        s_new = RET_G[h] * s0_ref[0, t, h] + col(1, h) * row(2, h)
        s_ref[0, t, h] = s_new
        o = jnp.sum(col(0, h) * rounded(s_new), axis=0, keepdims=True)
        mix_ref[pl.ds(local, 1), lo:lo + HEAD_DIM] = (
            _head_norm(o) * rgn_ref[:, lo:lo + HEAD_DIM] * row(RG_GROUP, h))

        dw = scal_ref[b, HEADS + h]
        iw = scal_ref[b, 2 * HEADS + h]
        emt = scal_ref[b, 3 * HEADS + h]
        c_new = iw * c0_ref[0, t, h] + col(3, h) * rounded(dw * row(6, h))
        c_ref[0, t, h] = c_new
        n_new = iw * n0_ref[0, t, h:h + 1, :] + dw * row(5, h)
        n_ref[0, t, h:h + 1, :] = n_new
        num = jnp.sum(col(2, h) * rounded(c_new), axis=0, keepdims=True)
        den = jnp.sum(row(4, h) * n_new, axis=1, keepdims=True)
        hid = num * (1.0 / jnp.maximum(jnp.abs(den), emt))
        mix_ref[pl.ds(local, 1), WIDTH + lo:WIDTH + lo + HEAD_DIM] = (
            _head_norm(hid) * mgn_ref[:, lo:lo + HEAD_DIM] * row(MO_GROUP, h))

    return [lambda h=h: chunk(h) for h in range(HEADS)]


def _outproj_kernel(a_ref, w_ref, x_ref, g_ref, b_ref, x1_ref, x1b_ref, *wb_out):
    if wb_out:
        wb_out[0][...] = w_ref[...].astype(BF16)
        w_ref = wb_out[0]
    tm = a_ref.shape[0]
    for rows in _row_splits(tm):
        mix = jnp.dot(a_ref[rows, :].astype(BF16), w_ref[...], preferred_element_type=F32)
        x1 = _layer_norm(ALPHA * x_ref[rows, :] + mix, g_ref[...], b_ref[...])
        x1_ref[rows, :] = x1
        x1b_ref[rows, :] = x1.astype(BF16)


def _outproj(a, w, x, g, b, *, tm):
    m = a.shape[0]
    cast_w = w.dtype != BF16
    assert not cast_w or m == tm
    row = pl.BlockSpec((tm, D_MODEL), lambda i: (i, 0))
    vec = pl.BlockSpec((1, D_MODEL), lambda i: (0, 0))
    out_specs = [row, row]
    out_shape = [jax.ShapeDtypeStruct((m, D_MODEL), F32), jax.ShapeDtypeStruct((m, D_MODEL), BF16)]
    if cast_w:
        out_specs.append(pl.BlockSpec((D_MODEL, D_MODEL), lambda i: (0, 0)))
        out_shape.append(jax.ShapeDtypeStruct((D_MODEL, D_MODEL), BF16))
    return pl.pallas_call(
        _outproj_kernel,
        grid=(m // tm,),
        in_specs=[row, pl.BlockSpec((D_MODEL, D_MODEL), lambda i: (0, 0), pipeline_mode=pl.Buffered(1)),
                  row, vec, vec],
        out_specs=out_specs,
        out_shape=out_shape,
        compiler_params=_params("arbitrary"),
        name="outproj_ln1_cast" if cast_w else "outproj_ln1",
    )(a, w, x, g, b)


def _ffn_kernel(x1b_ref, w1_ref, w2_ref, o_ref, *wb_out):
    @pl.when(pl.program_id(1) == 0)
    def _():
        o_ref[...] = jnp.zeros_like(o_ref)

    w1 = w1_ref[...]
    w2 = w2_ref[...]
    if wb_out:
        w1 = w1.astype(BF16)
        w2 = w2.astype(BF16)
        wb_out[0][...] = w1
        wb_out[1][...] = w2
    hid = jnp.dot(x1b_ref[...], w1, preferred_element_type=F32)
    hid = jnp.square(jnp.maximum(hid, 0.0)).astype(BF16)
    o_ref[...] += jnp.dot(hid, w2, preferred_element_type=F32)


def _ffn(x1b, w1, w2, *, tm, tf):
    m = x1b.shape[0]
    cast_w = w1.dtype != BF16
    assert not cast_w or m == tm
    row = pl.BlockSpec((tm, D_MODEL), lambda i, f: (i, 0))
    w1spec = pl.BlockSpec((D_MODEL, tf), lambda i, f: (0, f))
    w2spec = pl.BlockSpec((tf, D_MODEL), lambda i, f: (f, 0))
    oshape = jax.ShapeDtypeStruct((m, D_MODEL), F32)
    return pl.pallas_call(
        _ffn_kernel,
        grid=(m // tm, D_FF // tf),
        in_specs=[row, w1spec, w2spec],
        out_specs=[row, w1spec, w2spec] if cast_w else row,
        out_shape=[oshape, jax.ShapeDtypeStruct(w1.shape, BF16),
                   jax.ShapeDtypeStruct(w2.shape, BF16)] if cast_w else oshape,
        compiler_params=_params("arbitrary", "arbitrary"),
        name="ffn_cast" if cast_w else "ffn",
    )(x1b, w1, w2)


def _pe_ln2_kernel(x1_ref, x1b_ref, p_ref, ff_ref, wpe_ref, wg_ref, g_ref, b_ref, o_ref, *wb_out):
    if wb_out:
        wb_out[0][...] = wpe_ref[...].astype(BF16)
        wb_out[1][...] = wg_ref[...].astype(BF16)
        wpe_ref, wg_ref = wb_out
    for rows in _row_splits(x1_ref.shape[0]):
        gate = _sigmoid(jnp.dot(x1b_ref[rows, :], wg_ref[...], preferred_element_type=F32))
        pe = jnp.dot(p_ref[rows, :].astype(BF16), wpe_ref[...], preferred_element_type=F32)
        y = ALPHA * x1_ref[rows, :] + ff_ref[rows, :] + pe * gate
        o_ref[rows, :] = _layer_norm(y, g_ref[...], b_ref[...])


def _pe_ln2(x1, x1b, p, ff, wpe, wg, g, b, *, tm):
    m = x1.shape[0]
    cast_w = wg.dtype != BF16
    assert not cast_w or m == tm
    row = pl.BlockSpec((tm, D_MODEL), lambda i: (i, 0))
    vec = pl.BlockSpec((1, D_MODEL), lambda i: (0, 0))
    oshape = jax.ShapeDtypeStruct((m, D_MODEL), F32)
    wb_specs = [pl.BlockSpec((PLE_DIM, D_MODEL), lambda i: (0, 0)),
                pl.BlockSpec((D_MODEL, D_MODEL), lambda i: (0, 0))]
    wb_shapes = [jax.ShapeDtypeStruct(wpe.shape, BF16), jax.ShapeDtypeStruct(wg.shape, BF16)]
    return pl.pallas_call(
        _pe_ln2_kernel,
        grid=(m // tm,),
        in_specs=[row, row, pl.BlockSpec((tm, PLE_DIM), lambda i: (i, 0)), row,
                  pl.BlockSpec((PLE_DIM, D_MODEL), lambda i: (0, 0), pipeline_mode=pl.Buffered(1)),
                  pl.BlockSpec((D_MODEL, D_MODEL), lambda i: (0, 0), pipeline_mode=pl.Buffered(1)),
                  vec, vec],
        out_specs=[row] + wb_specs if cast_w else row,
        out_shape=[oshape] + wb_shapes if cast_w else oshape,
        compiler_params=_params("arbitrary"),
        name="pe_ln2_cast" if cast_w else "pe_ln2",
    )(x1, x1b, p, ff, wpe, wg, g, b)


def _rope_tables(pos):
    inv = ROPE_BASE ** (-jnp.arange(HALF, dtype=F32) * 2.0 / HEAD_DIM)
    ang = pos[:, None] * inv[None, :]
    return jnp.cos(ang), jnp.sin(ang)


def _decay_tables():
    lg = jnp.log(1.0 - 2.0 ** (-5.0 - jnp.arange(HEADS, dtype=F32)))
    t = jnp.arange(CHUNK, dtype=F32)
    causal = t[:, None] >= t[None, :]
    intra = jnp.exp(jnp.where(causal, (t[:, None] - t[None, :]) * lg[:, None, None], -jnp.inf))
    q_decay = jnp.exp(lg[:, None] * (t + 1.0))
    k_decay = jnp.exp(lg[:, None] * (CHUNK - 1.0 - t))
    return intra, jnp.broadcast_to(q_decay[:, :, None], (HEADS, CHUNK, HEAD_DIM)), k_decay


def _tail(x, mixed, p, w, *, tm, tm_ffn, tf):
    casts = {}
    res = _outproj(mixed, w["out"], x, w["ln1_g"], w["ln1_b"], tm=tm)
    x1, x1b = res[0], res[1]
    if len(res) > 2:
        casts["out"] = res[2]
    res = _ffn(x1b, w["ff1"], w["ff2"], tm=tm_ffn, tf=tf)
    if isinstance(res, (list, tuple)):
        ff, casts["ff1"], casts["ff2"] = res
    else:
        ff = res
    res = _pe_ln2(x1, x1b, p, ff, w["pe"], w["pe_gate"], w["ln2_g"], w["ln2_b"], tm=tm)
    if isinstance(res, (list, tuple)):
        y, casts["pe"], casts["pe_gate"] = res
    else:
        y = res
    return y, casts


def kernel(x_prompt, x_sample, state_ret, state_mlstm_C, state_mlstm_n, state_mlstm_m, p_prompt, p_sample, w_in, b_gate, ret_gn_w, mlstm_gn_w, w_out, ln1_g, ln1_b, w_ff1, w_ff2, w_pe, w_pe_gate, ln2_g, ln2_b):
    batch, seq, _ = x_prompt.shape
    nb = x_sample.shape[0]

    w_in_t = w_in[0].T
    b8 = jnp.pad(b_gate[0].astype(F32), (0, LANES - 2 * HEADS)).reshape(1, LANES)
    w = {
        "out": w_out[0], "ff1": w_ff1[0], "ff2": w_ff2[0], "pe": w_pe[0], "pe_gate": w_pe_gate[0],
        "ln1_g": ln1_g[0].reshape(1, D_MODEL), "ln1_b": ln1_b[0].reshape(1, D_MODEL),
        "ln2_g": ln2_g[0].reshape(1, D_MODEL), "ln2_b": ln2_b[0].reshape(1, D_MODEL),
    }
    rgn = ret_gn_w[0].reshape(1, WIDTH)
    mgn = mlstm_gn_w[0].reshape(1, WIDTH)
    intra, qdec, kdec = _decay_tables()

    xs = x_sample.reshape(nb, D_MODEL)
    xsb, gs_tok, _ = _mgate(xs, w_in_t, b8, tm=nb)
    ctab, stab = _rope_tables(jnp.full((nb,), PAST_LEN, dtype=F32))
    proj_s, w_in_b = _inproj(xsb, w_in_t, ctab, stab, tm=nb)
    proj_s = proj_s.astype(F32)
    m0 = jnp.pad(state_mlstm_m[0], ((0, 0), (0, LANES - HEADS)))
    mt, dw, iw, emt = _sample_gates(gs_tok, m0)
    scal = jnp.concatenate([mt[:, :HEADS], dw[:, :HEADS], iw[:, :HEADS], emt[:, :HEADS]], axis=1)
    m_s = mt[:, :HEADS].reshape(1, nb, HEADS)

    xp = x_prompt.reshape(batch * seq, D_MODEL)
    xpb, g_tok, g_t = _mgate(xp, w_in_t, b8, tm=GATE_TM)
    ctab, stab = _rope_tables(jnp.arange(seq, dtype=F32))
    rows_h = proj_s[nb - HOSTED_TOKENS:]
    qk = jnp.concatenate([rows_h[:, 0:2 * WIDTH], rows_h[:, 4 * WIDTH:6 * WIDTH]], axis=1)
    cols_h = jnp.transpose(qk.reshape(HOSTED_TOKENS, 4 * HEADS, HEAD_DIM), (0, 2, 1))
    proj_p, mixed_h, s_h, c_h, n_h = _inproj(
        xpb, w_in_b, ctab, stab, tm=INPROJ_TM,
        sample=(scal, cols_h, rows_h, state_ret, state_mlstm_C, state_mlstm_n, rgn, mgn))
    mixed_m, s_s, c_s, n_s, mixed_p, s_p, c_p, n_p, m_rows = _mixers(
        scal, proj_s, state_ret, state_mlstm_C, state_mlstm_n,
        proj_p, g_tok, g_t, intra, qdec, kdec, rgn, mgn, (s_h, c_h, n_h), batch=batch, seq=seq)
    assert mixed_m.shape[0] + HOSTED_TOKENS == nb
    mixed_s = jnp.concatenate([mixed_m, mixed_h], axis=0)
    m_p = m_rows[:, :HEADS, 0].reshape(1, batch, HEADS)

    y_s, w_bf16 = _tail(xs, mixed_s, p_sample[0].reshape(nb, PLE_DIM), w,
                        tm=nb, tm_ffn=nb, tf=CAST_FFN_TF)
    w = {**w, **w_bf16}
    y_p, _ = _tail(xp, mixed_p, p_prompt[0].reshape(batch * seq, PLE_DIM), w,
                   tm=TAIL_TM, tm_ffn=FFN_TM, tf=FFN_TF)

    return (y_p.reshape(batch, seq, D_MODEL), y_s.reshape(nb, 1, D_MODEL),
            s_p, c_p, n_p, m_p, s_s, c_s, n_s, m_s)
```
